```python
import math
import jax
import jax.numpy as jnp
from jax import lax
import numpy as np

D_MODEL = 1024
BATCH = 8
SEQ = 4096
DEPTH = 4

GRID_W = 64
CTX_LEN = 256
D_MIX = D_MODEL
D_GROUP = D_MIX // 4
N_MOD = 6

FOURIER_HEADS = 4
FOURIER_HD = D_GROUP // FOURIER_HEADS

S5_CH = 16
S5_GROUPS = D_GROUP // S5_CH
S5_STATE = 64

GLA_HEADS = 4
GLA_DK = D_GROUP // (2 * GLA_HEADS)
GLA_DV = D_GROUP // GLA_HEADS
GLA_RANK = 16
GLA_TAU = 16.0
GLA_CHUNK = 64

SSD_HEADS = 4
SSD_HD = D_GROUP // SSD_HEADS
SSD_GROUPS = 2
SSD_STATE = 64
SSD_CONV = 5
SSD_CHUNK = 64
SSD_XBC = D_GROUP + 2 * SSD_GROUPS * SSD_STATE

N_EXPERTS = 32
TOP_K = 4
D_EXPERT = D_MODEL
SWIGLU_LIMIT = 7.0
SWIGLU_ALPHA = 1.702
MOE_BLOCK = 256

EPS = 1e-6

IN_SPLIT = (D_GROUP, D_GROUP, GLA_HEADS * GLA_DK, GLA_HEADS * GLA_DK, D_GROUP, D_GROUP,
            2 * GLA_RANK, D_GROUP, SSD_XBC, 2 * SSD_HEADS)
D_IN_PROJ = (2 * D_GROUP + 2 * GLA_HEADS * GLA_DK + 2 * D_GROUP + 2 * GLA_RANK
             + D_GROUP + SSD_XBC + 2 * SSD_HEADS)

kernel_name = 'hybrid_flow_backbone_fourier_s5_gla_ssd_moe'


def _rmsnorm(x, g):
    xf = x.astype(jnp.float32)
    y = xf * lax.rsqrt(jnp.mean(xf * xf, axis=-1, keepdims=True) + EPS)
    return (y * g.astype(jnp.float32)).astype(x.dtype)


def _modulate(h, shift, scale):
    return h * (1.0 + scale) + shift


def _ada(cond, w, b):
    m = cond @ w + b
    return [t[:, None, :] for t in jnp.split(m, N_MOD, axis=-1)]


def _split_in(p):
    return jnp.split(p, np.cumsum(IN_SPLIT)[:-1].tolist(), axis=-1)


def _flip(t, rev):
    return jnp.flip(t, axis=1) if rev else t


def _tril(n):
    return jnp.tril(jnp.ones((n, n), dtype=bool))


def fourier_mixer(u, w):
    bsz, seq_len, _ = u.shape
    uh = u.astype(jnp.float32).reshape(bsz, seq_len, FOURIER_HEADS, FOURIER_HD)
    f = jnp.fft.fft2(uh, axes=(1, 3), norm='ortho').real
    return (f.reshape(bsz, seq_len, D_GROUP) @ w.astype(jnp.float32)).astype(u.dtype)


def _s5_combine(left, right):
    a_l, b_l = left
    a_r, b_r = right
    return a_r * a_l, a_r * b_l + b_r


def _s5_scan(a_bar, lam_dt, bu, h0):
    seq_len = bu.shape[1]
    a = jnp.broadcast_to(a_bar, bu.shape)
    _, h = lax.associative_scan(_s5_combine, (a, bu), axis=1)
    if h0 is None:
        return h
    steps = jnp.arange(1, seq_len + 1, dtype=jnp.float32)[:, None, None]
    return h + jnp.exp(lam_dt * steps)[None] * h0[:, None]


def s5_mixer(u_c, u_l, lam_re, lam_im, log_dt, b_re, b_im, c_re, c_im, d_skip, w_glu, ctx_out):
    f32 = jnp.float32

    def groups(u):
        return u.astype(f32).reshape(u.shape[0], u.shape[1], S5_GROUPS, S5_CH)

    uc, ul = groups(u_c), groups(u_l)
    y_c, y_l = [], []
    for r in range(2):
        rev = r == 1
        lam = lax.complex(lam_re[r].astype(f32), lam_im[r].astype(f32))
        lam_dt = lam * jnp.exp(log_dt[r].astype(f32))[:, None]
        a_bar = jnp.exp(lam_dt)
        b_bar = ((a_bar - 1.0) / lam)[..., None] * lax.complex(b_re[r].astype(f32), b_im[r].astype(f32))
        c_mat = lax.complex(c_re[r].astype(f32), c_im[r].astype(f32))
        h_c = _s5_scan(a_bar, lam_dt, jnp.einsum('blgh,gph->blgp', _flip(uc, rev), b_bar), None)
        h_l = _s5_scan(a_bar, lam_dt, jnp.einsum('blgh,gph->blgp', _flip(ul, rev), b_bar), h_c[:, -1])
        y_l.append(_flip(jnp.einsum('blgp,ghp->blgh', h_l, c_mat).real, rev))
        if ctx_out:
            y_c.append(_flip(jnp.einsum('blgp,ghp->blgh', h_c, c_mat).real, rev))

    def finish(u, u_g, ys):
        y = (ys[0] + ys[1] + d_skip.astype(f32).reshape(S5_GROUPS, S5_CH) * u_g).reshape(u.shape)
        act = jax.nn.gelu(y)
        return (act * jax.nn.sigmoid(act @ w_glu.astype(f32))).astype(u.dtype)

    return (finish(u_c, uc, y_c) if ctx_out else None), finish(u_l, ul, y_l)


def _chunk_states(k_dec, v, chunk_decay, s0):
    ds = jnp.einsum('bnchk,bnchv->bnhkv', k_dec, v)

    def step(s, inp):
        dec, d = inp
        return dec[..., None] * s + d, s

    s_final, s_enter = lax.scan(step, s0, (jnp.moveaxis(chunk_decay, 1, 0), jnp.moveaxis(ds, 1, 0)))
    return jnp.moveaxis(s_enter, 0, 1), s_final


def _gla_direction(q, k, v, log_a, s0, with_out):
    bsz, seq_len, nh, dk = q.shape
    csz = min(GLA_CHUNK, seq_len)
    nc = seq_len // csz
    q, k, v, log_a = (t.reshape(bsz, nc, csz, nh, t.shape[-1]) for t in (q, k, v, log_a))
    b = jnp.cumsum(log_a, axis=2)
    b_last = b[:, :, -1]
    s_enter, s_final = _chunk_states(k * jnp.exp(b_last[:, :, None] - b), v, jnp.exp(b_last), s0)
    if not with_out:
        return None, s_final
    q_in = q * jnp.exp(b)
    k_in = k * jnp.exp(-b)
    att = jnp.where(_tril(csz), jnp.einsum('bnthk,bnshk->bnhts', q_in, k_in), 0.0)
    o = (jnp.einsum('bnhts,bnshv->bnthv', att, v)
         + jnp.einsum('bnthk,bnhkv->bnthv', q_in, s_enter))
    return o.reshape(bsz, seq_len, nh, v.shape[-1]), s_final


def gla_mixer(pc, pl, a2, a_b, norm_g, ctx_out):
    f32 = jnp.float32

    def prep(p):
        q, k, v, r, a1 = p
        bsz, seq_len, _ = q.shape
        z = jnp.einsum('blzr,zrk->blzk', a1.astype(f32).reshape(bsz, seq_len, 2, GLA_RANK),
                       a2.astype(f32)) + a_b.astype(f32)
        log_a = (jax.nn.log_sigmoid(z) / GLA_TAU).reshape(bsz, seq_len, 2, GLA_HEADS, GLA_DK)
        qh = q.astype(f32).reshape(bsz, seq_len, GLA_HEADS, GLA_DK) * GLA_DK ** -0.5
        kh = k.astype(f32).reshape(bsz, seq_len, GLA_HEADS, GLA_DK)
        vh = v.astype(f32).reshape(bsz, seq_len, GLA_HEADS, GLA_DV)
        return qh, kh, vh, log_a, r

    qc, kc, vc, lac, rc = prep(pc)
    ql, kl, vl, lal, rl = prep(pl)
    s0 = jnp.zeros((qc.shape[0], GLA_HEADS, GLA_DK, GLA_DV), f32)
    o_c, o_l = [], []
    for r in range(2):
        rev = r == 1
        oc, s_c = _gla_direction(_flip(qc, rev), _flip(kc, rev), _flip(vc, rev), _flip(lac[:, :, r], rev), s0, ctx_out)
        ol, _ = _gla_direction(_flip(ql, rev), _flip(kl, rev), _flip(vl, rev), _flip(lal[:, :, r], rev), s_c, True)
        o_l.append(_flip(ol, rev))
        if ctx_out:
            o_c.append(_flip(oc, rev))

    def finish(os_, r):
        o = _rmsnorm(os_[0] + os_[1], norm_g)
        return (o.reshape(r.shape) * jax.nn.silu(r.astype(f32))).astype(r.dtype)

    return (finish(o_c, rc) if ctx_out else None), finish(o_l, rl)


def _dwconv(x, w, b):
    y = lax.conv_general_dilated(x, w[:, None, :], window_strides=(1,),
                                 padding=[(SSD_CONV // 2, SSD_CONV // 2)],
                                 dimension_numbers=('NWC', 'WIO', 'NWC'),
                                 feature_group_count=x.shape[-1])
    return y + b


def _ssd_direction(x, bm, cm, dt, a, s0, with_out):
    bsz, seq_len, nh, hd = x.shape
    csz = min(SSD_CHUNK, seq_len)
    nc = seq_len // csz
    x, bm, cm = (t.reshape(bsz, nc, csz, nh, t.shape[-1]) for t in (x, bm, cm))
    dt = dt.reshape(bsz, nc, csz, nh)
    cum = jnp.cumsum(dt * a, axis=2)
    xdt = x * dt[..., None]
    s_enter, s_final = _chunk_states(bm * jnp.exp(cum[:, :, -1:] - cum)[..., None], xdt,
                                     jnp.exp(cum[:, :, -1])[..., None], s0)
    if not with_out:
        return None, s_final
    seg = cum[:, :, :, None, :] - cum[:, :, None, :, :]
    decay = jnp.exp(jnp.where(_tril(csz)[..., None], seg, -jnp.inf))
    scores = jnp.einsum('bnthd,bnshd->bntsh', cm, bm) * decay
    y = (jnp.einsum('bntsh,bnshp->bnthp', scores, xdt)
         + jnp.einsum('bnthd,bnhdp->bnthp', cm * jnp.exp(cum)[..., None], s_enter))
    return y.reshape(bsz, seq_len, nh, hd), s_final


def ssd_mixer(pc, pl, conv_w, conv_b, a_log, dt_bias, d_skip, norm_g, ctx_out):
    f32 = jnp.float32
    w, bconv = conv_w.astype(f32), conv_b.astype(f32)

    def conv_ctx(t):
        return _dwconv(t, w, bconv)

    def conv_lat(t):
        bsz, seq_len, ch = t.shape
        rows = seq_len // GRID_W
        return _dwconv(t.reshape(bsz * rows, GRID_W, ch), w, bconv).reshape(bsz, seq_len, ch)

    def prep(p, conv):
        z, xbc, dt_raw = p
        bsz, seq_len, _ = z.shape
        xbc = jax.nn.silu(conv(xbc.astype(f32)))
        xs, bm, cm = jnp.split(xbc, [D_GROUP, D_GROUP + SSD_GROUPS * SSD_STATE], axis=-1)
        rep = SSD_HEADS // SSD_GROUPS

        def grp(t):
            return jnp.repeat(t.reshape(bsz, seq_len, SSD_GROUPS, SSD_STATE), rep, axis=2)

        dt = jax.nn.softplus(dt_raw.astype(f32).reshape(bsz, seq_len, 2, SSD_HEADS) + dt_bias.astype(f32))
        return xs.reshape(bsz, seq_len, SSD_HEADS, SSD_HD), grp(bm), grp(cm), dt, z

    xc, bc, cc, dtc, zc = prep(pc, conv_ctx)
    xl, bl, cl, dtl, zl = prep(pl, conv_lat)
    a = -jnp.exp(a_log.astype(f32))
    s0 = jnp.zeros((xc.shape[0], SSD_HEADS, SSD_STATE, SSD_HD), f32)
    y_c, y_l = [], []
    for r in range(2):
        rev = r == 1
        yc, s_c = _ssd_direction(_flip(xc, rev), _flip(bc, rev), _flip(cc, rev), _flip(dtc[:, :, r], rev), a[r], s0, ctx_out)
        yl, _ = _ssd_direction(_flip(xl, rev), _flip(bl, rev), _flip(cl, rev), _flip(dtl[:, :, r], rev), a[r], s_c, True)
        y_l.append(_flip(yl, rev))
        if ctx_out:
            y_c.append(_flip(yc, rev))
    dsk = d_skip.astype(f32)

    def finish(x_h, ys, z):
        y = (ys[0] + ys[1] + dsk[:, None] * x_h).reshape(z.shape)
        return _rmsnorm(y * jax.nn.silu(z.astype(f32)), norm_g).astype(z.dtype)

    return (finish(xc, y_c, zc) if ctx_out else None), finish(xl, y_l, zl)


def moe_ffn(h, router_w, router_b, w_gu, b_gu, w_dn, b_dn):
    n_tok, dm = h.shape
    logits = (h @ router_w + router_b).astype(jnp.float32)
    top_logit, top_idx = lax.top_k(logits, TOP_K)
    gates = jax.nn.softmax(top_logit, axis=-1)
    e_flat = top_idx.reshape(-1)
    order = jnp.argsort(e_flat)
    e_sorted = e_flat[order]
    tok_sorted = order // TOP_K
    counts = jnp.bincount(e_flat, length=N_EXPERTS)
    padded = (counts + MOE_BLOCK - 1) // MOE_BLOCK * MOE_BLOCK
    seg_start = jnp.cumsum(counts) - counts
    pad_end = jnp.cumsum(padded)
    dest = (pad_end - padded)[e_sorted] + jnp.arange(n_tok * TOP_K) - seg_start[e_sorted]
    n_blocks = -(-(n_tok * TOP_K) // MOE_BLOCK) + N_EXPERTS
    buf = jnp.zeros((n_blocks * MOE_BLOCK, dm), h.dtype).at[dest].set(h[tok_sorted])
    block_expert = jnp.minimum(jnp.searchsorted(pad_end, jnp.arange(n_blocks) * MOE_BLOCK, side='right'),
                               N_EXPERTS - 1)

    def run_block(args):
        xb, e = args
        gu = xb @ w_gu[e] + b_gu[e]
        gate = jnp.minimum(gu[:, 0::2], SWIGLU_LIMIT)
        up = jnp.clip(gu[:, 1::2], -SWIGLU_LIMIT, SWIGLU_LIMIT)
        act = (up + 1.0) * gate * jax.nn.sigmoid(SWIGLU_ALPHA * gate)
        return act @ w_dn[e] + b_dn[e]

    y_buf = lax.map(run_block, (buf.reshape(n_blocks, MOE_BLOCK, dm), block_expert)).reshape(-1, dm)
    y_pairs = y_buf[dest] * gates.reshape(-1)[order][:, None].astype(h.dtype)
    return jnp.zeros((n_tok, dm), h.dtype).at[tok_sorted].add(y_pairs)


def setup_inputs(seed: int = 0) -> dict:
    key = jax.random.key(seed)
    ks = jax.random.split(key, 36)
    f32 = jnp.float32
    nl = DEPTH

    def nrm(k, shape, scale):
        return jax.random.normal(k, shape, f32) * scale

    def gain(k, shape):
        return 1.0 + nrm(k, shape, 0.05)

    s5_n = jnp.arange(S5_STATE, dtype=f32)
    ssd_dt = jnp.exp(jax.random.uniform(ks[26], (nl, 2, SSD_HEADS), f32, math.log(1e-3), math.log(1e-1)))
    return {
        'x': nrm(ks[0], (BATCH, SEQ, D_MODEL), 1.0),
        'c': nrm(ks[1], (BATCH, D_MODEL), 1.0),
        'ctx': nrm(ks[2], (BATCH, CTX_LEN, D_MODEL), 1.0),
        'c_ctx': nrm(ks[3], (D_MODEL,), 1.0),
        'ada_w': nrm(ks[4], (nl, D_MODEL, N_MOD * D_MODEL), 0.5 * D_MODEL ** -0.5),
        'ada_b': nrm(ks[5], (nl, N_MOD * D_MODEL), 0.02),
        'norm_mix_g': gain(ks[6], (nl, D_MODEL)),
        'norm_ffn_g': gain(ks[7], (nl, D_MODEL)),
        'w_in': nrm(ks[8], (nl, D_MODEL, D_IN_PROJ), D_MODEL ** -0.5),
        'w_out': nrm(ks[9], (nl, D_MIX, D_MODEL), D_MIX ** -0.5),
        'four_w': nrm(ks[10], (nl, D_GROUP, D_GROUP), D_GROUP ** -0.5),
        's5_lam_re': -0.5 + nrm(ks[11], (nl, 2, S5_GROUPS, S5_STATE), 0.01),
        's5_lam_im': math.pi * s5_n + nrm(ks[12], (nl, 2, S5_GROUPS, S5_STATE), 0.01),
        's5_log_dt': jax.random.uniform(ks[13], (nl, 2, S5_GROUPS), f32, math.log(1e-3), math.log(1e-1)),
        's5_b_re': nrm(ks[14], (nl, 2, S5_GROUPS, S5_STATE, S5_CH), (2 * S5_CH) ** -0.5),
        's5_b_im': nrm(ks[15], (nl, 2, S5_GROUPS, S5_STATE, S5_CH), (2 * S5_CH) ** -0.5),
        's5_c_re': nrm(ks[16], (nl, 2, S5_GROUPS, S5_CH, S5_STATE), S5_STATE ** -0.5),
        's5_c_im': nrm(ks[17], (nl, 2, S5_GROUPS, S5_CH, S5_STATE), S5_STATE ** -0.5),
        's5_d': nrm(ks[18], (nl, D_GROUP), 1.0),
        's5_w_glu': nrm(ks[19], (nl, D_GROUP, D_GROUP), D_GROUP ** -0.5),
        'gla_a2': nrm(ks[20], (nl, 2, GLA_RANK, GLA_HEADS * GLA_DK), GLA_RANK ** -0.5),
        'gla_a_b': nrm(ks[21], (nl, 2, GLA_HEADS * GLA_DK), 0.1),
        'gla_norm_g': gain(ks[22], (nl, GLA_DV)),
        'ssd_conv_w': nrm(ks[23], (nl, SSD_CONV, SSD_XBC), SSD_CONV ** -0.5),
        'ssd_conv_b': nrm(ks[24], (nl, SSD_XBC), 0.02),
        'ssd_a_log': jnp.log(jax.random.uniform(ks[25], (nl, 2, SSD_HEADS), f32, 1.0, 16.0)),
        'ssd_dt_bias': ssd_dt + jnp.log(-jnp.expm1(-ssd_dt)),
        'ssd_d': 1.0 + nrm(ks[27], (nl, SSD_HEADS), 0.1),
        'ssd_norm_g': gain(ks[28], (nl, D_GROUP)),
        'router_w': nrm(ks[29], (nl, D_MODEL, N_EXPERTS), D_MODEL ** -0.5),
        'router_b': nrm(ks[30], (nl, N_EXPERTS), 0.01),
        'exp_w_gu': nrm(ks[31], (nl, N_EXPERTS, D_MODEL, 2 * D_EXPERT), D_MODEL ** -0.5),
        'exp_b_gu': nrm(ks[32], (nl, N_EXPERTS, 2 * D_EXPERT), 0.02),
        'exp_w_dn': nrm(ks[33], (nl, N_EXPERTS, D_EXPERT, D_MODEL), D_EXPERT ** -0.5),
        'exp_b_dn': nrm(ks[34], (nl, N_EXPERTS, D_MODEL), 0.02),
        'final_g': gain(ks[35], (D_MODEL,)),
    }


def reference(x, c, ctx, c_ctx, ada_w, ada_b, norm_mix_g, norm_ffn_g, w_in, w_out, four_w,
              s5_lam_re, s5_lam_im, s5_log_dt, s5_b_re, s5_b_im, s5_c_re, s5_c_im, s5_d, s5_w_glu,
              gla_a2, gla_a_b, gla_norm_g, ssd_conv_w, ssd_conv_b, ssd_a_log, ssd_dt_bias, ssd_d,
              ssd_norm_g, router_w, router_b, exp_w_gu, exp_b_gu, exp_w_dn, exp_b_dn, final_g):
    x_lat, x_ctx = x, ctx
    cond_lat = jax.nn.silu(c)
    cond_ctx = jax.nn.silu(c_ctx)[None, :]
    for i in range(DEPTH):
        ctx_out = i < DEPTH - 1
        m_l = _ada(cond_lat, ada_w[i], ada_b[i])
        m_c = _ada(cond_ctx, ada_w[i], ada_b[i])
        pl = _split_in(_modulate(_rmsnorm(x_lat, norm_mix_g[i]), m_l[0], m_l[1]) @ w_in[i])
        pc = _split_in(_modulate(_rmsnorm(x_ctx, norm_mix_g[i]), m_c[0], m_c[1]) @ w_in[i])
        s5_c, s5_l = s5_mixer(pc[1], pl[1], s5_lam_re[i], s5_lam_im[i], s5_log_dt[i], s5_b_re[i], s5_b_im[i],
                              s5_c_re[i], s5_c_im[i], s5_d[i], s5_w_glu[i], ctx_out)
        gla_c, gla_l = gla_mixer(pc[2:7], pl[2:7], gla_a2[i], gla_a_b[i], gla_norm_g[i], ctx_out)
        ssd_c, ssd_l = ssd_mixer(pc[7:], pl[7:], ssd_conv_w[i], ssd_conv_b[i], ssd_a_log[i], ssd_dt_bias[i],
                                 ssd_d[i], ssd_norm_g[i], ctx_out)
        mix_l = jnp.concatenate([fourier_mixer(pl[0], four_w[i]), s5_l, gla_l, ssd_l], axis=-1)
        x_lat = x_lat + m_l[2] * (mix_l @ w_out[i])
        h_l = _modulate(_rmsnorm(x_lat, norm_ffn_g[i]), m_l[3], m_l[4])
        if ctx_out:
            mix_c = jnp.concatenate([fourier_mixer(pc[0], four_w[i]), s5_c, gla_c, ssd_c], axis=-1)
            x_ctx = x_ctx + m_c[2] * (mix_c @ w_out[i])
            h_c = _modulate(_rmsnorm(x_ctx, norm_ffn_g[i]), m_c[3], m_c[4])
            n_c = h_c.shape[0] * h_c.shape[1]
            f = moe_ffn(jnp.concatenate([h_c.reshape(n_c, D_MODEL), h_l.reshape(-1, D_MODEL)], axis=0),
                        router_w[i], router_b[i], exp_w_gu[i], exp_b_gu[i], exp_w_dn[i], exp_b_dn[i])
            x_ctx = x_ctx + m_c[5] * f[:n_c].reshape(x_ctx.shape)
            f_l = f[n_c:].reshape(x_lat.shape)
        else:
            f_l = moe_ffn(h_l.reshape(-1, D_MODEL), router_w[i], router_b[i], exp_w_gu[i], exp_b_gu[i],
                          exp_w_dn[i], exp_b_dn[i]).reshape(x_lat.shape)
        x_lat = x_lat + m_l[5] * f_l
    return _rmsnorm(x_lat, final_g)
```

```python
import functools
import math

import jax
import jax.numpy as jnp
import numpy as np
from jax import lax
from jax.experimental import pallas as pl
from jax.experimental.pallas import tpu as pltpu

F32 = jnp.float32
BF16 = jnp.bfloat16
I32 = jnp.int32
U32 = jnp.uint32
HI = lax.Precision.HIGHEST

D = 1024
DG = 256
N_MOD = 6
GRID_W = 64
EPS = 1e-6

S5_CH = 16
S5_G = DG // S5_CH
S5_P = 64
S5_T = 32
S5_W = S5_T * S5_CH

GLA_H = 4
GLA_DK = 32
GLA_DV = 64
GLA_RANK = 16
GLA_TAU = 16.0
CHUNK = 64

SSD_H = 4
SSD_HD = 64
SSD_G = 2
SSD_N = 64
SSD_K = 5

N_EXP = 32
TOP_K = 4
LIMIT = 7.0
ALPHA = 1.702

TM = 256
BM = 512
LANES = 128

C_FOUR, C_S5, C_QK, C_V, C_R, C_Z, C_XBC, C_MISC = 0, 256, 512, 768, 1024, 1280, 1536, 2048
D_P = 2176
MISC_A1 = 0
MISC_DT = 32

VMEM_LIMIT = 56 * 1024 * 1024


def _cparams(n_axes, vmem=None):
    return pltpu.CompilerParams(dimension_semantics=("arbitrary",) * n_axes,
                                vmem_limit_bytes=vmem)


def _silu(x):
    return x * jax.nn.sigmoid(x)


def _rms(x, g):
    return x * lax.rsqrt(jnp.mean(x * x, axis=-1, keepdims=True) + EPS) * g


def _dot(a, b, dims=None, precision=None):
    if dims is None:
        dims = (((a.ndim - 1,), (0,)), ((), ()))
    return lax.dot_general(a, b, dims, precision=precision, preferred_element_type=F32)


_NT = (((1,), (1,)), ((), ()))
_TN = (((0,), (0,)), ((), ()))


def _pack_bf16_pairs(a):
    n = a.shape[1] // 2
    hi = pltpu.bitcast(a[:, :n].astype(BF16).astype(F32), U32)
    lo = pltpu.bitcast(a[:, n:].astype(BF16).astype(F32), U32)
    return hi | (lo >> 16)


def _unpack_bf16_pairs(w):
    hi = pltpu.bitcast(w & jnp.uint32(0xFFFF0000), F32)
    lo = pltpu.bitcast(w << 16, F32)
    return jnp.concatenate([hi, lo], axis=1)


def _ada_kernel(cond_ref, w_ref, b_ref, o_ref):
    c = _silu(cond_ref[...])
    o_ref[...] = _dot(c, w_ref[...], precision=HI) + b_ref[...]


def _ada(cond, ada_w, ada_b):
    depth = ada_w.shape[0]
    rows = cond.shape[0]
    tn = 1536
    return pl.pallas_call(
        _ada_kernel,
        grid=(depth, (N_MOD * D) // tn),
        in_specs=[pl.BlockSpec((rows, D), lambda l, n: (0, 0)),
                  pl.BlockSpec((None, D, tn), lambda l, n: (l, 0, n)),
                  pl.BlockSpec((None, 1, tn), lambda l, n: (l, 0, n))],
        out_specs=pl.BlockSpec((None, rows, tn), lambda l, n: (l, 0, n)),
        out_shape=jax.ShapeDtypeStruct((depth, rows, N_MOD * D), F32),
        compiler_params=_cparams(2, VMEM_LIMIT),
        name="ada_mod",
    )(cond, ada_w, ada_b.reshape(depth, 1, N_MOD * D))


def _pre_kernel(x_ref, mod_ref, g_ref, w_ref, cw_ref, cb_ref, csm_ref, p_ref, fl_ref, fc_ref):
    j = pl.program_id(1)
    x = x_ref[...]
    hn = _rms(x, g_ref[...]) * (1.0 + mod_ref[1:2, :]) + mod_ref[0:1, :]
    p = _dot(hn.astype(BF16), w_ref[...])
    p_ref[:, :C_XBC] = p[:, :C_XBC]
    p_ref[:, C_MISC:] = p[:, C_MISC:]

    xbc = p[:, C_XBC:C_MISC]
    period = jnp.where(j == 0, TM, GRID_W)
    pos = lax.broadcasted_iota(I32, (TM, 1), 0) & (period - 1)
    acc = jnp.zeros_like(xbc) + cb_ref[...]
    for k in range(SSD_K):
        d = k - SSD_K // 2
        sh = xbc if d == 0 else pltpu.roll(xbc, (TM - d) % TM, axis=0)
        ok = (pos + d >= 0) & (pos + d < period)
        acc = acc + jnp.where(ok, sh, 0.0) * cw_ref[k:k + 1, :]
    p_ref[:, C_XBC:C_MISC] = _silu(acc)

    xcs = _dot(p[:, C_FOUR:C_FOUR + DG].astype(BF16), csm_ref[...])

    @pl.when(j == 0)
    def _():
        fc_ref[0] = xcs[:, :DG].astype(BF16)
        fc_ref[1] = xcs[:, DG:].astype(BF16)

    @pl.when(j > 0)
    def _():
        fl_ref[0] = xcs[:, :DG].astype(BF16)
        fl_ref[1] = xcs[:, DG:].astype(BF16)


def _pre(x_all, mod, g, w_in_p, conv_w, conv_b, csm, seq, ctx):
    b, s, _ = x_all.shape
    nt = s // TM
    return pl.pallas_call(
        _pre_kernel,
        grid=(b, nt),
        in_specs=[pl.BlockSpec((None, TM, D), lambda i, j: (i, j, 0)),
                  pl.BlockSpec((None, None, 8, D), lambda i, j: (i, jnp.minimum(j, 1), 0, 0)),
                  pl.BlockSpec((1, D), lambda i, j: (0, 0)),
                  pl.BlockSpec((D, D_P), lambda i, j: (0, 0)),
                  pl.BlockSpec((8, 512), lambda i, j: (0, 0)),
                  pl.BlockSpec((1, 512), lambda i, j: (0, 0)),
                  pl.BlockSpec((DG, 2 * DG), lambda i, j: (0, 0))],
        out_specs=[pl.BlockSpec((None, TM, D_P), lambda i, j: (i, j, 0)),
                   pl.BlockSpec((None, 2, TM, DG), lambda i, j: (i, 0, jnp.maximum(j - 1, 0), 0)),
                   pl.BlockSpec((None, 2, ctx, DG), lambda i, j: (i, 0, 0, 0))],
        out_shape=[jax.ShapeDtypeStruct((b, s, D_P), F32),
                   jax.ShapeDtypeStruct((b, 2, seq, DG), BF16),
                   jax.ShapeDtypeStruct((b, 2, ctx, DG), BF16)],
        compiler_params=_cparams(2, VMEM_LIMIT),
        name="pre_mixer",
    )(x_all, mod, g, w_in_p, conv_w, conv_b, csm)


def _four_kernel(f_ref, x_ref, w_ref, o_ref):
    y = _dot(f_ref[...], x_ref[...])
    o_ref[...] = _dot(y.astype(BF16), w_ref[...])


def _fourier(fmat, x2, four_w):
    ln = fmat.shape[0]
    b = x2.shape[0]
    tf = min(512, ln)
    return pl.pallas_call(
        _four_kernel,
        grid=(ln // tf, b),
        in_specs=[pl.BlockSpec((tf, 2 * ln), lambda i, j: (i, 0)),
                  pl.BlockSpec((None, 2 * ln, DG), lambda i, j: (j, 0, 0)),
                  pl.BlockSpec((DG, DG), lambda i, j: (0, 0))],
        out_specs=pl.BlockSpec((None, tf, DG), lambda i, j: (j, i, 0)),
        out_shape=jax.ShapeDtypeStruct((b, ln, DG), F32),
        compiler_params=_cparams(2, VMEM_LIMIT),
        name="fourier_dft",
    )(fmat, x2, four_w)


def _dft_matrix(ln):
    f = jnp.arange(ln, dtype=I32)[:, None]
    t = jnp.arange(ln, dtype=I32)[None, :]
    ang = ((f * t) % ln).astype(F32) * (2.0 * math.pi / ln)
    scale = 1.0 / math.sqrt(ln * 64.0)
    return (jnp.concatenate([jnp.cos(ang), jnp.sin(ang)], axis=1) * scale).astype(BF16)


def _channel_dft():
    i = np.arange(DG)
    same = (i[:, None] // 64) == (i[None, :] // 64)
    ang = 2.0 * np.pi * ((i[:, None] % 64) * (i[None, :] % 64) % 64) / 64.0
    c = np.where(same, np.cos(ang), 0.0)
    s = np.where(same, np.sin(ang), 0.0)
    return jnp.asarray(np.concatenate([c, -s], axis=1), dtype=BF16)


def _s5_kernel(u_ref, p_ref, mq_ref, a_ref, y_ref, hl_ref, he_ref, *, nb, nch, ncc):
    u = u_ref[...].astype(BF16)
    hl_ref[...] = _dot(u, p_ref[...])
    a1f, a2f = a_ref[0:1, :LANES], a_ref[1:2, :LANES]
    a1b, a2b = a_ref[0:1, LANES:], a_ref[1:2, LANES:]

    def step(i, carry):
        hf, hb = carry
        rf = pl.multiple_of(i * nb, nb)
        nbk = jnp.where(i < ncc, ncc - 1 - i, nch + ncc - 1 - i)
        rb = pl.multiple_of(nbk * nb, nb)
        he_ref[pl.ds(rf, nb), :LANES] = hf
        he_ref[pl.ds(rb, nb), LANES:] = hb
        hf = a1f * hf + a2f * pltpu.roll(hf, 64, axis=1) + hl_ref[pl.ds(rf, nb), :LANES]
        hb = a1b * hb + a2b * pltpu.roll(hb, 64, axis=1) + hl_ref[pl.ds(rb, nb), LANES:]
        return hf, hb

    z = jnp.zeros((nb, LANES), F32)
    lax.fori_loop(0, nch, step, (z, z))
    y_ref[...] = (_dot(u, mq_ref[:S5_W, :])
                  + _dot(he_ref[...].astype(BF16), mq_ref[S5_W:, :]))


def _s5(u_g, ptab, mq, atab, nb, nch, ncc):
    g, r, _ = u_g.shape
    return pl.pallas_call(
        functools.partial(_s5_kernel, nb=nb, nch=nch, ncc=ncc),
        grid=(g,),
        in_specs=[pl.BlockSpec((None, r, S5_W), lambda i: (i, 0, 0)),
                  pl.BlockSpec((None, S5_W, 2 * LANES), lambda i: (i, 0, 0)),
                  pl.BlockSpec((None, S5_W + 2 * LANES, S5_W), lambda i: (i, 0, 0)),
                  pl.BlockSpec((None, 8, 2 * LANES), lambda i: (i, 0, 0))],
        out_specs=pl.BlockSpec((None, r, S5_W), lambda i: (i, 0, 0)),
        out_shape=jax.ShapeDtypeStruct((g, r, S5_W), F32),
        scratch_shapes=[pltpu.VMEM((r, 2 * LANES), F32), pltpu.VMEM((r, 2 * LANES), F32)],
        compiler_params=_cparams(1, VMEM_LIMIT),
        name="s5_mixer",
    )(u_g, ptab, mq, atab)


def _s5_tables(lam_re, lam_im, log_dt, b_re, b_im, c_re, c_im):
    t = S5_T
    dt = jnp.exp(log_dt)[..., None]
    ldr, ldi = lam_re * dt, lam_im * dt
    jj = jnp.arange(t + 1, dtype=F32)[None, None, :, None]
    mag = jnp.exp(ldr[:, :, None, :] * jj)
    pr, pi = mag * jnp.cos(ldi[:, :, None, :] * jj), mag * jnp.sin(ldi[:, :, None, :] * jj)
    ar, ai = pr[:, :, 1], pi[:, :, 1]
    den = lam_re * lam_re + lam_im * lam_im
    qr = ((ar - 1.0) * lam_re + ai * lam_im) / den
    qi = (ai * lam_re - (ar - 1.0) * lam_im) / den
    bbr = qr[..., None] * b_re - qi[..., None] * b_im
    bbi = qr[..., None] * b_im + qi[..., None] * b_re
    cpr = c_re[:, :, None] * pr[:, :, :, None, :] - c_im[:, :, None] * pi[:, :, :, None, :]
    cpi = c_re[:, :, None] * pi[:, :, :, None, :] + c_im[:, :, None] * pr[:, :, :, None, :]
    klag = (jnp.einsum('rgjop,rgph->rgjoh', cpr[:, :, :t], bbr, precision=HI)
            - jnp.einsum('rgjop,rgph->rgjoh', cpi[:, :, :t], bbi, precision=HI))
    s_i = jnp.arange(t)[:, None]
    t_i = jnp.arange(t)[None, :]

    def toeplitz(k, d):
        m = k[:, jnp.clip(d, 0, t - 1)]
        m = jnp.where((d >= 0)[None, :, :, None, None], m, 0.0)
        return jnp.transpose(m, (0, 1, 4, 2, 3))

    m_sum = (toeplitz(klag[0], t_i - s_i) + toeplitz(klag[1], s_i - t_i)).reshape(S5_G, S5_W, S5_W)
    pbr = pr[..., None] * bbr[:, :, None] - pi[..., None] * bbi[:, :, None]
    pbi = pr[..., None] * bbi[:, :, None] + pi[..., None] * bbr[:, :, None]

    def ptab(r, exps):
        re = jnp.transpose(pbr[r][:, exps], (0, 1, 3, 2))
        im = jnp.transpose(pbi[r][:, exps], (0, 1, 3, 2))
        return jnp.concatenate([re, im], axis=-1).reshape(S5_G, S5_W, 2 * S5_P)

    ar_t = np.arange(t)
    p2 = jnp.concatenate([ptab(0, t - 1 - ar_t), ptab(1, ar_t)], axis=-1)

    def qtab(r, exps):
        re = jnp.transpose(cpr[r][:, exps], (0, 3, 1, 2)).reshape(S5_G, S5_P, S5_W)
        im = jnp.transpose(cpi[r][:, exps], (0, 3, 1, 2)).reshape(S5_G, S5_P, S5_W)
        return jnp.concatenate([re, -im], axis=1)

    mq = jnp.concatenate([m_sum, qtab(0, ar_t + 1), qtab(1, t - ar_t)], axis=1)
    art, ait = pr[:, :, t], pi[:, :, t]
    a1 = jnp.concatenate([art[0], art[0], art[1], art[1]], axis=-1)
    a2 = jnp.concatenate([-ait[0], ait[0], -ait[1], ait[1]], axis=-1)
    atab = jnp.concatenate([a1[:, None], a2[:, None], jnp.zeros((S5_G, 6, 4 * S5_P), F32)], axis=1)
    return p2.astype(BF16), mq.astype(BF16), atab


def _chunk_mask(r):
    t = lax.broadcasted_iota(I32, (CHUNK, CHUNK), 0)
    s = lax.broadcasted_iota(I32, (CHUNK, CHUNK), 1)
    return (t - s) * (1 - 2 * r) >= 0


def _gla_kernel(qk_ref, v_ref, misc_ref, a2_ref, ab_ref, o_ref, st_ref):
    r = pl.program_id(1)
    j = pl.program_id(2)

    @pl.when(j == 0)
    def _():
        st_ref[...] = jnp.zeros_like(st_ref)

    msk = _chunk_mask(r)
    mskf = msk.astype(F32)
    msk4 = jnp.concatenate([mskf] * GLA_H, axis=0)
    lane_k = lax.broadcasted_iota(I32, (1, 2 * CHUNK), 1) // GLA_DK
    lane_v = lax.broadcasted_iota(I32, (1, DG), 1) // GLA_DV

    for c in range(TM // CHUNK):
        cc = jnp.where(r == 0, c, TM // CHUNK - 1 - c)
        r0 = pl.multiple_of(cc * CHUNK, CHUNK)
        zc = _dot(misc_ref[pl.ds(r0, CHUNK), :], a2_ref[...], precision=HI) + ab_ref[...]
        la = -(jnp.maximum(-zc, 0.0) + jnp.log1p(jnp.exp(-jnp.abs(zc)))) * (1.0 / GLA_TAU)
        q = qk_ref[pl.ds(r0, CHUNK), :LANES] * (GLA_DK ** -0.5)
        k = qk_ref[pl.ds(r0, CHUNK), LANES:]
        v = v_ref[pl.ds(r0, CHUNK), :]
        bcum = _dot(mskf, la, precision=HI)
        tot = jnp.sum(la, axis=0, keepdims=True)
        q_in = q * jnp.exp(bcum)
        k_in = k * jnp.exp(-bcum)
        k_dec = k * jnp.exp(tot - bcum)
        qm = jnp.concatenate([jnp.where(lane_k == h, q_in, 0.0) for h in range(GLA_H)], axis=0)
        qmb = qm.astype(BF16)
        att = _dot(qmb, k_in.astype(BF16), _NT) * msk4
        st = st_ref[...]
        os_ = _dot(att.astype(BF16), v.astype(BF16)) + _dot(qmb, st.astype(BF16), _NT)
        o = jnp.zeros((CHUNK, DG), F32)
        for h in range(GLA_H):
            o = o + jnp.where(lane_v == h, os_[h * CHUNK:(h + 1) * CHUNK, :], 0.0)
        o_ref[pl.ds(r0, CHUNK), :] = o
        st_ref[...] = st * jnp.exp(tot) + _dot(v.astype(BF16), k_dec.astype(BF16), _TN)


def _tile_order(r, j, nt):
    return jnp.where((r == 0) | (j == 0), j, nt - j)


def _gla(p, a2m, abm):
    b, s, _ = p.shape
    nt = s // TM
    return pl.pallas_call(
        _gla_kernel,
        grid=(b, 2, nt),
        in_specs=[pl.BlockSpec((None, TM, DG), lambda i, r, j: (i, _tile_order(r, j, nt), C_QK // DG)),
                  pl.BlockSpec((None, TM, DG), lambda i, r, j: (i, _tile_order(r, j, nt), C_V // DG)),
                  pl.BlockSpec((None, TM, LANES), lambda i, r, j: (i, _tile_order(r, j, nt), C_MISC // LANES)),
                  pl.BlockSpec((None, LANES, LANES), lambda i, r, j: (r, 0, 0)),
                  pl.BlockSpec((None, 1, LANES), lambda i, r, j: (r, 0, 0))],
        out_specs=pl.BlockSpec((None, None, TM, DG), lambda i, r, j: (r, i, _tile_order(r, j, nt), 0)),
        out_shape=jax.ShapeDtypeStruct((2, b, s, DG), F32),
        scratch_shapes=[pltpu.VMEM((DG, LANES), F32)],
        compiler_params=_cparams(3, VMEM_LIMIT),
        name="gla_mixer",
    )(p, p, p, a2m, abm)


def _ssd_kernel(xbc_ref, misc_ref, dtb_ref, eh_ref, ef_ref, eb_ref, av_ref, y_ref, sb_ref):
    r = pl.program_id(1)
    j = pl.program_id(2)

    @pl.when(j == 0)
    def _():
        sb_ref[...] = jnp.zeros_like(sb_ref)

    msk = _chunk_mask(r)
    mskf = msk.astype(F32)
    mskt = (~msk | (lax.broadcasted_iota(I32, (CHUNK, CHUNK), 0)
                    == lax.broadcasted_iota(I32, (CHUNK, CHUNK), 1))).astype(F32)
    row_h = lax.broadcasted_iota(I32, (DG, 1), 0) // SSD_HD
    lane_h = lax.broadcasted_iota(I32, (1, DG), 1) // SSD_HD
    blockmask = row_h == lane_h
    lane_g = lax.broadcasted_iota(I32, (1, LANES), 1) // SSD_N

    for c in range(TM // CHUNK):
        cc = jnp.where(r == 0, c, TM // CHUNK - 1 - c)
        r0 = pl.multiple_of(cc * CHUNK, CHUNK)
        zm = misc_ref[pl.ds(r0, CHUNK), :] + dtb_ref[...]
        spc = jnp.maximum(zm, 0.0) + jnp.log1p(jnp.exp(-jnp.abs(zm)))
        dt = _dot(spc, ef_ref[...], precision=HI)
        da = dt * av_ref[...]
        bmc = xbc_ref[pl.ds(r0, CHUNK), DG:DG + LANES]
        cmc = xbc_ref[pl.ds(r0, CHUNK), DG + LANES:]
        bt = _dot(bmc, eb_ref[...], precision=HI)
        ct = _dot(cmc, eb_ref[...], precision=HI)
        xdt = xbc_ref[pl.ds(r0, CHUNK), :DG] * dt
        cum = _dot(mskf, da, precision=HI)
        cum_t = _dot(da, mskt, _TN, precision=HI)
        tot = jnp.sum(da, axis=0, keepdims=True)
        cb = [_dot(jnp.where(lane_g == g, cmc, 0.0).astype(BF16), bmc.astype(BF16), _NT)
              for g in range(SSD_G)]
        sc = []
        for h in range(SSD_H):
            dah = _dot(spc, eh_ref[h], precision=HI)
            cumcol = _dot(mskf, dah, precision=HI)
            seg = cumcol - cum_t[h * SSD_HD:(h + 1) * SSD_HD, :]
            lh = jnp.where(msk, jnp.exp(jnp.minimum(seg, 0.0)), 0.0)
            sc.append(cb[h // (SSD_H // SSD_G)] * lh)
        scs = jnp.concatenate(sc, axis=0)
        xb = xdt.astype(BF16)
        ys = _dot(scs.astype(BF16), xb)
        y = _dot((ct * jnp.exp(cum)).astype(BF16), sb_ref[...].astype(BF16))
        for h in range(SSD_H):
            y = y + jnp.where(lane_h == h, ys[h * CHUNK:(h + 1) * CHUNK, :], 0.0)
        y_ref[pl.ds(r0, CHUNK), :] = y
        bdec = bt * jnp.exp(tot - cum)
        upd = _dot(bdec.astype(BF16), xb, _TN)
        sb_ref[...] = sb_ref[...] * jnp.exp(tot) + jnp.where(blockmask, upd, 0.0)


def _ssd(p, dtb, eh, ef, eb, av):
    b, s, _ = p.shape
    nt = s // TM
    return pl.pallas_call(
        _ssd_kernel,
        grid=(b, 2, nt),
        in_specs=[pl.BlockSpec((None, TM, 512), lambda i, r, j: (i, _tile_order(r, j, nt), C_XBC // 512)),
                  pl.BlockSpec((None, TM, LANES), lambda i, r, j: (i, _tile_order(r, j, nt), C_MISC // LANES)),
                  pl.BlockSpec((None, 1, LANES), lambda i, r, j: (r, 0, 0)),
                  pl.BlockSpec((None, SSD_H, LANES, SSD_HD), lambda i, r, j: (r, 0, 0, 0)),
                  pl.BlockSpec((None, LANES, DG), lambda i, r, j: (r, 0, 0)),
                  pl.BlockSpec((LANES, DG), lambda i, r, j: (0, 0)),
                  pl.BlockSpec((None, 1, DG), lambda i, r, j: (r, 0, 0))],
        out_specs=pl.BlockSpec((None, None, TM, DG), lambda i, r, j: (r, i, _tile_order(r, j, nt), 0)),
        out_shape=jax.ShapeDtypeStruct((2, b, s, DG), F32),
        scratch_shapes=[pltpu.VMEM((DG, DG), F32)],
        compiler_params=_cparams(3, VMEM_LIMIT),
        name="ssd_mixer",
    )(p, p, dtb, eh, ef, eb, av)


def _ssd_tables(a_log, dt_bias):
    eh = np.zeros((2, SSD_H, LANES, SSD_HD), np.float32)
    ef = np.zeros((2, LANES, DG), np.float32)
    for r in range(2):
        for h in range(SSD_H):
            eh[r, h, MISC_DT + SSD_H * r + h, :] = 1.0
            ef[r, MISC_DT + SSD_H * r + h, h * SSD_HD:(h + 1) * SSD_HD] = 1.0
    eb = np.zeros((LANES, DG), np.float32)
    for h in range(SSD_H):
        g = h // (SSD_H // SSD_G)
        eb[g * SSD_N + np.arange(SSD_N), h * SSD_N + np.arange(SSD_N)] = 1.0
    dtb = jnp.zeros((2, 1, LANES), F32)
    for r in range(2):
        dtb = dtb.at[r, 0, MISC_DT + SSD_H * r:MISC_DT + SSD_H * (r + 1)].set(dt_bias[r])
    a = -jnp.exp(a_log)
    av = jnp.repeat(a, SSD_HD, axis=-1)[:, None, :]
    eha = jnp.asarray(eh) * a[:, :, None, None]
    return dtb, eha, jnp.asarray(ef), jnp.asarray(eb), av


def _gelu_tanh(x):
    return 0.5 * x * (1.0 + jnp.tanh(math.sqrt(2.0 / math.pi) * (x + 0.044715 * x * x * x)))


def _post_kernel(x_ref, mod_ref, us5_ref, r_ref, z_ref, xs_ref, fl_ref, fc_ref, s5y_ref,
                 gf_ref, gb_ref, sf_ref, sbk_ref, vec_ref, wglu_ref, avg_ref, wout_ref,
                 gffn_ref, rw_ref, rb_ref, tri_ref,
                 xo_ref, hp_ref, ri_ref, rg_ref, cnt_ref, cnt_sc):
    i = pl.program_id(0)
    j = pl.program_id(1)

    @pl.when((i == 0) & (j == 0))
    def _():
        cnt_sc[...] = jnp.zeros_like(cnt_sc)

    s5_d, gla_g, ssd_d, ssd_g = vec_ref[0:1, :], vec_ref[1:2, :], vec_ref[2:3, :], vec_ref[3:4, :]
    four = jnp.where(j == 0, fc_ref[...], fl_ref[...])
    y5 = s5y_ref[...] + s5_d * us5_ref[...]
    act = _gelu_tanh(y5)
    s5o = act * jax.nn.sigmoid(_dot(act.astype(BF16), wglu_ref[...]))
    og = gf_ref[...] + gb_ref[...]
    ms = _dot(og * og, avg_ref[...], precision=HI)
    glo = og * lax.rsqrt(ms + EPS) * gla_g * _silu(r_ref[...])
    ys = (sf_ref[...] + sbk_ref[...] + ssd_d * xs_ref[...]) * _silu(z_ref[...])
    sso = _rms(ys, ssd_g)
    mix = jnp.concatenate([four, s5o, glo, sso], axis=1).astype(BF16)
    x = x_ref[...] + mod_ref[2:3, :] * _dot(mix, wout_ref[...])
    xo_ref[...] = x
    h = _rms(x, gffn_ref[...]) * (1.0 + mod_ref[4:5, :]) + mod_ref[3:4, :]
    hp_ref[...] = _pack_bf16_pairs(h)

    logits = _dot(h, rw_ref[...], precision=HI) + rb_ref[...]
    lane = lax.broadcasted_iota(I32, (TM, LANES), 1)
    vals, idxs = [], []
    for _k in range(TOP_K):
        mx = jnp.max(logits, axis=1, keepdims=True)
        ix = jnp.min(jnp.where(logits == mx, lane, LANES), axis=1, keepdims=True)
        vals.append(mx)
        idxs.append(ix)
        logits = jnp.where(lane == ix, -jnp.inf, logits)
    es = [jnp.exp(v - vals[0]) for v in vals]
    den = es[0] + es[1] + es[2] + es[3]
    onehots = [(lane == ix) for ix in idxs]
    oh = sum(o.astype(F32) for o in onehots)
    before = _dot(tri_ref[...], oh.astype(BF16)) + cnt_sc[0:1, :]
    ri = jnp.zeros((TM, LANES), I32)
    rg = jnp.zeros((TM, LANES), F32)
    for k in range(TOP_K):
        rank = jnp.sum(jnp.where(onehots[k], before, 0.0), axis=1, keepdims=True).astype(I32)
        ri = jnp.where(lane == k, idxs[k], ri)
        ri = jnp.where(lane == TOP_K + k, rank, ri)
        rg = jnp.where(lane == k, es[k] / den, rg)
    ri_ref[...] = ri
    rg_ref[...] = rg
    cnt_sc[...] = cnt_sc[...] + jnp.sum(oh, axis=0, keepdims=True)
    cnt_ref[...] = cnt_sc[...]


def _post(x_all, mod, p, four_l, four_c, s5y, gla_o, ssd_y, vecs, wglu, avg, wout, gffn, rw, rb, tri,
          seq, ctx):
    b, s, _ = x_all.shape
    nt = s // TM
    tile = lambda w, col: pl.BlockSpec((None, TM, w), lambda i, j: (i, j, col))
    dirspec = lambda r: pl.BlockSpec((None, None, TM, DG), lambda i, j: (r, i, j, 0))
    full = lambda a: pl.BlockSpec(a.shape, lambda i, j: (0,) * a.ndim)
    return pl.pallas_call(
        _post_kernel,
        grid=(b, nt),
        in_specs=[tile(D, 0),
                  pl.BlockSpec((None, None, 8, D), lambda i, j: (i, jnp.minimum(j, 1), 0, 0)),
                  tile(DG, C_S5 // DG), tile(DG, C_R // DG), tile(DG, C_Z // DG), tile(DG, C_XBC // DG),
                  pl.BlockSpec((None, TM, DG), lambda i, j: (i, jnp.maximum(j - 1, 0), 0)),
                  pl.BlockSpec((None, ctx, DG), lambda i, j: (i, 0, 0)),
                  tile(DG, 0),
                  dirspec(0), dirspec(1), dirspec(0), dirspec(1),
                  full(vecs), full(wglu), full(avg), full(wout), full(gffn), full(rw), full(rb), full(tri)],
        out_specs=[tile(D, 0), tile(D // 2, 0), tile(LANES, 0), tile(LANES, 0),
                   pl.BlockSpec((8, LANES), lambda i, j: (0, 0))],
        out_shape=[jax.ShapeDtypeStruct((b, s, D), F32),
                   jax.ShapeDtypeStruct((b, s, D // 2), U32),
                   jax.ShapeDtypeStruct((b, s, LANES), I32),
                   jax.ShapeDtypeStruct((b, s, LANES), F32),
                   jax.ShapeDtypeStruct((8, LANES), F32)],
        scratch_shapes=[pltpu.VMEM((8, LANES), F32)],
        compiler_params=_cparams(2, VMEM_LIMIT),
        name="post_mixer_router",
    )(x_all, mod, p, p, p, p, four_l, four_c, s5y, gla_o, gla_o, ssd_y, ssd_y,
      vecs, wglu, avg, wout, gffn, rw, rb, tri)


def _dispatch_kernel(dest_ref, h_ref, buf_in_ref, buf_ref, sem):
    del buf_in_ref

    def row_copy(t, d):
        return pltpu.make_async_copy(h_ref.at[pl.ds(t, 1)], buf_ref.at[pl.ds(d, 1)], sem)

    def issue(t, c):
        for k in range(TOP_K):
            row_copy(t, dest_ref[0, 0, t * TOP_K + k]).start()
        return c

    lax.fori_loop(0, TM, issue, 0)

    def drain(n, c):
        row_copy(0, 0).wait()
        return c

    lax.fori_loop(0, TM * TOP_K, drain, 0)


def _dispatch(dest3, hp, buf0):
    n = hp.shape[0]
    return pl.pallas_call(
        _dispatch_kernel,
        grid=(n // TM,),
        in_specs=[pl.BlockSpec((1, 1, TM * TOP_K), lambda i: (i, 0, 0), memory_space=pltpu.SMEM),
                  pl.BlockSpec((TM, D // 2), lambda i: (i, 0)),
                  pl.BlockSpec(memory_space=pl.ANY)],
        out_specs=pl.BlockSpec(memory_space=pl.ANY),
        out_shape=jax.ShapeDtypeStruct(buf0.shape, U32),
        scratch_shapes=[pltpu.SemaphoreType.DMA(())],
        input_output_aliases={2: 0},
        compiler_params=_cparams(1, VMEM_LIMIT),
        name="moe_dispatch",
    )(dest3, hp, buf0)


def _expert_kernel(be_ref, nu_ref, x_ref, wg_ref, wu_ref, wd_ref, bg_ref, bu_ref, bd_ref, y_ref):
    @pl.when(pl.program_id(0) < nu_ref[0])
    def _():
        x = _unpack_bf16_pairs(x_ref[...]).astype(BF16)
        g = _dot(x, wg_ref[...]) + bg_ref[...]
        u = _dot(x, wu_ref[...]) + bu_ref[...]
        gate = jnp.minimum(g, LIMIT)
        up = jnp.clip(u, -LIMIT, LIMIT)
        act = (up + 1.0) * gate * jax.nn.sigmoid(ALPHA * gate)
        y = _dot(act.astype(BF16), wd_ref[...]) + bd_ref[...]
        y_ref[...] = _pack_bf16_pairs(y)

    @pl.when(pl.program_id(0) >= nu_ref[0])
    def _():
        y_ref[...] = jnp.zeros_like(y_ref)


def _experts(blk_e, n_used, buf, wg, wu, wd, bg, bu, bd):
    nblk = buf.shape[0] // BM
    row = lambda i, be, nu: (jnp.minimum(i, nu[0] - 1), 0)
    wsel = lambda i, be, nu: (be[jnp.minimum(i, nu[0] - 1)], 0, 0)
    grid_spec = pltpu.PrefetchScalarGridSpec(
        num_scalar_prefetch=2,
        grid=(nblk,),
        in_specs=[pl.BlockSpec((BM, D // 2), row),
                  pl.BlockSpec((None, D, D), wsel), pl.BlockSpec((None, D, D), wsel),
                  pl.BlockSpec((None, D, D), wsel),
                  pl.BlockSpec((None, 1, D), wsel), pl.BlockSpec((None, 1, D), wsel),
                  pl.BlockSpec((None, 1, D), wsel)],
        out_specs=pl.BlockSpec((BM, D // 2), lambda i, be, nu: (i, 0)),
    )
    return pl.pallas_call(
        _expert_kernel,
        grid_spec=grid_spec,
        out_shape=jax.ShapeDtypeStruct(buf.shape, U32),
        compiler_params=_cparams(1, VMEM_LIMIT),
        name="moe_experts",
    )(blk_e, n_used, buf, wg, wu, wd, bg, bu, bd)


def _combine_kernel(dc_ref, dn_ref, x_ref, rg_ref, mod_ref, fg_ref, yb_ref, o_ref, rows, sem, *, final):
    i = pl.program_id(0)
    n = pl.num_programs(0)
    slot = i % 2

    def row_copy(d, sl, k, t):
        return pltpu.make_async_copy(yb_ref.at[pl.ds(d, 1)], rows.at[sl, k, pl.ds(t, 1)], sem.at[sl])

    def gather(dref, sl):
        def issue(t, c):
            for k in range(TOP_K):
                row_copy(dref[0, 0, t * TOP_K + k], sl, k, t).start()
            return c
        lax.fori_loop(0, TM, issue, 0)

    @pl.when(i == 0)
    def _():
        gather(dc_ref, 0)

    @pl.when(i + 1 < n)
    def _():
        gather(dn_ref, 1 - slot)

    def drain(t, c):
        row_copy(0, slot, 0, 0).wait()
        return c

    lax.fori_loop(0, TM * TOP_K, drain, 0)

    f = jnp.zeros((TM, D), F32)
    for k in range(TOP_K):
        f = f + rg_ref[:, k:k + 1] * _unpack_bf16_pairs(rows[slot, k])
    x = x_ref[...] + mod_ref[5:6, :] * f
    o_ref[...] = _rms(x, fg_ref[...]) if final else x


def _combine(dest3, x_flat, rg_flat, mod, fg, ybuf, nt, final):
    n = x_flat.shape[0]
    steps = n // TM
    dspec = lambda f: pl.BlockSpec((1, 1, TM * TOP_K), f, memory_space=pltpu.SMEM)
    return pl.pallas_call(
        functools.partial(_combine_kernel, final=final),
        grid=(steps,),
        in_specs=[dspec(lambda i: (i, 0, 0)),
                  dspec(lambda i: (jnp.minimum(i + 1, steps - 1), 0, 0)),
                  pl.BlockSpec((TM, D), lambda i: (i, 0)),
                  pl.BlockSpec((TM, LANES), lambda i: (i, 0)),
                  pl.BlockSpec((None, None, 8, D), lambda i: (i // nt, jnp.minimum(i % nt, 1), 0, 0)),
                  pl.BlockSpec((1, D), lambda i: (0, 0)),
                  pl.BlockSpec(memory_space=pl.ANY)],
        out_specs=pl.BlockSpec((TM, D), lambda i: (i, 0)),
        out_shape=jax.ShapeDtypeStruct((n, D), F32),
        scratch_shapes=[pltpu.VMEM((2, TOP_K, TM, D // 2), U32), pltpu.SemaphoreType.DMA((2,))],
        compiler_params=_cparams(1, VMEM_LIMIT),
        name="moe_combine",
    )(dest3, dest3, x_flat, rg_flat, mod, fg, ybuf)


def _repack_w_in(w):
    four, s5, q, k, v, r, a1, z, xbc, dt = jnp.split(
        w, np.cumsum([256, 256, 128, 128, 256, 256, 32, 256, 512, 8])[:-1].tolist(), axis=-1)
    misc = jnp.concatenate([a1, dt, jnp.zeros((D, LANES - 40), w.dtype)], axis=-1)
    return jnp.concatenate([four, s5, q, k, v, r, z, xbc, misc], axis=-1).astype(BF16)


def kernel(x, c, ctx, c_ctx, ada_w, ada_b, norm_mix_g, norm_ffn_g, w_in, w_out, four_w, s5_lam_re, s5_lam_im, s5_log_dt, s5_b_re, s5_b_im, s5_c_re, s5_c_im, s5_d, s5_w_glu, gla_a2, gla_a_b, gla_norm_g, ssd_conv_w, ssd_conv_b, ssd_a_log, ssd_dt_bias, ssd_d, ssd_norm_g, router_w, router_b, exp_w_gu, exp_b_gu, exp_w_dn, exp_b_dn, final_g):
    bsz, seq, _ = x.shape
    clen = ctx.shape[1]
    depth = ada_w.shape[0]
    assert clen == TM and seq % TM == 0 and bsz == 8
    s = clen + seq
    nt = s // TM
    n_tok = bsz * s
    nch, ncc = s // S5_T, clen // S5_T

    cond = jnp.concatenate([c, c_ctx[None, :], jnp.zeros((16 - bsz - 1, D), F32)], axis=0)
    mods = _ada(cond, ada_w, ada_b).reshape(depth, 16, N_MOD, D)
    m_lat = mods[:, :bsz]
    m_ctx = jnp.broadcast_to(mods[:, bsz:bsz + 1], m_lat.shape)
    mods = jnp.stack([m_ctx, m_lat], axis=2)
    mods = jnp.pad(mods, ((0, 0), (0, 0), (0, 0), (0, 2), (0, 0)))

    x_all = jnp.concatenate([ctx, x], axis=1)
    csm = _channel_dft()
    f_lat, f_ctx = _dft_matrix(seq), _dft_matrix(clen)
    avg = jnp.asarray(np.kron(np.eye(GLA_H), np.full((GLA_DV, GLA_DV), 1.0 / GLA_DV)), F32)
    tri = jnp.asarray(np.tril(np.ones((TM, TM)), -1), BF16)

    n_rows = n_tok * TOP_K + N_EXP * BM
    nblk = n_rows // BM

    for l in range(depth):
        mod = mods[l]
        w_in_p = _repack_w_in(w_in[l])
        conv_w = jnp.pad(ssd_conv_w[l], ((0, 8 - SSD_K), (0, 0)))
        p, four_l, four_c = _pre(x_all, mod, norm_mix_g[l][None], w_in_p, conv_w, ssd_conv_b[l][None],
                                 csm, seq, clen)
        fw = four_w[l].astype(BF16)
        fy_l = _fourier(f_lat, four_l.reshape(bsz, 2 * seq, DG), fw)
        fy_c = _fourier(f_ctx, four_c.reshape(bsz, 2 * clen, DG), fw)

        ptab, mq, atab = _s5_tables(s5_lam_re[l], s5_lam_im[l], s5_log_dt[l], s5_b_re[l], s5_b_im[l],
                                    s5_c_re[l], s5_c_im[l])
        u5 = p[:, :, C_S5:C_S5 + DG].reshape(bsz, nch, S5_T, S5_G, S5_CH)
        u5 = jnp.transpose(u5, (3, 1, 0, 2, 4)).reshape(S5_G, nch * bsz, S5_W)
        y5 = _s5(u5, ptab, mq, atab, bsz, nch, ncc).reshape(S5_G, nch, bsz, S5_T, S5_CH)
        y5 = jnp.transpose(y5, (2, 1, 3, 0, 4)).reshape(bsz, s, DG)

        a2m = jnp.zeros((2, LANES, LANES), F32)
        a2m = a2m.at[0, :GLA_RANK].set(gla_a2[l, 0]).at[1, GLA_RANK:2 * GLA_RANK].set(gla_a2[l, 1])
        gla_o = _gla(p, a2m, gla_a_b[l][:, None, :])

        ssd_y = _ssd(p, *_ssd_tables(ssd_a_log[l], ssd_dt_bias[l]))

        vecs = jnp.stack([s5_d[l], jnp.tile(gla_norm_g[l], GLA_H), jnp.repeat(ssd_d[l], SSD_HD),
                          ssd_norm_g[l]] + [jnp.zeros((DG,), F32)] * 4)
        rw = jnp.pad(router_w[l], ((0, 0), (0, LANES - N_EXP)))
        rb = jnp.concatenate([router_b[l], jnp.full((LANES - N_EXP,), -1e30, F32)])[None]
        x_mid, hp, ri, rg, cnt = _post(x_all, mod, p, fy_l, fy_c, y5, gla_o, ssd_y, vecs,
                                       s5_w_glu[l].astype(BF16), avg, w_out[l].astype(BF16),
                                       norm_ffn_g[l][None], rw, rb, tri, seq, clen)

        counts = cnt[0, :N_EXP].astype(I32)
        padded = (counts + BM - 1) // BM * BM
        pad_end = jnp.cumsum(padded)
        pad_start = pad_end - padded
        ri = ri.reshape(n_tok, LANES)
        dest = pad_start[ri[:, :TOP_K]] + ri[:, TOP_K:2 * TOP_K]
        dest3 = dest.reshape(n_tok // TM, 1, TM * TOP_K)
        blk_e = jnp.minimum(jnp.searchsorted(pad_end, jnp.arange(nblk, dtype=I32) * BM, side='right'),
                            N_EXP - 1).astype(I32)
        n_used = (pad_end[-1] // BM).astype(I32)[None]

        buf = _dispatch(dest3, hp.reshape(n_tok, D // 2), jnp.zeros((n_rows, D // 2), U32))
        wgu = exp_w_gu[l].reshape(N_EXP, D, D, 2)
        bgu = exp_b_gu[l].reshape(N_EXP, 1, D, 2)
        ybuf = _experts(blk_e, n_used, buf, wgu[..., 0].astype(BF16), wgu[..., 1].astype(BF16),
                        exp_w_dn[l].astype(BF16), bgu[..., 0], bgu[..., 1], exp_b_dn[l][:, None, :])
        x_all = _combine(dest3, x_mid.reshape(n_tok, D), rg.reshape(n_tok, LANES), mod, final_g[None],
                         ybuf, nt, final=(l == depth - 1)).reshape(bsz, s, D)

    return x_all[:, clen:, :]
```

```python
import functools
import math

import jax
import jax.numpy as jnp
import numpy as np
from jax import lax
from jax.experimental import pallas as pl
from jax.experimental.pallas import tpu as pltpu

F32 = jnp.float32
BF16 = jnp.bfloat16
I32 = jnp.int32
U32 = jnp.uint32
HI = lax.Precision.HIGHEST

D = 1024
DG = 256
N_MOD = 6
GRID_W = 64
EPS = 1e-6

S5_CH = 16
S5_G = DG // S5_CH
S5_P = 64
S5_T = 32
S5_W = S5_T * S5_CH

GLA_H = 4
GLA_DK = 32
GLA_DV = 64
GLA_RANK = 16
GLA_TAU = 16.0
CHUNK = 64

SSD_H = 4
SSD_HD = 64
SSD_G = 2
SSD_N = 64
SSD_K = 5

N_EXP = 32
TOP_K = 4
LIMIT = 7.0
ALPHA = 1.702

TM = 256
BM = 512
LANES = 128

C_FOUR, C_S5, C_QK, C_V, C_R, C_Z, C_XBC, C_MISC = 0, 256, 512, 768, 1024, 1280, 1536, 2048
D_P = 2176
MISC_A1 = 0
MISC_DT = 32

VMEM_LIMIT = 56 * 1024 * 1024


def _cparams(n_axes, vmem=None):
    return pltpu.CompilerParams(dimension_semantics=("arbitrary",) * n_axes,
                                vmem_limit_bytes=vmem)


def _silu(x):
    return x * jax.nn.sigmoid(x)


def _rms(x, g):
    return x * lax.rsqrt(jnp.mean(x * x, axis=-1, keepdims=True) + EPS) * g


def _dot(a, b, dims=None, precision=None):
    if dims is None:
        dims = (((a.ndim - 1,), (0,)), ((), ()))
    return lax.dot_general(a, b, dims, precision=precision, preferred_element_type=F32)


_NT = (((1,), (1,)), ((), ()))
_TN = (((0,), (0,)), ((), ()))


def _pack_bf16_pairs(a):
    n = a.shape[1] // 2
    hi = pltpu.bitcast(a[:, :n].astype(BF16).astype(F32), U32)
    lo = pltpu.bitcast(a[:, n:].astype(BF16).astype(F32), U32)
    return hi | (lo >> 16)


def _unpack_bf16_pairs(w):
    hi = pltpu.bitcast(w & jnp.uint32(0xFFFF0000), F32)
    lo = pltpu.bitcast(w << 16, F32)
    return jnp.concatenate([hi, lo], axis=1)


def _ada_kernel(cond_ref, w_ref, b_ref, o_ref):
    c = _silu(cond_ref[...])
    o_ref[...] = _dot(c, w_ref[...], precision=HI) + b_ref[...]


def _ada(cond, ada_w, ada_b):
    depth = ada_w.shape[0]
    rows = cond.shape[0]
    tn = 1536
    return pl.pallas_call(
        _ada_kernel,
        grid=(depth, (N_MOD * D) // tn),
        in_specs=[pl.BlockSpec((rows, D), lambda l, n: (0, 0)),
                  pl.BlockSpec((None, D, tn), lambda l, n: (l, 0, n)),
                  pl.BlockSpec((None, 1, tn), lambda l, n: (l, 0, n))],
        out_specs=pl.BlockSpec((None, rows, tn), lambda l, n: (l, 0, n)),
        out_shape=jax.ShapeDtypeStruct((depth, rows, N_MOD * D), F32),
        compiler_params=_cparams(2, VMEM_LIMIT),
        name="ada_mod",
    )(cond, ada_w, ada_b.reshape(depth, 1, N_MOD * D))


def _pre_kernel(x_ref, mod_ref, g_ref, w_ref, cw_ref, cb_ref, csm_ref, p_ref, fl_ref, fc_ref):
    j = pl.program_id(1)
    x = x_ref[...]
    hn = _rms(x, g_ref[...]) * (1.0 + mod_ref[1:2, :]) + mod_ref[0:1, :]
    p = _dot(hn.astype(BF16), w_ref[...])
    p_ref[:, :C_XBC] = p[:, :C_XBC]
    p_ref[:, C_MISC:] = p[:, C_MISC:]

    xbc = p[:, C_XBC:C_MISC]
    period = jnp.where(j == 0, TM, GRID_W)
    pos = lax.broadcasted_iota(I32, (TM, 1), 0) & (period - 1)
    acc = jnp.zeros_like(xbc) + cb_ref[...]
    for k in range(SSD_K):
        d = k - SSD_K // 2
        sh = xbc if d == 0 else pltpu.roll(xbc, (TM - d) % TM, axis=0)
        ok = (pos + d >= 0) & (pos + d < period)
        acc = acc + jnp.where(ok, sh, 0.0) * cw_ref[k:k + 1, :]
    p_ref[:, C_XBC:C_MISC] = _silu(acc)

    xcs = _dot(p[:, C_FOUR:C_FOUR + DG].astype(BF16), csm_ref[...])

    @pl.when(j == 0)
    def _():
        fc_ref[0] = xcs[:, :DG].astype(BF16)
        fc_ref[1] = xcs[:, DG:].astype(BF16)

    @pl.when(j > 0)
    def _():
        fl_ref[0] = xcs[:, :DG].astype(BF16)
        fl_ref[1] = xcs[:, DG:].astype(BF16)


def _pre(x_all, mod, g, w_in_p, conv_w, conv_b, csm, seq, ctx):
    b, s, _ = x_all.shape
    nt = s // TM
    return pl.pallas_call(
        _pre_kernel,
        grid=(b, nt),
        in_specs=[pl.BlockSpec((None, TM, D), lambda i, j: (i, j, 0)),
                  pl.BlockSpec((None, None, 8, D), lambda i, j: (i, jnp.minimum(j, 1), 0, 0)),
                  pl.BlockSpec((1, D), lambda i, j: (0, 0)),
                  pl.BlockSpec((D, D_P), lambda i, j: (0, 0)),
                  pl.BlockSpec((8, 512), lambda i, j: (0, 0)),
                  pl.BlockSpec((1, 512), lambda i, j: (0, 0)),
                  pl.BlockSpec((DG, 2 * DG), lambda i, j: (0, 0))],
        out_specs=[pl.BlockSpec((None, TM, D_P), lambda i, j: (i, j, 0)),
                   pl.BlockSpec((None, 2, TM, DG), lambda i, j: (i, 0, jnp.maximum(j - 1, 0), 0)),
                   pl.BlockSpec((None, 2, ctx, DG), lambda i, j: (i, 0, 0, 0))],
        out_shape=[jax.ShapeDtypeStruct((b, s, D_P), F32),
                   jax.ShapeDtypeStruct((b, 2, seq, DG), BF16),
                   jax.ShapeDtypeStruct((b, 2, ctx, DG), BF16)],
        compiler_params=_cparams(2, VMEM_LIMIT),
        name="pre_mixer",
    )(x_all, mod, g, w_in_p, conv_w, conv_b, csm)


def _four_kernel(f_ref, x_ref, w_ref, o_ref):
    y = _dot(f_ref[...], x_ref[...])
    o_ref[...] = _dot(y.astype(BF16), w_ref[...])


def _fourier(fmat, x2, four_w):
    ln = fmat.shape[0]
    b = x2.shape[0]
    tf = min(512, ln)
    return pl.pallas_call(
        _four_kernel,
        grid=(ln // tf, b),
        in_specs=[pl.BlockSpec((tf, 2 * ln), lambda i, j: (i, 0)),
                  pl.BlockSpec((None, 2 * ln, DG), lambda i, j: (j, 0, 0)),
                  pl.BlockSpec((DG, DG), lambda i, j: (0, 0))],
        out_specs=pl.BlockSpec((None, tf, DG), lambda i, j: (j, i, 0)),
        out_shape=jax.ShapeDtypeStruct((b, ln, DG), F32),
        compiler_params=_cparams(2, VMEM_LIMIT),
        name="fourier_dft",
    )(fmat, x2, four_w)


def _dft_matrix(ln):
    f = jnp.arange(ln, dtype=I32)[:, None]
    t = jnp.arange(ln, dtype=I32)[None, :]
    ang = ((f * t) % ln).astype(F32) * (2.0 * math.pi / ln)
    scale = 1.0 / math.sqrt(ln * 64.0)
    return (jnp.concatenate([jnp.cos(ang), jnp.sin(ang)], axis=1) * scale).astype(BF16)


def _channel_dft():
    i = np.arange(DG)
    same = (i[:, None] // 64) == (i[None, :] // 64)
    ang = 2.0 * np.pi * ((i[:, None] % 64) * (i[None, :] % 64) % 64) / 64.0
    c = np.where(same, np.cos(ang), 0.0)
    s = np.where(same, np.sin(ang), 0.0)
    return jnp.asarray(np.concatenate([c, -s], axis=1), dtype=BF16)


def _s5_kernel(u_ref, p_ref, mq_ref, a_ref, y_ref, hl_ref, he_ref, *, nb, nch, ncc):
    u = u_ref[...].astype(BF16)
    hl_ref[...] = _dot(u, p_ref[...])
    a1f, a2f = a_ref[0:1, :LANES], a_ref[1:2, :LANES]
    a1b, a2b = a_ref[0:1, LANES:], a_ref[1:2, LANES:]

    def step(i, carry):
        hf, hb = carry
        rf = pl.multiple_of(i * nb, nb)
        nbk = jnp.where(i < ncc, ncc - 1 - i, nch + ncc - 1 - i)
        rb = pl.multiple_of(nbk * nb, nb)
        he_ref[pl.ds(rf, nb), :LANES] = hf
        he_ref[pl.ds(rb, nb), LANES:] = hb
        hf = a1f * hf + a2f * pltpu.roll(hf, 64, axis=1) + hl_ref[pl.ds(rf, nb), :LANES]
        hb = a1b * hb + a2b * pltpu.roll(hb, 64, axis=1) + hl_ref[pl.ds(rb, nb), LANES:]
        return hf, hb

    z = jnp.zeros((nb, LANES), F32)
    lax.fori_loop(0, nch, step, (z, z))
    y_ref[...] = (_dot(u, mq_ref[:S5_W, :])
                  + _dot(he_ref[...].astype(BF16), mq_ref[S5_W:, :]))


def _s5(u_g, ptab, mq, atab, nb, nch, ncc):
    g, r, _ = u_g.shape
    return pl.pallas_call(
        functools.partial(_s5_kernel, nb=nb, nch=nch, ncc=ncc),
        grid=(g,),
        in_specs=[pl.BlockSpec((None, r, S5_W), lambda i: (i, 0, 0)),
                  pl.BlockSpec((None, S5_W, 2 * LANES), lambda i: (i, 0, 0)),
                  pl.BlockSpec((None, S5_W + 2 * LANES, S5_W), lambda i: (i, 0, 0)),
                  pl.BlockSpec((None, 8, 2 * LANES), lambda i: (i, 0, 0))],
        out_specs=pl.BlockSpec((None, r, S5_W), lambda i: (i, 0, 0)),
        out_shape=jax.ShapeDtypeStruct((g, r, S5_W), F32),
        scratch_shapes=[pltpu.VMEM((r, 2 * LANES), F32), pltpu.VMEM((r, 2 * LANES), F32)],
        compiler_params=_cparams(1, VMEM_LIMIT),
        name="s5_mixer",
    )(u_g, ptab, mq, atab)


def _s5_tables(lam_re, lam_im, log_dt, b_re, b_im, c_re, c_im):
    t = S5_T
    dt = jnp.exp(log_dt)[..., None]
    ldr, ldi = lam_re * dt, lam_im * dt
    jj = jnp.arange(t + 1, dtype=F32)[None, None, :, None]
    mag = jnp.exp(ldr[:, :, None, :] * jj)
    pr, pi = mag * jnp.cos(ldi[:, :, None, :] * jj), mag * jnp.sin(ldi[:, :, None, :] * jj)
    ar, ai = pr[:, :, 1], pi[:, :, 1]
    den = lam_re * lam_re + lam_im * lam_im
    qr = ((ar - 1.0) * lam_re + ai * lam_im) / den
    qi = (ai * lam_re - (ar - 1.0) * lam_im) / den
    bbr = qr[..., None] * b_re - qi[..., None] * b_im
    bbi = qr[..., None] * b_im + qi[..., None] * b_re
    cpr = c_re[:, :, None] * pr[:, :, :, None, :] - c_im[:, :, None] * pi[:, :, :, None, :]
    cpi = c_re[:, :, None] * pi[:, :, :, None, :] + c_im[:, :, None] * pr[:, :, :, None, :]
    klag = (jnp.einsum('rgjop,rgph->rgjoh', cpr[:, :, :t], bbr, precision=HI)
            - jnp.einsum('rgjop,rgph->rgjoh', cpi[:, :, :t], bbi, precision=HI))
    s_i = jnp.arange(t)[:, None]
    t_i = jnp.arange(t)[None, :]

    def toeplitz(k, d):
        m = k[:, jnp.clip(d, 0, t - 1)]
        m = jnp.where((d >= 0)[None, :, :, None, None], m, 0.0)
        return jnp.transpose(m, (0, 1, 4, 2, 3))

    m_sum = (toeplitz(klag[0], t_i - s_i) + toeplitz(klag[1], s_i - t_i)).reshape(S5_G, S5_W, S5_W)
    pbr = pr[..., None] * bbr[:, :, None] - pi[..., None] * bbi[:, :, None]
    pbi = pr[..., None] * bbi[:, :, None] + pi[..., None] * bbr[:, :, None]

    def ptab(r, exps):
        re = jnp.transpose(pbr[r][:, exps], (0, 1, 3, 2))
        im = jnp.transpose(pbi[r][:, exps], (0, 1, 3, 2))
        return jnp.concatenate([re, im], axis=-1).reshape(S5_G, S5_W, 2 * S5_P)

    ar_t = np.arange(t)
    p2 = jnp.concatenate([ptab(0, t - 1 - ar_t), ptab(1, ar_t)], axis=-1)

    def qtab(r, exps):
        re = jnp.transpose(cpr[r][:, exps], (0, 3, 1, 2)).reshape(S5_G, S5_P, S5_W)
        im = jnp.transpose(cpi[r][:, exps], (0, 3, 1, 2)).reshape(S5_G, S5_P, S5_W)
        return jnp.concatenate([re, -im], axis=1)

    mq = jnp.concatenate([m_sum, qtab(0, ar_t + 1), qtab(1, t - ar_t)], axis=1)
    art, ait = pr[:, :, t], pi[:, :, t]
    a1 = jnp.concatenate([art[0], art[0], art[1], art[1]], axis=-1)
    a2 = jnp.concatenate([-ait[0], ait[0], -ait[1], ait[1]], axis=-1)
    atab = jnp.concatenate([a1[:, None], a2[:, None], jnp.zeros((S5_G, 6, 4 * S5_P), F32)], axis=1)
    return p2.astype(BF16), mq.astype(BF16), atab


def _split3(x):
    x1 = x.astype(BF16)
    r1 = x - x1.astype(F32)
    x2 = r1.astype(BF16)
    x3 = (r1 - x2.astype(F32)).astype(BF16)
    return x1, x2, x3


def _dot01(m, x):
    x1, x2, x3 = _split3(x)
    return (_dot(m, x3) + _dot(m, x2)) + _dot(m, x1)


def _tile_cumsum_mask(rev):
    t = lax.broadcasted_iota(I32, (TM, TM), 0)
    s = lax.broadcasted_iota(I32, (TM, TM), 1)
    order = (s >= t) if rev else (s <= t)
    return (order & ((t // CHUNK) == (s // CHUNK))).astype(BF16)


def _head_lane_mask(rev, width):
    t = lax.broadcasted_iota(I32, (CHUNK, width), 0)
    s = lax.broadcasted_iota(I32, (CHUNK, width), 1) % CHUNK
    return (s >= t) if rev else (s <= t)


def _gla_kernel(qk_ref, v_ref, misc_ref, a2_ref, ab_ref, o_ref, st_ref, *, rev):
    j = pl.program_id(1)

    @pl.when(j == 0)
    def _():
        st_ref[...] = jnp.zeros_like(st_ref)

    lane_k = lax.broadcasted_iota(I32, (1, LANES), 1) // GLA_DK
    lane_v = lax.broadcasted_iota(I32, (1, DG), 1) // GLA_DV
    causal = _head_lane_mask(rev, DG)
    st_diag = (lax.broadcasted_iota(I32, (DG, 1), 0) // GLA_DV) == lane_k

    z = _dot(misc_ref[...].astype(BF16), a2_ref[...]) + ab_ref[...]
    la_t = -(jnp.maximum(-z, 0.0) + jnp.log1p(jnp.exp(-jnp.abs(z)))) * (1.0 / GLA_TAU)
    b_t = _dot01(_tile_cumsum_mask(rev), la_t)
    st = st_ref[...]
    chunks = range(TM // CHUNK)
    for c in (reversed(chunks) if rev else chunks):
        rows = slice(c * CHUNK, (c + 1) * CHUNK)
        la, bcum = la_t[rows], b_t[rows]
        q = qk_ref[rows, :LANES] * (GLA_DK ** -0.5)
        k = qk_ref[rows, LANES:]
        v = v_ref[rows, :]
        tot = jnp.sum(la, axis=0, keepdims=True)
        q_in = (q * jnp.exp(bcum)).astype(BF16)
        k_in = k * jnp.exp(-bcum)
        k_dec = (k * jnp.exp(tot - bcum)).astype(BF16)
        k4 = jnp.concatenate([jnp.where(lane_k == h, k_in, 0.0) for h in range(GLA_H)], axis=0)
        att = jnp.where(causal, _dot(q_in, k4.astype(BF16), _NT), 0.0)
        vb = v.astype(BF16)
        vbd = jnp.concatenate([jnp.where(lane_v == h, vb, jnp.zeros_like(vb)) for h in range(GLA_H)], axis=0)
        o = _dot(att.astype(BF16), vbd) + _dot(q_in, jnp.where(st_diag, st, 0.0).astype(BF16), _NT)
        o_ref[rows, :] = o
        st = st * jnp.exp(tot) + _dot(vb, k_dec, _TN)
    st_ref[...] = st


def _tile_order(rev, j, nt):
    return jnp.where(j == 0, 0, nt - j) if rev else j


def _gla(p, a2m, abm, rev):
    b, s, _ = p.shape
    nt = s // TM
    return pl.pallas_call(
        functools.partial(_gla_kernel, rev=rev),
        grid=(b, nt),
        in_specs=[pl.BlockSpec((None, TM, DG), lambda i, j: (i, _tile_order(rev, j, nt), C_QK // DG)),
                  pl.BlockSpec((None, TM, DG), lambda i, j: (i, _tile_order(rev, j, nt), C_V // DG)),
                  pl.BlockSpec((None, TM, LANES), lambda i, j: (i, _tile_order(rev, j, nt), C_MISC // LANES)),
                  pl.BlockSpec((LANES, LANES), lambda i, j: (0, 0)),
                  pl.BlockSpec((1, LANES), lambda i, j: (0, 0))],
        out_specs=pl.BlockSpec((None, TM, DG), lambda i, j: (i, _tile_order(rev, j, nt), 0)),
        out_shape=jax.ShapeDtypeStruct((b, s, DG), F32),
        scratch_shapes=[pltpu.VMEM((DG, LANES), F32)],
        compiler_params=_cparams(2, VMEM_LIMIT),
        name="gla_bwd" if rev else "gla_fwd",
    )(p, p, p, a2m, abm)


def _ssd_kernel(xbc_ref, misc_ref, dtb_ref, av_ref, y_ref, sb_ref, *, rev):
    j = pl.program_id(1)

    @pl.when(j == 0)
    def _():
        sb_ref[...] = jnp.zeros_like(sb_ref)

    lane = lax.broadcasted_iota(I32, (1, LANES), 1)
    lane_g = lane // SSD_N
    lane_h = lax.broadcasted_iota(I32, (1, DG), 1) // SSD_HD
    row_h = lax.broadcasted_iota(I32, (DG, 1), 0) // SSD_HD
    blockmask = row_h == lane_h
    causal = _head_lane_mask(rev, DG)
    diag = (lax.broadcasted_iota(I32, (CHUNK, DG), 0)
            == lax.broadcasted_iota(I32, (CHUNK, DG), 1) % CHUNK)

    def per_head(x):
        xr = pltpu.roll(x, SSD_N, axis=1)
        return jnp.concatenate([jnp.where(lane < SSD_N, x, xr), jnp.where(lane < SSD_N, xr, x)], axis=1)

    zm = misc_ref[...] + dtb_ref[...]
    sp = jnp.maximum(zm, 0.0) + jnp.log1p(jnp.exp(-jnp.abs(zm)))
    dt_t = jnp.zeros((TM, DG), F32)
    for h in range(SSD_H):
        col = jnp.sum(jnp.where(lane == MISC_DT + SSD_H * int(rev) + h, sp, 0.0), axis=1, keepdims=True)
        dt_t = jnp.where(lane_h == h, col, dt_t)
    da_t = dt_t * av_ref[...]
    cum_t = _dot01(_tile_cumsum_mask(rev), da_t)

    sb = sb_ref[...]
    chunks = range(TM // CHUNK)
    for c in (reversed(chunks) if rev else chunks):
        rows = slice(c * CHUNK, (c + 1) * CHUNK)
        da, cum = da_t[rows], cum_t[rows]
        bmc = xbc_ref[rows, DG:DG + LANES]
        cmc = xbc_ref[rows, DG + LANES:]
        xdt = (xbc_ref[rows, :DG] * dt_t[rows]).astype(BF16)
        tot = jnp.sum(da, axis=0, keepdims=True)
        cum_row = jnp.sum(jnp.where(diag, cum, 0.0), axis=0, keepdims=True)
        decay = jnp.where(causal, jnp.exp(jnp.minimum(cum - cum_row, 0.0)), 0.0)
        b4 = jnp.concatenate([jnp.where(lane_g == h // (SSD_H // SSD_G), bmc, 0.0) for h in range(SSD_H)],
                             axis=0)
        scores = _dot(cmc.astype(BF16), b4.astype(BF16), _NT) * decay
        xbd = jnp.concatenate([jnp.where(lane_h == h, xdt, jnp.zeros_like(xdt)) for h in range(SSD_H)],
                              axis=0)
        cdec = per_head(cmc) * jnp.exp(cum)
        y = _dot(scores.astype(BF16), xbd) + _dot(cdec.astype(BF16), sb.astype(BF16))
        y_ref[rows, :] = y
        bdec = per_head(bmc) * jnp.exp(tot - cum)
        upd = _dot(bdec.astype(BF16), xdt, _TN)
        sb = sb * jnp.exp(tot) + jnp.where(blockmask, upd, 0.0)
    sb_ref[...] = sb


def _ssd(p, dtb, av, rev):
    b, s, _ = p.shape
    nt = s // TM
    return pl.pallas_call(
        functools.partial(_ssd_kernel, rev=rev),
        grid=(b, nt),
        in_specs=[pl.BlockSpec((None, TM, 512), lambda i, j: (i, _tile_order(rev, j, nt), C_XBC // 512)),
                  pl.BlockSpec((None, TM, LANES), lambda i, j: (i, _tile_order(rev, j, nt), C_MISC // LANES)),
                  pl.BlockSpec((1, LANES), lambda i, j: (0, 0)),
                  pl.BlockSpec((1, DG), lambda i, j: (0, 0))],
        out_specs=pl.BlockSpec((None, TM, DG), lambda i, j: (i, _tile_order(rev, j, nt), 0)),
        out_shape=jax.ShapeDtypeStruct((b, s, DG), F32),
        scratch_shapes=[pltpu.VMEM((DG, DG), F32)],
        compiler_params=_cparams(2, VMEM_LIMIT),
        name="ssd_bwd" if rev else "ssd_fwd",
    )(p, p, dtb, av)


def _ssd_tables(a_log, dt_bias, r):
    dtb = jnp.zeros((1, LANES), F32).at[0, MISC_DT + SSD_H * r:MISC_DT + SSD_H * (r + 1)].set(dt_bias[r])
    av = jnp.repeat(-jnp.exp(a_log[r]), SSD_HD)[None, :]
    return dtb, av


def _gelu_tanh(x):
    return 0.5 * x * (1.0 + jnp.tanh(math.sqrt(2.0 / math.pi) * (x + 0.044715 * x * x * x)))


def _post_kernel(x_ref, mod_ref, us5_ref, r_ref, z_ref, xs_ref, fl_ref, fc_ref, s5y_ref,
                 gf_ref, gb_ref, sf_ref, sbk_ref, vec_ref, wglu_ref, avg_ref, wout_ref,
                 gffn_ref, rw_ref, rb_ref, tri_ref,
                 xo_ref, hp_ref, ri_ref, rg_ref, cnt_ref, cnt_sc):
    i = pl.program_id(0)
    j = pl.program_id(1)

    @pl.when((i == 0) & (j == 0))
    def _():
        cnt_sc[...] = jnp.zeros_like(cnt_sc)

    s5_d, gla_g, ssd_d, ssd_g = vec_ref[0:1, :], vec_ref[1:2, :], vec_ref[2:3, :], vec_ref[3:4, :]
    four = jnp.where(j == 0, fc_ref[...], fl_ref[...])
    y5 = s5y_ref[...] + s5_d * us5_ref[...]
    act = _gelu_tanh(y5)
    s5o = act * jax.nn.sigmoid(_dot(act.astype(BF16), wglu_ref[...]))
    og = gf_ref[...] + gb_ref[...]
    ms = _dot(og * og, avg_ref[...], precision=HI)
    glo = og * lax.rsqrt(ms + EPS) * gla_g * _silu(r_ref[...])
    ys = (sf_ref[...] + sbk_ref[...] + ssd_d * xs_ref[...]) * _silu(z_ref[...])
    sso = _rms(ys, ssd_g)
    mix = jnp.concatenate([four, s5o, glo, sso], axis=1).astype(BF16)
    x = x_ref[...] + mod_ref[2:3, :] * _dot(mix, wout_ref[...])
    xo_ref[...] = x
    h = _rms(x, gffn_ref[...]) * (1.0 + mod_ref[4:5, :]) + mod_ref[3:4, :]
    hp_ref[...] = _pack_bf16_pairs(h)

    logits = _dot(h, rw_ref[...], precision=HI) + rb_ref[...]
    lane = lax.broadcasted_iota(I32, (TM, LANES), 1)
    vals, idxs = [], []
    for _k in range(TOP_K):
        mx = jnp.max(logits, axis=1, keepdims=True)
        ix = jnp.min(jnp.where(logits == mx, lane, LANES), axis=1, keepdims=True)
        vals.append(mx)
        idxs.append(ix)
        logits = jnp.where(lane == ix, -jnp.inf, logits)
    es = [jnp.exp(v - vals[0]) for v in vals]
    den = es[0] + es[1] + es[2] + es[3]
    onehots = [(lane == ix) for ix in idxs]
    oh = sum(o.astype(F32) for o in onehots)
    before = _dot(tri_ref[...], oh.astype(BF16)) + cnt_sc[0:1, :]
    ri = jnp.zeros((TM, LANES), I32)
    rg = jnp.zeros((TM, LANES), F32)
    for k in range(TOP_K):
        rank = jnp.sum(jnp.where(onehots[k], before, 0.0), axis=1, keepdims=True).astype(I32)
        ri = jnp.where(lane == k, idxs[k], ri)
        ri = jnp.where(lane == TOP_K + k, rank, ri)
        rg = jnp.where(lane == k, es[k] / den, rg)
    ri_ref[...] = ri
    rg_ref[...] = rg
    cnt_sc[...] = cnt_sc[...] + jnp.sum(oh, axis=0, keepdims=True)
    cnt_ref[...] = cnt_sc[...]


def _post(x_all, mod, p, four_l, four_c, s5y, gla_f, gla_b, ssd_f, ssd_b, vecs, wglu, avg, wout, gffn,
          rw, rb, tri, seq, ctx):
    b, s, _ = x_all.shape
    nt = s // TM
    tile = lambda w, col: pl.BlockSpec((None, TM, w), lambda i, j: (i, j, col))
    full = lambda a: pl.BlockSpec(a.shape, lambda i, j: (0,) * a.ndim)
    return pl.pallas_call(
        _post_kernel,
        grid=(b, nt),
        in_specs=[tile(D, 0),
                  pl.BlockSpec((None, None, 8, D), lambda i, j: (i, jnp.minimum(j, 1), 0, 0)),
                  tile(DG, C_S5 // DG), tile(DG, C_R // DG), tile(DG, C_Z // DG), tile(DG, C_XBC // DG),
                  pl.BlockSpec((None, TM, DG), lambda i, j: (i, jnp.maximum(j - 1, 0), 0)),
                  pl.BlockSpec((None, ctx, DG), lambda i, j: (i, 0, 0)),
                  tile(DG, 0), tile(DG, 0), tile(DG, 0), tile(DG, 0), tile(DG, 0),
                  full(vecs), full(wglu), full(avg), full(wout), full(gffn), full(rw), full(rb), full(tri)],
        out_specs=[tile(D, 0), tile(D // 2, 0), tile(LANES, 0), tile(LANES, 0),
                   pl.BlockSpec((8, LANES), lambda i, j: (0, 0))],
        out_shape=[jax.ShapeDtypeStruct((b, s, D), F32),
                   jax.ShapeDtypeStruct((b, s, D // 2), U32),
                   jax.ShapeDtypeStruct((b, s, LANES), I32),
                   jax.ShapeDtypeStruct((b, s, LANES), F32),
                   jax.ShapeDtypeStruct((8, LANES), F32)],
        scratch_shapes=[pltpu.VMEM((8, LANES), F32)],
        compiler_params=_cparams(2, VMEM_LIMIT),
        name="post_mixer_router",
    )(x_all, mod, p, p, p, p, four_l, four_c, s5y, gla_f, gla_b, ssd_f, ssd_b,
      vecs, wglu, avg, wout, gffn, rw, rb, tri)


def _dispatch_kernel(dest_ref, h_ref, buf_in_ref, buf_ref, sem):
    del buf_in_ref

    def row_copy(t, d):
        return pltpu.make_async_copy(h_ref.at[pl.ds(t, 1)], buf_ref.at[pl.ds(d, 1)], sem)

    def issue(t, c):
        for k in range(TOP_K):
            row_copy(t, dest_ref[0, 0, t * TOP_K + k]).start(priority=k % 2)
        return c

    lax.fori_loop(0, TM, issue, 0)

    def drain(n, c):
        row_copy(0, 0).wait()
        return c

    lax.fori_loop(0, TM * TOP_K, drain, 0)


def _dispatch(dest3, hp, buf0):
    n = hp.shape[0]
    return pl.pallas_call(
        _dispatch_kernel,
        grid=(n // TM,),
        in_specs=[pl.BlockSpec((1, 1, TM * TOP_K), lambda i: (i, 0, 0), memory_space=pltpu.SMEM),
                  pl.BlockSpec((TM, D // 2), lambda i: (i, 0)),
                  pl.BlockSpec(memory_space=pl.ANY)],
        out_specs=pl.BlockSpec(memory_space=pl.ANY),
        out_shape=jax.ShapeDtypeStruct(buf0.shape, U32),
        scratch_shapes=[pltpu.SemaphoreType.DMA(())],
        input_output_aliases={2: 0},
        compiler_params=_cparams(1, VMEM_LIMIT),
        name="moe_dispatch",
    )(dest3, hp, buf0)


def _expert_kernel(be_ref, nu_ref, x_ref, wgu_ref, wd_ref, bg_ref, bu_ref, bd_ref, perm_ref, y_ref,
                   wg_sc, wu_sc, wd_sc):
    i = pl.program_id(0)
    used = i < nu_ref[0]
    new_expert = (i == 0) | (be_ref[i] != be_ref[jnp.maximum(i - 1, 0)])

    @pl.when(used & new_expert)
    def _():
        for c in range(2 * D // DG):
            d = _dot(wgu_ref[:, c * DG:(c + 1) * DG].astype(BF16), perm_ref[...])
            wg_sc[:, c * LANES:(c + 1) * LANES] = d[:, :LANES].astype(BF16)
            wu_sc[:, c * LANES:(c + 1) * LANES] = d[:, LANES:].astype(BF16)
        wd_sc[...] = wd_ref[...].astype(BF16)

    @pl.when(used)
    def _():
        x = _unpack_bf16_pairs(x_ref[...]).astype(BF16)
        g = _dot(x, wg_sc[...]) + bg_ref[...]
        u = _dot(x, wu_sc[...]) + bu_ref[...]
        gate = jnp.minimum(g, LIMIT)
        up = jnp.clip(u, -LIMIT, LIMIT)
        act = (up + 1.0) * gate * jax.nn.sigmoid(ALPHA * gate)
        y = _dot(act.astype(BF16), wd_sc[...]) + bd_ref[...]
        y_ref[...] = _pack_bf16_pairs(y)

    @pl.when(jnp.logical_not(used))
    def _():
        y_ref[...] = jnp.zeros_like(y_ref)


def _deinterleave_perm():
    pm = np.zeros((DG, DG), np.float32)
    o = np.arange(LANES)
    pm[2 * o, o] = 1.0
    pm[2 * o + 1, LANES + o] = 1.0
    return jnp.asarray(pm, BF16)


def _experts(blk_e, n_used, buf, wgu, wd, bg, bu, bd):
    nblk = buf.shape[0] // BM
    row = lambda i, be, nu: (jnp.minimum(i, nu[0] - 1), 0)
    wsel = lambda i, be, nu: (be[jnp.minimum(i, nu[0] - 1)], 0, 0)
    grid_spec = pltpu.PrefetchScalarGridSpec(
        num_scalar_prefetch=2,
        grid=(nblk,),
        in_specs=[pl.BlockSpec((BM, D // 2), row),
                  pl.BlockSpec((None, D, 2 * D), wsel), pl.BlockSpec((None, D, D), wsel),
                  pl.BlockSpec((None, 1, D), wsel), pl.BlockSpec((None, 1, D), wsel),
                  pl.BlockSpec((None, 1, D), wsel),
                  pl.BlockSpec((DG, DG), lambda i, be, nu: (0, 0))],
        out_specs=pl.BlockSpec((BM, D // 2), lambda i, be, nu: (i, 0)),
        scratch_shapes=[pltpu.VMEM((D, D), BF16), pltpu.VMEM((D, D), BF16), pltpu.VMEM((D, D), BF16)],
    )
    return pl.pallas_call(
        _expert_kernel,
        grid_spec=grid_spec,
        out_shape=jax.ShapeDtypeStruct(buf.shape, U32),
        compiler_params=_cparams(1, VMEM_LIMIT),
        name="moe_experts",
    )(blk_e, n_used, buf, wgu, wd, bg, bu, bd, _deinterleave_perm())


def _combine_kernel(dc_ref, dn_ref, x_ref, rg_ref, mod_ref, fg_ref, yb_ref, o_ref, rows, sem, *, final):
    i = pl.program_id(0)
    n = pl.num_programs(0)
    slot = i % 2

    def row_copy(d, sl, k, t):
        return pltpu.make_async_copy(yb_ref.at[pl.ds(d, 1)], rows.at[sl, k, pl.ds(t, 1)], sem.at[sl])

    def gather(dref, sl):
        def issue(t, c):
            for k in range(TOP_K):
                row_copy(dref[0, 0, t * TOP_K + k], sl, k, t).start(priority=k % 2)
            return c
        lax.fori_loop(0, TM, issue, 0)

    @pl.when(i == 0)
    def _():
        gather(dc_ref, 0)

    @pl.when(i + 1 < n)
    def _():
        gather(dn_ref, 1 - slot)

    def drain(t, c):
        row_copy(0, slot, 0, 0).wait()
        return c

    lax.fori_loop(0, TM * TOP_K, drain, 0)

    f = jnp.zeros((TM, D), F32)
    for k in range(TOP_K):
        f = f + rg_ref[:, k:k + 1] * _unpack_bf16_pairs(rows[slot, k])
    x = x_ref[...] + mod_ref[5:6, :] * f
    o_ref[...] = _rms(x, fg_ref[...]) if final else x


def _combine(dest3, x_flat, rg_flat, mod, fg, ybuf, nt, final):
    n = x_flat.shape[0]
    steps = n // TM
    dspec = lambda f: pl.BlockSpec((1, 1, TM * TOP_K), f, memory_space=pltpu.SMEM)
    return pl.pallas_call(
        functools.partial(_combine_kernel, final=final),
        grid=(steps,),
        in_specs=[dspec(lambda i: (i, 0, 0)),
                  dspec(lambda i: (jnp.minimum(i + 1, steps - 1), 0, 0)),
                  pl.BlockSpec((TM, D), lambda i: (i, 0)),
                  pl.BlockSpec((TM, LANES), lambda i: (i, 0)),
                  pl.BlockSpec((None, None, 8, D), lambda i: (i // nt, jnp.minimum(i % nt, 1), 0, 0)),
                  pl.BlockSpec((1, D), lambda i: (0, 0)),
                  pl.BlockSpec(memory_space=pl.ANY)],
        out_specs=pl.BlockSpec((TM, D), lambda i: (i, 0)),
        out_shape=jax.ShapeDtypeStruct((n, D), F32),
        scratch_shapes=[pltpu.VMEM((2, TOP_K, TM, D // 2), U32), pltpu.SemaphoreType.DMA((2,))],
        compiler_params=_cparams(1, VMEM_LIMIT),
        name="moe_combine",
    )(dest3, dest3, x_flat, rg_flat, mod, fg, ybuf)


def _repack_w_in(w):
    four, s5, q, k, v, r, a1, z, xbc, dt = jnp.split(
        w, np.cumsum([256, 256, 128, 128, 256, 256, 32, 256, 512, 8])[:-1].tolist(), axis=-1)
    misc = jnp.concatenate([a1, dt, jnp.zeros((D, LANES - 40), w.dtype)], axis=-1)
    return jnp.concatenate([four, s5, q, k, v, r, z, xbc, misc], axis=-1).astype(BF16)


def kernel(x, c, ctx, c_ctx, ada_w, ada_b, norm_mix_g, norm_ffn_g, w_in, w_out, four_w, s5_lam_re, s5_lam_im, s5_log_dt, s5_b_re, s5_b_im, s5_c_re, s5_c_im, s5_d, s5_w_glu, gla_a2, gla_a_b, gla_norm_g, ssd_conv_w, ssd_conv_b, ssd_a_log, ssd_dt_bias, ssd_d, ssd_norm_g, router_w, router_b, exp_w_gu, exp_b_gu, exp_w_dn, exp_b_dn, final_g):
    bsz, seq, _ = x.shape
    clen = ctx.shape[1]
    depth = ada_w.shape[0]
    assert clen == TM and seq % TM == 0 and bsz == 8
    s = clen + seq
    nt = s // TM
    n_tok = bsz * s
    nch, ncc = s // S5_T, clen // S5_T

    cond = jnp.concatenate([c, c_ctx[None, :], jnp.zeros((16 - bsz - 1, D), F32)], axis=0)
    mods = _ada(cond, ada_w, ada_b).reshape(depth, 16, N_MOD, D)
    m_lat = mods[:, :bsz]
    m_ctx = jnp.broadcast_to(mods[:, bsz:bsz + 1], m_lat.shape)
    mods = jnp.stack([m_ctx, m_lat], axis=2)
    mods = jnp.pad(mods, ((0, 0), (0, 0), (0, 0), (0, 2), (0, 0)))

    x_all = jnp.concatenate([ctx, x], axis=1)
    csm = _channel_dft()
    f_lat, f_ctx = _dft_matrix(seq), _dft_matrix(clen)
    avg = jnp.asarray(np.kron(np.eye(GLA_H), np.full((GLA_DV, GLA_DV), 1.0 / GLA_DV)), F32)
    tri = jnp.asarray(np.tril(np.ones((TM, TM)), -1), BF16)

    n_rows = n_tok * TOP_K + N_EXP * BM
    nblk = n_rows // BM

    for l in range(depth):
        mod = mods[l]
        w_in_p = _repack_w_in(w_in[l])
        conv_w = jnp.pad(ssd_conv_w[l], ((0, 8 - SSD_K), (0, 0)))
        p, four_l, four_c = _pre(x_all, mod, norm_mix_g[l][None], w_in_p, conv_w, ssd_conv_b[l][None],
                                 csm, seq, clen)
        fw = four_w[l].astype(BF16)
        fy_l = _fourier(f_lat, four_l.reshape(bsz, 2 * seq, DG), fw)
        fy_c = _fourier(f_ctx, four_c.reshape(bsz, 2 * clen, DG), fw)

        ptab, mq, atab = _s5_tables(s5_lam_re[l], s5_lam_im[l], s5_log_dt[l], s5_b_re[l], s5_b_im[l],
                                    s5_c_re[l], s5_c_im[l])
        u5 = p[:, :, C_S5:C_S5 + DG].reshape(bsz, nch, S5_T, S5_G, S5_CH)
        u5 = jnp.transpose(u5, (3, 1, 0, 2, 4)).reshape(S5_G, nch * bsz, S5_W)
        y5 = _s5(u5, ptab, mq, atab, bsz, nch, ncc).reshape(S5_G, nch, bsz, S5_T, S5_CH)
        y5 = jnp.transpose(y5, (2, 1, 3, 0, 4)).reshape(bsz, s, DG)

        gla_o, ssd_y = [], []
        for r in range(2):
            a2m = jnp.zeros((LANES, LANES), F32).at[GLA_RANK * r:GLA_RANK * (r + 1)].set(gla_a2[l, r])
            gla_o.append(_gla(p, a2m.astype(BF16), gla_a_b[l, r][None, :], rev=bool(r)))
            ssd_y.append(_ssd(p, *_ssd_tables(ssd_a_log[l], ssd_dt_bias[l], r), rev=bool(r)))

        vecs = jnp.stack([s5_d[l], jnp.tile(gla_norm_g[l], GLA_H), jnp.repeat(ssd_d[l], SSD_HD),
                          ssd_norm_g[l]] + [jnp.zeros((DG,), F32)] * 4)
        rw = jnp.pad(router_w[l], ((0, 0), (0, LANES - N_EXP)))
        rb = jnp.concatenate([router_b[l], jnp.full((LANES - N_EXP,), -1e30, F32)])[None]
        x_mid, hp, ri, rg, cnt = _post(x_all, mod, p, fy_l, fy_c, y5, gla_o[0], gla_o[1], ssd_y[0], ssd_y[1], vecs,
                                       s5_w_glu[l].astype(BF16), avg, w_out[l].astype(BF16),
                                       norm_ffn_g[l][None], rw, rb, tri, seq, clen)

        counts = cnt[0, :N_EXP].astype(I32)
        padded = (counts + BM - 1) // BM * BM
        pad_end = jnp.cumsum(padded)
        pad_start = pad_end - padded
        ri = ri.reshape(n_tok, LANES)
        dest = pad_start[ri[:, :TOP_K]] + ri[:, TOP_K:2 * TOP_K]
        dest3 = dest.reshape(n_tok // TM, 1, TM * TOP_K)
        blk_start = jnp.arange(nblk, dtype=I32) * BM
        blk_e = jnp.minimum(jnp.sum((pad_end[None, :] <= blk_start[:, None]).astype(I32), axis=1), N_EXP - 1)
        n_used = (pad_end[-1] // BM).astype(I32)[None]

        buf = _dispatch(dest3, hp.reshape(n_tok, D // 2), jnp.zeros((n_rows, D // 2), U32))
        bgu = exp_b_gu[l].reshape(N_EXP, 1, D, 2)
        ybuf = _experts(blk_e, n_used, buf, exp_w_gu[l], exp_w_dn[l], bgu[..., 0], bgu[..., 1],
                        exp_b_dn[l][:, None, :])
        x_all = _combine(dest3, x_mid.reshape(n_tok, D), rg.reshape(n_tok, LANES), mod, final_g[None],
                         ybuf, nt, final=(l == depth - 1)).reshape(bsz, s, D)

    return x_all[:, clen:, :]
```

```python
import functools
import math

import jax
import jax.numpy as jnp
import numpy as np
from jax import lax
from jax.experimental import pallas as pl
from jax.experimental.pallas import tpu as pltpu

F32 = jnp.float32
BF16 = jnp.bfloat16
I32 = jnp.int32
U32 = jnp.uint32
HI = lax.Precision.HIGHEST

D = 1024
DG = 256
N_MOD = 6
GRID_W = 64
EPS = 1e-6

S5_CH = 16
S5_G = DG // S5_CH
S5_P = 64
S5_T = 32
S5_W = S5_T * S5_CH

GLA_H = 4
GLA_DK = 32
GLA_DV = 64
GLA_RANK = 16
GLA_TAU = 16.0
CHUNK = 64

SSD_H = 4
SSD_HD = 64
SSD_G = 2
SSD_N = 64
SSD_K = 5

N_EXP = 32
TOP_K = 4
LIMIT = 7.0
ALPHA = 1.702

TM = 256
BM = 512
LANES = 128

C_FOUR, C_S5, C_QK, C_V, C_R, C_Z, C_XBC, C_MISC = 0, 256, 512, 768, 1024, 1280, 1536, 2048
D_P = 2176
MISC_A1 = 0
MISC_DT = 32

VMEM_LIMIT = 56 * 1024 * 1024


def _cparams(n_axes, vmem=None):
    return pltpu.CompilerParams(dimension_semantics=("arbitrary",) * n_axes,
                                vmem_limit_bytes=vmem)


def _silu(x):
    return x * jax.nn.sigmoid(x)


def _rms(x, g):
    return x * lax.rsqrt(jnp.mean(x * x, axis=-1, keepdims=True) + EPS) * g


def _dot(a, b, dims=None, precision=None):
    if dims is None:
        dims = (((a.ndim - 1,), (0,)), ((), ()))
    return lax.dot_general(a, b, dims, precision=precision, preferred_element_type=F32)


_NT = (((1,), (1,)), ((), ()))
_TN = (((0,), (0,)), ((), ()))


def _pack_bf16_pairs(a):
    n = a.shape[1] // 2
    hi = pltpu.bitcast(a[:, :n].astype(BF16).astype(F32), U32)
    lo = pltpu.bitcast(a[:, n:].astype(BF16).astype(F32), U32)
    return hi | (lo >> 16)


def _unpack_bf16_pairs(w):
    hi = pltpu.bitcast(w & jnp.uint32(0xFFFF0000), F32)
    lo = pltpu.bitcast(w << 16, F32)
    return jnp.concatenate([hi, lo], axis=1)


def _ada_kernel(cond_ref, w_ref, b_ref, o_ref):
    c = _silu(cond_ref[...])
    o_ref[...] = _dot(c, w_ref[...], precision=HI) + b_ref[...]


def _ada(cond, ada_w, ada_b):
    depth = ada_w.shape[0]
    rows = cond.shape[0]
    tn = 1536
    return pl.pallas_call(
        _ada_kernel,
        grid=(depth, (N_MOD * D) // tn),
        in_specs=[pl.BlockSpec((rows, D), lambda l, n: (0, 0)),
                  pl.BlockSpec((None, D, tn), lambda l, n: (l, 0, n)),
                  pl.BlockSpec((None, 1, tn), lambda l, n: (l, 0, n))],
        out_specs=pl.BlockSpec((None, rows, tn), lambda l, n: (l, 0, n)),
        out_shape=jax.ShapeDtypeStruct((depth, rows, N_MOD * D), F32),
        compiler_params=_cparams(2, VMEM_LIMIT),
        name="ada_mod",
    )(cond, ada_w, ada_b.reshape(depth, 1, N_MOD * D))


def _pre_kernel(x_ref, mod_ref, g_ref, w_ref, cw_ref, cb_ref, csm_ref, p_ref, fl_ref, fc_ref):
    j = pl.program_id(1)
    x = x_ref[...]
    hn = _rms(x, g_ref[...]) * (1.0 + mod_ref[1:2, :]) + mod_ref[0:1, :]
    p = _dot(hn.astype(BF16), w_ref[...])
    p_ref[:, :C_XBC] = p[:, :C_XBC]
    p_ref[:, C_MISC:] = p[:, C_MISC:]

    xbc = p[:, C_XBC:C_MISC]
    period = jnp.where(j == 0, TM, GRID_W)
    pos = lax.broadcasted_iota(I32, (TM, 1), 0) & (period - 1)
    acc = jnp.zeros_like(xbc) + cb_ref[...]
    for k in range(SSD_K):
        d = k - SSD_K // 2
        sh = xbc if d == 0 else pltpu.roll(xbc, (TM - d) % TM, axis=0)
        ok = (pos + d >= 0) & (pos + d < period)
        acc = acc + jnp.where(ok, sh, 0.0) * cw_ref[k:k + 1, :]
    p_ref[:, C_XBC:C_MISC] = _silu(acc)

    xcs = _dot(p[:, C_FOUR:C_FOUR + DG].astype(BF16), csm_ref[...])

    @pl.when(j == 0)
    def _():
        fc_ref[0] = xcs[:, :DG].astype(BF16)
        fc_ref[1] = xcs[:, DG:].astype(BF16)

    @pl.when(j > 0)
    def _():
        fl_ref[0] = xcs[:, :DG].astype(BF16)
        fl_ref[1] = xcs[:, DG:].astype(BF16)


def _pre(x_all, mod, g, w_in_p, conv_w, conv_b, csm, seq, ctx):
    b, s, _ = x_all.shape
    nt = s // TM
    return pl.pallas_call(
        _pre_kernel,
        grid=(b, nt),
        in_specs=[pl.BlockSpec((None, TM, D), lambda i, j: (i, j, 0)),
                  pl.BlockSpec((None, None, 8, D), lambda i, j: (i, jnp.minimum(j, 1), 0, 0)),
                  pl.BlockSpec((1, D), lambda i, j: (0, 0)),
                  pl.BlockSpec((D, D_P), lambda i, j: (0, 0)),
                  pl.BlockSpec((8, 512), lambda i, j: (0, 0)),
                  pl.BlockSpec((1, 512), lambda i, j: (0, 0)),
                  pl.BlockSpec((DG, 2 * DG), lambda i, j: (0, 0))],
        out_specs=[pl.BlockSpec((None, TM, D_P), lambda i, j: (i, j, 0)),
                   pl.BlockSpec((None, 2, TM, DG), lambda i, j: (i, 0, jnp.maximum(j - 1, 0), 0)),
                   pl.BlockSpec((None, 2, ctx, DG), lambda i, j: (i, 0, 0, 0))],
        out_shape=[jax.ShapeDtypeStruct((b, s, D_P), F32),
                   jax.ShapeDtypeStruct((b, 2, seq, DG), BF16),
                   jax.ShapeDtypeStruct((b, 2, ctx, DG), BF16)],
        compiler_params=_cparams(2, VMEM_LIMIT),
        name="pre_mixer",
    )(x_all, mod, g, w_in_p, conv_w, conv_b, csm)


def _four_kernel(f_ref, x_ref, w_ref, o_ref):
    y = _dot(f_ref[...], x_ref[...])
    o_ref[...] = _dot(y.astype(BF16), w_ref[...])


def _fourier(fmat, x2, four_w):
    ln = fmat.shape[0]
    b = x2.shape[0]
    tf = min(512, ln)
    return pl.pallas_call(
        _four_kernel,
        grid=(ln // tf, b),
        in_specs=[pl.BlockSpec((tf, 2 * ln), lambda i, j: (i, 0)),
                  pl.BlockSpec((None, 2 * ln, DG), lambda i, j: (j, 0, 0)),
                  pl.BlockSpec((DG, DG), lambda i, j: (0, 0))],
        out_specs=pl.BlockSpec((None, tf, DG), lambda i, j: (j, i, 0)),
        out_shape=jax.ShapeDtypeStruct((b, ln, DG), F32),
        compiler_params=_cparams(2, VMEM_LIMIT),
        name="fourier_dft",
    )(fmat, x2, four_w)


def _dft_matrix(ln):
    f = jnp.arange(ln, dtype=I32)[:, None]
    t = jnp.arange(ln, dtype=I32)[None, :]
    ang = ((f * t) % ln).astype(F32) * (2.0 * math.pi / ln)
    scale = 1.0 / math.sqrt(ln * 64.0)
    return (jnp.concatenate([jnp.cos(ang), jnp.sin(ang)], axis=1) * scale).astype(BF16)


def _channel_dft():
    i = np.arange(DG)
    same = (i[:, None] // 64) == (i[None, :] // 64)
    ang = 2.0 * np.pi * ((i[:, None] % 64) * (i[None, :] % 64) % 64) / 64.0
    c = np.where(same, np.cos(ang), 0.0)
    s = np.where(same, np.sin(ang), 0.0)
    return jnp.asarray(np.concatenate([c, -s], axis=1), dtype=BF16)


def _s5_kernel(u_ref, p_ref, mq_ref, a_ref, y_ref, hl_ref, he_ref, *, nb, nch, ncc):
    u = u_ref[...].astype(BF16)
    hl_ref[...] = _dot(u, p_ref[...])
    a1f, a2f = a_ref[0:1, :LANES], a_ref[1:2, :LANES]
    a1b, a2b = a_ref[0:1, LANES:], a_ref[1:2, LANES:]

    def step(i, carry):
        hf, hb = carry
        rf = pl.multiple_of(i * nb, nb)
        nbk = jnp.where(i < ncc, ncc - 1 - i, nch + ncc - 1 - i)
        rb = pl.multiple_of(nbk * nb, nb)
        he_ref[pl.ds(rf, nb), :LANES] = hf
        he_ref[pl.ds(rb, nb), LANES:] = hb
        hf = a1f * hf + a2f * pltpu.roll(hf, 64, axis=1) + hl_ref[pl.ds(rf, nb), :LANES]
        hb = a1b * hb + a2b * pltpu.roll(hb, 64, axis=1) + hl_ref[pl.ds(rb, nb), LANES:]
        return hf, hb

    z = jnp.zeros((nb, LANES), F32)
    lax.fori_loop(0, nch, step, (z, z))
    y_ref[...] = (_dot(u, mq_ref[:S5_W, :])
                  + _dot(he_ref[...].astype(BF16), mq_ref[S5_W:, :]))


def _s5(u_g, ptab, mq, atab, nb, nch, ncc):
    g, r, _ = u_g.shape
    return pl.pallas_call(
        functools.partial(_s5_kernel, nb=nb, nch=nch, ncc=ncc),
        grid=(g,),
        in_specs=[pl.BlockSpec((None, r, S5_W), lambda i: (i, 0, 0)),
                  pl.BlockSpec((None, S5_W, 2 * LANES), lambda i: (i, 0, 0)),
                  pl.BlockSpec((None, S5_W + 2 * LANES, S5_W), lambda i: (i, 0, 0)),
                  pl.BlockSpec((None, 8, 2 * LANES), lambda i: (i, 0, 0))],
        out_specs=pl.BlockSpec((None, r, S5_W), lambda i: (i, 0, 0)),
        out_shape=jax.ShapeDtypeStruct((g, r, S5_W), F32),
        scratch_shapes=[pltpu.VMEM((r, 2 * LANES), F32), pltpu.VMEM((r, 2 * LANES), F32)],
        compiler_params=_cparams(1, VMEM_LIMIT),
        name="s5_mixer",
    )(u_g, ptab, mq, atab)


def _s5_tables(lam_re, lam_im, log_dt, b_re, b_im, c_re, c_im):
    t = S5_T
    dt = jnp.exp(log_dt)[..., None]
    ldr, ldi = lam_re * dt, lam_im * dt
    jj = jnp.arange(t + 1, dtype=F32)[None, None, :, None]
    mag = jnp.exp(ldr[:, :, None, :] * jj)
    pr, pi = mag * jnp.cos(ldi[:, :, None, :] * jj), mag * jnp.sin(ldi[:, :, None, :] * jj)
    ar, ai = pr[:, :, 1], pi[:, :, 1]
    den = lam_re * lam_re + lam_im * lam_im
    qr = ((ar - 1.0) * lam_re + ai * lam_im) / den
    qi = (ai * lam_re - (ar - 1.0) * lam_im) / den
    bbr = qr[..., None] * b_re - qi[..., None] * b_im
    bbi = qr[..., None] * b_im + qi[..., None] * b_re
    cpr = c_re[:, :, None] * pr[:, :, :, None, :] - c_im[:, :, None] * pi[:, :, :, None, :]
    cpi = c_re[:, :, None] * pi[:, :, :, None, :] + c_im[:, :, None] * pr[:, :, :, None, :]
    klag = (jnp.einsum('rgjop,rgph->rgjoh', cpr[:, :, :t], bbr, precision=HI)
            - jnp.einsum('rgjop,rgph->rgjoh', cpi[:, :, :t], bbi, precision=HI))
    s_i = jnp.arange(t)[:, None]
    t_i = jnp.arange(t)[None, :]

    def toeplitz(k, d):
        m = k[:, jnp.clip(d, 0, t - 1)]
        m = jnp.where((d >= 0)[None, :, :, None, None], m, 0.0)
        return jnp.transpose(m, (0, 1, 4, 2, 3))

    m_sum = (toeplitz(klag[0], t_i - s_i) + toeplitz(klag[1], s_i - t_i)).reshape(S5_G, S5_W, S5_W)
    pbr = pr[..., None] * bbr[:, :, None] - pi[..., None] * bbi[:, :, None]
    pbi = pr[..., None] * bbi[:, :, None] + pi[..., None] * bbr[:, :, None]

    def ptab(r, exps):
        re = jnp.transpose(pbr[r][:, exps], (0, 1, 3, 2))
        im = jnp.transpose(pbi[r][:, exps], (0, 1, 3, 2))
        return jnp.concatenate([re, im], axis=-1).reshape(S5_G, S5_W, 2 * S5_P)

    ar_t = np.arange(t)
    p2 = jnp.concatenate([ptab(0, t - 1 - ar_t), ptab(1, ar_t)], axis=-1)

    def qtab(r, exps):
        re = jnp.transpose(cpr[r][:, exps], (0, 3, 1, 2)).reshape(S5_G, S5_P, S5_W)
        im = jnp.transpose(cpi[r][:, exps], (0, 3, 1, 2)).reshape(S5_G, S5_P, S5_W)
        return jnp.concatenate([re, -im], axis=1)

    mq = jnp.concatenate([m_sum, qtab(0, ar_t + 1), qtab(1, t - ar_t)], axis=1)
    art, ait = pr[:, :, t], pi[:, :, t]
    a1 = jnp.concatenate([art[0], art[0], art[1], art[1]], axis=-1)
    a2 = jnp.concatenate([-ait[0], ait[0], -ait[1], ait[1]], axis=-1)
    atab = jnp.concatenate([a1[:, None], a2[:, None], jnp.zeros((S5_G, 6, 4 * S5_P), F32)], axis=1)
    return p2.astype(BF16), mq.astype(BF16), atab


def _split3(x):
    x1 = x.astype(BF16)
    r1 = x - x1.astype(F32)
    x2 = r1.astype(BF16)
    x3 = (r1 - x2.astype(F32)).astype(BF16)
    return x1, x2, x3


def _dot01(m, x):
    x1, x2, x3 = _split3(x)
    return (_dot(m, x3) + _dot(m, x2)) + _dot(m, x1)


def _tile_cumsum_mask(rev):
    t = lax.broadcasted_iota(I32, (TM, TM), 0)
    s = lax.broadcasted_iota(I32, (TM, TM), 1)
    order = (s >= t) if rev else (s <= t)
    return (order & ((t // CHUNK) == (s // CHUNK))).astype(BF16)


def _head_lane_mask(rev, width):
    t = lax.broadcasted_iota(I32, (CHUNK, width), 0)
    s = lax.broadcasted_iota(I32, (CHUNK, width), 1) % CHUNK
    return (s >= t) if rev else (s <= t)


def _gla_kernel(qk_ref, v_ref, misc_ref, a2_ref, ab_ref, o_ref, st_ref, *, rev):
    j = pl.program_id(1)

    @pl.when(j == 0)
    def _():
        st_ref[...] = jnp.zeros_like(st_ref)

    lane_k = lax.broadcasted_iota(I32, (1, LANES), 1) // GLA_DK
    lane_v = lax.broadcasted_iota(I32, (1, DG), 1) // GLA_DV
    causal = _head_lane_mask(rev, DG)
    st_diag = (lax.broadcasted_iota(I32, (DG, 1), 0) // GLA_DV) == lane_k

    z = _dot(misc_ref[...].astype(BF16), a2_ref[...]) + ab_ref[...]
    la_t = -(jnp.maximum(-z, 0.0) + jnp.log1p(jnp.exp(-jnp.abs(z)))) * (1.0 / GLA_TAU)
    b_t = _dot01(_tile_cumsum_mask(rev), la_t)
    st = st_ref[...]
    chunks = range(TM // CHUNK)
    for c in (reversed(chunks) if rev else chunks):
        rows = slice(c * CHUNK, (c + 1) * CHUNK)
        la, bcum = la_t[rows], b_t[rows]
        q = qk_ref[rows, :LANES] * (GLA_DK ** -0.5)
        k = qk_ref[rows, LANES:]
        v = v_ref[rows, :]
        tot = jnp.sum(la, axis=0, keepdims=True)
        q_in = (q * jnp.exp(bcum)).astype(BF16)
        k_in = k * jnp.exp(-bcum)
        k_dec = (k * jnp.exp(tot - bcum)).astype(BF16)
        k4 = jnp.concatenate([jnp.where(lane_k == h, k_in, 0.0) for h in range(GLA_H)], axis=0)
        att = jnp.where(causal, _dot(q_in, k4.astype(BF16), _NT), 0.0)
        vb = v.astype(BF16)
        vbd = jnp.concatenate([jnp.where(lane_v == h, vb, jnp.zeros_like(vb)) for h in range(GLA_H)], axis=0)
        o = _dot(att.astype(BF16), vbd) + _dot(q_in, jnp.where(st_diag, st, 0.0).astype(BF16), _NT)
        o_ref[rows, :] = o
        st = st * jnp.exp(tot) + _dot(vb, k_dec, _TN)
    st_ref[...] = st


def _tile_order(rev, j, nt):
    return jnp.where(j == 0, 0, nt - j) if rev else j


def _gla(p, a2m, abm, rev):
    b, s, _ = p.shape
    nt = s // TM
    return pl.pallas_call(
        functools.partial(_gla_kernel, rev=rev),
        grid=(b, nt),
        in_specs=[pl.BlockSpec((None, TM, DG), lambda i, j: (i, _tile_order(rev, j, nt), C_QK // DG)),
                  pl.BlockSpec((None, TM, DG), lambda i, j: (i, _tile_order(rev, j, nt), C_V // DG)),
                  pl.BlockSpec((None, TM, LANES), lambda i, j: (i, _tile_order(rev, j, nt), C_MISC // LANES)),
                  pl.BlockSpec((LANES, LANES), lambda i, j: (0, 0)),
                  pl.BlockSpec((1, LANES), lambda i, j: (0, 0))],
        out_specs=pl.BlockSpec((None, TM, DG), lambda i, j: (i, _tile_order(rev, j, nt), 0)),
        out_shape=jax.ShapeDtypeStruct((b, s, DG), F32),
        scratch_shapes=[pltpu.VMEM((DG, LANES), F32)],
        compiler_params=_cparams(2, VMEM_LIMIT),
        name="gla_bwd" if rev else "gla_fwd",
    )(p, p, p, a2m, abm)


def _ssd_kernel(xbc_ref, misc_ref, dtb_ref, av_ref, y_ref, sb_ref, *, rev):
    j = pl.program_id(1)

    @pl.when(j == 0)
    def _():
        sb_ref[...] = jnp.zeros_like(sb_ref)

    lane = lax.broadcasted_iota(I32, (1, LANES), 1)
    lane_g = lane // SSD_N
    lane_h = lax.broadcasted_iota(I32, (1, DG), 1) // SSD_HD
    row_h = lax.broadcasted_iota(I32, (DG, 1), 0) // SSD_HD
    blockmask = row_h == lane_h
    causal = _head_lane_mask(rev, DG)
    diag = (lax.broadcasted_iota(I32, (CHUNK, DG), 0)
            == lax.broadcasted_iota(I32, (CHUNK, DG), 1) % CHUNK)

    def per_head(x):
        xr = pltpu.roll(x, SSD_N, axis=1)
        return jnp.concatenate([jnp.where(lane < SSD_N, x, xr), jnp.where(lane < SSD_N, xr, x)], axis=1)

    zm = misc_ref[...] + dtb_ref[...]
    sp = jnp.maximum(zm, 0.0) + jnp.log1p(jnp.exp(-jnp.abs(zm)))
    dt_t = jnp.zeros((TM, DG), F32)
    for h in range(SSD_H):
        col = jnp.sum(jnp.where(lane == MISC_DT + SSD_H * int(rev) + h, sp, 0.0), axis=1, keepdims=True)
        dt_t = jnp.where(lane_h == h, col, dt_t)
    da_t = dt_t * av_ref[...]
    cum_t = _dot01(_tile_cumsum_mask(rev), da_t)

    sb = sb_ref[...]
    chunks = range(TM // CHUNK)
    for c in (reversed(chunks) if rev else chunks):
        rows = slice(c * CHUNK, (c + 1) * CHUNK)
        da, cum = da_t[rows], cum_t[rows]
        bmc = xbc_ref[rows, DG:DG + LANES]
        cmc = xbc_ref[rows, DG + LANES:]
        xdt = (xbc_ref[rows, :DG] * dt_t[rows]).astype(BF16)
        tot = jnp.sum(da, axis=0, keepdims=True)
        cum_row = jnp.sum(jnp.where(diag, cum, 0.0), axis=0, keepdims=True)
        decay = jnp.where(causal, jnp.exp(jnp.minimum(cum - cum_row, 0.0)), 0.0)
        b4 = jnp.concatenate([jnp.where(lane_g == h // (SSD_H // SSD_G), bmc, 0.0) for h in range(SSD_H)],
                             axis=0)
        scores = _dot(cmc.astype(BF16), b4.astype(BF16), _NT) * decay
        xbd = jnp.concatenate([jnp.where(lane_h == h, xdt, jnp.zeros_like(xdt)) for h in range(SSD_H)],
                              axis=0)
        cdec = per_head(cmc) * jnp.exp(cum)
        y = _dot(scores.astype(BF16), xbd) + _dot(cdec.astype(BF16), sb.astype(BF16))
        y_ref[rows, :] = y
        bdec = per_head(bmc) * jnp.exp(tot - cum)
        upd = _dot(bdec.astype(BF16), xdt, _TN)
        sb = sb * jnp.exp(tot) + jnp.where(blockmask, upd, 0.0)
    sb_ref[...] = sb


def _ssd(p, dtb, av, rev):
    b, s, _ = p.shape
    nt = s // TM
    return pl.pallas_call(
        functools.partial(_ssd_kernel, rev=rev),
        grid=(b, nt),
        in_specs=[pl.BlockSpec((None, TM, 512), lambda i, j: (i, _tile_order(rev, j, nt), C_XBC // 512)),
                  pl.BlockSpec((None, TM, LANES), lambda i, j: (i, _tile_order(rev, j, nt), C_MISC // LANES)),
                  pl.BlockSpec((1, LANES), lambda i, j: (0, 0)),
                  pl.BlockSpec((1, DG), lambda i, j: (0, 0))],
        out_specs=pl.BlockSpec((None, TM, DG), lambda i, j: (i, _tile_order(rev, j, nt), 0)),
        out_shape=jax.ShapeDtypeStruct((b, s, DG), F32),
        scratch_shapes=[pltpu.VMEM((DG, DG), F32)],
        compiler_params=_cparams(2, VMEM_LIMIT),
        name="ssd_bwd" if rev else "ssd_fwd",
    )(p, p, dtb, av)


def _ssd_tables(a_log, dt_bias, r):
    dtb = jnp.zeros((1, LANES), F32).at[0, MISC_DT + SSD_H * r:MISC_DT + SSD_H * (r + 1)].set(dt_bias[r])
    av = jnp.repeat(-jnp.exp(a_log[r]), SSD_HD)[None, :]
    return dtb, av


def _gelu_tanh(x):
    return 0.5 * x * (1.0 + jnp.tanh(math.sqrt(2.0 / math.pi) * (x + 0.044715 * x * x * x)))


def _post_kernel(x_ref, mod_ref, us5_ref, r_ref, z_ref, xs_ref, fl_ref, fc_ref, s5y_ref,
                 gf_ref, gb_ref, sf_ref, sbk_ref, vec_ref, wglu_ref, avg_ref, wout_ref,
                 gffn_ref, rw_ref, rb_ref, tri_ref,
                 xo_ref, hp_ref, ri_ref, rg_ref, cnt_ref, cnt_sc):
    i = pl.program_id(0)
    j = pl.program_id(1)

    @pl.when((i == 0) & (j == 0))
    def _():
        cnt_sc[...] = jnp.zeros_like(cnt_sc)

    s5_d, gla_g, ssd_d, ssd_g = vec_ref[0:1, :], vec_ref[1:2, :], vec_ref[2:3, :], vec_ref[3:4, :]
    four = jnp.where(j == 0, fc_ref[...], fl_ref[...])
    y5 = s5y_ref[...] + s5_d * us5_ref[...]
    act = _gelu_tanh(y5)
    s5o = act * jax.nn.sigmoid(_dot(act.astype(BF16), wglu_ref[...]))
    og = gf_ref[...] + gb_ref[...]
    ms = _dot(og * og, avg_ref[...], precision=HI)
    glo = og * lax.rsqrt(ms + EPS) * gla_g * _silu(r_ref[...])
    ys = (sf_ref[...] + sbk_ref[...] + ssd_d * xs_ref[...]) * _silu(z_ref[...])
    sso = _rms(ys, ssd_g)
    mix = jnp.concatenate([four, s5o, glo, sso], axis=1).astype(BF16)
    x = x_ref[...] + mod_ref[2:3, :] * _dot(mix, wout_ref[...])
    xo_ref[...] = x
    h = _rms(x, gffn_ref[...]) * (1.0 + mod_ref[4:5, :]) + mod_ref[3:4, :]
    hp_ref[...] = _pack_bf16_pairs(h)

    logits = _dot(h, rw_ref[...], precision=HI) + rb_ref[...]
    lane = lax.broadcasted_iota(I32, (TM, LANES), 1)
    vals, idxs = [], []
    for _k in range(TOP_K):
        mx = jnp.max(logits, axis=1, keepdims=True)
        ix = jnp.min(jnp.where(logits == mx, lane, LANES), axis=1, keepdims=True)
        vals.append(mx)
        idxs.append(ix)
        logits = jnp.where(lane == ix, -jnp.inf, logits)
    es = [jnp.exp(v - vals[0]) for v in vals]
    den = es[0] + es[1] + es[2] + es[3]
    onehots = [(lane == ix) for ix in idxs]
    oh = sum(o.astype(F32) for o in onehots)
    before = _dot(tri_ref[...], oh.astype(BF16)) + cnt_sc[0:1, :]
    ri = jnp.zeros((TM, LANES), I32)
    rg = jnp.zeros((TM, LANES), F32)
    for k in range(TOP_K):
        rank = jnp.sum(jnp.where(onehots[k], before, 0.0), axis=1, keepdims=True).astype(I32)
        ri = jnp.where(lane == k, idxs[k], ri)
        ri = jnp.where(lane == TOP_K + k, rank, ri)
        rg = jnp.where(lane == k, es[k] / den, rg)
    ri_ref[...] = ri
    rg_ref[...] = rg
    cnt_sc[...] = cnt_sc[...] + jnp.sum(oh, axis=0, keepdims=True)
    cnt_ref[...] = cnt_sc[...]


def _post(x_all, mod, p, four_l, four_c, s5y, gla_f, gla_b, ssd_f, ssd_b, vecs, wglu, avg, wout, gffn,
          rw, rb, tri, seq, ctx):
    b, s, _ = x_all.shape
    nt = s // TM
    tile = lambda w, col: pl.BlockSpec((None, TM, w), lambda i, j: (i, j, col))
    full = lambda a: pl.BlockSpec(a.shape, lambda i, j: (0,) * a.ndim)
    return pl.pallas_call(
        _post_kernel,
        grid=(b, nt),
        in_specs=[tile(D, 0),
                  pl.BlockSpec((None, None, 8, D), lambda i, j: (i, jnp.minimum(j, 1), 0, 0)),
                  tile(DG, C_S5 // DG), tile(DG, C_R // DG), tile(DG, C_Z // DG), tile(DG, C_XBC // DG),
                  pl.BlockSpec((None, TM, DG), lambda i, j: (i, jnp.maximum(j - 1, 0), 0)),
                  pl.BlockSpec((None, ctx, DG), lambda i, j: (i, 0, 0)),
                  tile(DG, 0), tile(DG, 0), tile(DG, 0), tile(DG, 0), tile(DG, 0),
                  full(vecs), full(wglu), full(avg), full(wout), full(gffn), full(rw), full(rb), full(tri)],
        out_specs=[tile(D, 0), tile(D // 2, 0), tile(LANES, 0), tile(LANES, 0),
                   pl.BlockSpec((8, LANES), lambda i, j: (0, 0))],
        out_shape=[jax.ShapeDtypeStruct((b, s, D), F32),
                   jax.ShapeDtypeStruct((b, s, D // 2), U32),
                   jax.ShapeDtypeStruct((b, s, LANES), I32),
                   jax.ShapeDtypeStruct((b, s, LANES), F32),
                   jax.ShapeDtypeStruct((8, LANES), F32)],
        scratch_shapes=[pltpu.VMEM((8, LANES), F32)],
        compiler_params=_cparams(2, VMEM_LIMIT),
        name="post_mixer_router",
    )(x_all, mod, p, p, p, p, four_l, four_c, s5y, gla_f, gla_b, ssd_f, ssd_b,
      vecs, wglu, avg, wout, gffn, rw, rb, tri)


def _expert_kernel(be_ref, nu_ref, nv_ref, src_ref, srcn_ref, dst_ref, hp_ref, wgu_ref, wd_ref,
                   bg_ref, bu_ref, bd_ref, perm_ref, yp_ref,
                   xbuf, ybuf, wg_sc, wu_sc, wd_sc, gsem, ssem):
    i = pl.program_id(0)
    slot = i % 2
    n_used = nu_ref[0]
    used = i < n_used
    new_expert = (i == 0) | (be_ref[i] != be_ref[jnp.maximum(i - 1, 0)])

    def gather_copy(tok, sl, r):
        return pltpu.make_async_copy(hp_ref.at[pl.ds(tok, 1)], xbuf.at[sl, pl.ds(r, 1)], gsem.at[sl])

    def scatter_copy(sl, r, pair):
        return pltpu.make_async_copy(ybuf.at[sl, pl.ds(r, 1)], yp_ref.at[pl.ds(pair, 1)], ssem.at[sl])

    def start_gather(idx_ref, blk, sl):
        def body(r, c):
            gather_copy(idx_ref[0, 0, r], sl, r).start()
            return c
        lax.fori_loop(0, nv_ref[blk], body, 0)

    def wait_rows(copy, count):
        def body(r, c):
            copy.wait()
            return c
        lax.fori_loop(0, count, body, 0)

    @pl.when(i == 0)
    def _():
        xbuf[...] = jnp.zeros_like(xbuf)
        start_gather(src_ref, 0, 0)

    @pl.when(i + 1 < n_used)
    def _():
        start_gather(srcn_ref, i + 1, 1 - slot)

    @pl.when(used)
    def _():
        wait_rows(gather_copy(0, slot, 0), nv_ref[i])

    @pl.when(used & (i >= 2))
    def _():
        wait_rows(scatter_copy(slot, 0, 0), nv_ref[jnp.maximum(i - 2, 0)])

    @pl.when(used & new_expert)
    def _():
        for c in range(2 * D // DG):
            d = _dot(wgu_ref[:, c * DG:(c + 1) * DG].astype(BF16), perm_ref[...])
            wg_sc[:, c * LANES:(c + 1) * LANES] = d[:, :LANES].astype(BF16)
            wu_sc[:, c * LANES:(c + 1) * LANES] = d[:, LANES:].astype(BF16)
        wd_sc[...] = wd_ref[...].astype(BF16)

    @pl.when(used)
    def _():
        x = _unpack_bf16_pairs(xbuf[slot]).astype(BF16)
        g = _dot(x, wg_sc[...]) + bg_ref[...]
        u = _dot(x, wu_sc[...]) + bu_ref[...]
        gate = jnp.minimum(g, LIMIT)
        up = jnp.clip(u, -LIMIT, LIMIT)
        act = (up + 1.0) * gate * jax.nn.sigmoid(ALPHA * gate)
        y = _dot(act.astype(BF16), wd_sc[...]) + bd_ref[...]
        ybuf[slot] = _pack_bf16_pairs(y)

        def body(r, c):
            scatter_copy(slot, r, dst_ref[0, 0, r]).start()
            return c
        lax.fori_loop(0, nv_ref[i], body, 0)

    @pl.when(used & (i == n_used - 1))
    def _():
        wait_rows(scatter_copy(slot, 0, 0), nv_ref[i])

    @pl.when(used & (i == n_used - 1) & (i >= 1))
    def _():
        wait_rows(scatter_copy(1 - slot, 0, 0), nv_ref[jnp.maximum(i - 1, 0)])


def _deinterleave_perm():
    pm = np.zeros((DG, DG), np.float32)
    o = np.arange(LANES)
    pm[2 * o, o] = 1.0
    pm[2 * o + 1, LANES + o] = 1.0
    return jnp.asarray(pm, BF16)


def _experts(blk_e, n_used, n_valid, src3, dst3, hp, wgu, wd, bg, bu, bd):
    nblk = src3.shape[0]
    last = lambda i, nu: jnp.minimum(i, nu[0] - 1)
    wsel = lambda i, be, nu, nv: (be[last(i, nu)], 0, 0)
    ispec = lambda f: pl.BlockSpec((1, 1, BM), f, memory_space=pltpu.SMEM)
    grid_spec = pltpu.PrefetchScalarGridSpec(
        num_scalar_prefetch=3,
        grid=(nblk,),
        in_specs=[ispec(lambda i, be, nu, nv: (last(i, nu), 0, 0)),
                  ispec(lambda i, be, nu, nv: (last(i + 1, nu), 0, 0)),
                  ispec(lambda i, be, nu, nv: (last(i, nu), 0, 0)),
                  pl.BlockSpec(memory_space=pl.ANY),
                  pl.BlockSpec((None, D, 2 * D), wsel), pl.BlockSpec((None, D, D), wsel),
                  pl.BlockSpec((None, 1, D), wsel), pl.BlockSpec((None, 1, D), wsel),
                  pl.BlockSpec((None, 1, D), wsel),
                  pl.BlockSpec((DG, DG), lambda i, be, nu, nv: (0, 0))],
        out_specs=pl.BlockSpec(memory_space=pl.ANY),
        scratch_shapes=[pltpu.VMEM((2, BM, D // 2), U32), pltpu.VMEM((2, BM, D // 2), U32),
                        pltpu.VMEM((D, D), BF16), pltpu.VMEM((D, D), BF16), pltpu.VMEM((D, D), BF16),
                        pltpu.SemaphoreType.DMA((2,)), pltpu.SemaphoreType.DMA((2,))],
    )
    return pl.pallas_call(
        _expert_kernel,
        grid_spec=grid_spec,
        out_shape=jax.ShapeDtypeStruct((hp.shape[0] * TOP_K, D // 2), U32),
        compiler_params=_cparams(1, VMEM_LIMIT),
        name="moe_experts",
    )(blk_e, n_used, n_valid, src3, src3, dst3, hp, wgu, wd, bg, bu, bd, _deinterleave_perm())


def _combine_kernel(x_ref, rg_ref, mod_ref, fg_ref, yp_ref, o_ref, *, final):
    f = jnp.zeros((TM, D), F32)
    for k in range(TOP_K):
        f = f + rg_ref[:, k:k + 1] * _unpack_bf16_pairs(yp_ref[k])
    x = x_ref[...] + mod_ref[5:6, :] * f
    o_ref[...] = _rms(x, fg_ref[...]) if final else x


def _combine(x_flat, rg_flat, mod, fg, ypairs, nt, final):
    n = x_flat.shape[0]
    return pl.pallas_call(
        functools.partial(_combine_kernel, final=final),
        grid=(n // TM,),
        in_specs=[pl.BlockSpec((TM, D), lambda i: (i, 0)),
                  pl.BlockSpec((TM, LANES), lambda i: (i, 0)),
                  pl.BlockSpec((None, None, 8, D), lambda i: (i // nt, jnp.minimum(i % nt, 1), 0, 0)),
                  pl.BlockSpec((1, D), lambda i: (0, 0)),
                  pl.BlockSpec((TOP_K, TM, D // 2), lambda i: (0, i, 0))],
        out_specs=pl.BlockSpec((TM, D), lambda i: (i, 0)),
        out_shape=jax.ShapeDtypeStruct((n, D), F32),
        compiler_params=_cparams(1, VMEM_LIMIT),
        name="moe_combine",
    )(x_flat, rg_flat, mod, fg, ypairs)


def _repack_w_in(w):
    four, s5, q, k, v, r, a1, z, xbc, dt = jnp.split(
        w, np.cumsum([256, 256, 128, 128, 256, 256, 32, 256, 512, 8])[:-1].tolist(), axis=-1)
    misc = jnp.concatenate([a1, dt, jnp.zeros((D, LANES - 40), w.dtype)], axis=-1)
    return jnp.concatenate([four, s5, q, k, v, r, z, xbc, misc], axis=-1).astype(BF16)


def kernel(x, c, ctx, c_ctx, ada_w, ada_b, norm_mix_g, norm_ffn_g, w_in, w_out, four_w, s5_lam_re, s5_lam_im, s5_log_dt, s5_b_re, s5_b_im, s5_c_re, s5_c_im, s5_d, s5_w_glu, gla_a2, gla_a_b, gla_norm_g, ssd_conv_w, ssd_conv_b, ssd_a_log, ssd_dt_bias, ssd_d, ssd_norm_g, router_w, router_b, exp_w_gu, exp_b_gu, exp_w_dn, exp_b_dn, final_g):
    bsz, seq, _ = x.shape
    clen = ctx.shape[1]
    depth = ada_w.shape[0]
    assert clen == TM and seq % TM == 0 and bsz == 8
    s = clen + seq
    nt = s // TM
    n_tok = bsz * s
    nch, ncc = s // S5_T, clen // S5_T

    cond = jnp.concatenate([c, c_ctx[None, :], jnp.zeros((16 - bsz - 1, D), F32)], axis=0)
    mods = _ada(cond, ada_w, ada_b).reshape(depth, 16, N_MOD, D)
    m_lat = mods[:, :bsz]
    m_ctx = jnp.broadcast_to(mods[:, bsz:bsz + 1], m_lat.shape)
    mods = jnp.stack([m_ctx, m_lat], axis=2)
    mods = jnp.pad(mods, ((0, 0), (0, 0), (0, 0), (0, 2), (0, 0)))

    x_all = jnp.concatenate([ctx, x], axis=1)
    csm = _channel_dft()
    f_lat, f_ctx = _dft_matrix(seq), _dft_matrix(clen)
    avg = jnp.asarray(np.kron(np.eye(GLA_H), np.full((GLA_DV, GLA_DV), 1.0 / GLA_DV)), F32)
    tri = jnp.asarray(np.tril(np.ones((TM, TM)), -1), BF16)

    n_rows = n_tok * TOP_K + N_EXP * BM
    nblk = n_rows // BM

    for l in range(depth):
        mod = mods[l]
        w_in_p = _repack_w_in(w_in[l])
        conv_w = jnp.pad(ssd_conv_w[l], ((0, 8 - SSD_K), (0, 0)))
        p, four_l, four_c = _pre(x_all, mod, norm_mix_g[l][None], w_in_p, conv_w, ssd_conv_b[l][None],
                                 csm, seq, clen)
        fw = four_w[l].astype(BF16)
        fy_l = _fourier(f_lat, four_l.reshape(bsz, 2 * seq, DG), fw)
        fy_c = _fourier(f_ctx, four_c.reshape(bsz, 2 * clen, DG), fw)

        ptab, mq, atab = _s5_tables(s5_lam_re[l], s5_lam_im[l], s5_log_dt[l], s5_b_re[l], s5_b_im[l],
                                    s5_c_re[l], s5_c_im[l])
        u5 = p[:, :, C_S5:C_S5 + DG].reshape(bsz, nch, S5_T, S5_G, S5_CH)
        u5 = jnp.transpose(u5, (3, 1, 0, 2, 4)).reshape(S5_G, nch * bsz, S5_W)
        y5 = _s5(u5, ptab, mq, atab, bsz, nch, ncc).reshape(S5_G, nch, bsz, S5_T, S5_CH)
        y5 = jnp.transpose(y5, (2, 1, 3, 0, 4)).reshape(bsz, s, DG)

        gla_o, ssd_y = [], []
        for r in range(2):
            a2m = jnp.zeros((LANES, LANES), F32).at[GLA_RANK * r:GLA_RANK * (r + 1)].set(gla_a2[l, r])
            gla_o.append(_gla(p, a2m.astype(BF16), gla_a_b[l, r][None, :], rev=bool(r)))
            ssd_y.append(_ssd(p, *_ssd_tables(ssd_a_log[l], ssd_dt_bias[l], r), rev=bool(r)))

        vecs = jnp.stack([s5_d[l], jnp.tile(gla_norm_g[l], GLA_H), jnp.repeat(ssd_d[l], SSD_HD),
                          ssd_norm_g[l]] + [jnp.zeros((DG,), F32)] * 4)
        rw = jnp.pad(router_w[l], ((0, 0), (0, LANES - N_EXP)))
        rb = jnp.concatenate([router_b[l], jnp.full((LANES - N_EXP,), -1e30, F32)])[None]
        x_mid, hp, ri, rg, cnt = _post(x_all, mod, p, fy_l, fy_c, y5, gla_o[0], gla_o[1], ssd_y[0], ssd_y[1], vecs,
                                       s5_w_glu[l].astype(BF16), avg, w_out[l].astype(BF16),
                                       norm_ffn_g[l][None], rw, rb, tri, seq, clen)

        counts = cnt[0, :N_EXP].astype(I32)
        padded = (counts + BM - 1) // BM * BM
        pad_end = jnp.cumsum(padded)
        pad_start = pad_end - padded
        ri = ri.reshape(n_tok, LANES)
        dest = pad_start[ri[:, :TOP_K]] + ri[:, TOP_K:2 * TOP_K]
        blk_start = jnp.arange(nblk, dtype=I32) * BM
        blk_e = jnp.minimum(jnp.sum((pad_end[None, :] <= blk_start[:, None]).astype(I32), axis=1), N_EXP - 1)
        n_used = (pad_end[-1] // BM).astype(I32)[None]
        n_valid = jnp.clip(pad_start[blk_e] + counts[blk_e] - blk_start, 0, BM)
        n_valid = jnp.where(blk_start < pad_end[-1], n_valid, 0).astype(I32)
        pair_of_row = jnp.zeros((n_rows,), I32).at[dest.reshape(-1)].set(
            jnp.arange(n_tok * TOP_K, dtype=I32), unique_indices=True)
        src3 = (pair_of_row // TOP_K).reshape(nblk, 1, BM)
        dst3 = ((pair_of_row % TOP_K) * n_tok + pair_of_row // TOP_K).reshape(nblk, 1, BM)

        bgu = exp_b_gu[l].reshape(N_EXP, 1, D, 2)
        ypairs = _experts(blk_e, n_used, n_valid, src3, dst3, hp.reshape(n_tok, D // 2), exp_w_gu[l],
                          exp_w_dn[l], bgu[..., 0], bgu[..., 1], exp_b_dn[l][:, None, :])
        x_all = _combine(x_mid.reshape(n_tok, D), rg.reshape(n_tok, LANES), mod, final_g[None],
                         ypairs.reshape(TOP_K, n_tok, D // 2), nt, final=(l == depth - 1)).reshape(bsz, s, D)

    return x_all[:, clen:, :]
```

```python
import functools
import math

import jax
import jax.numpy as jnp
import numpy as np
from jax import lax
from jax.experimental import pallas as pl
from jax.experimental.pallas import tpu as pltpu

F32 = jnp.float32
BF16 = jnp.bfloat16
I32 = jnp.int32
U32 = jnp.uint32
HI = lax.Precision.HIGHEST

D = 1024
DG = 256
N_MOD = 6
GRID_W = 64
EPS = 1e-6

S5_CH = 16
S5_G = DG // S5_CH
S5_P = 64
S5_T = 32
S5_W = S5_T * S5_CH

GLA_H = 4
GLA_DK = 32
GLA_DV = 64
GLA_RANK = 16
GLA_TAU = 16.0
CHUNK = 64

SSD_H = 4
SSD_HD = 64
SSD_G = 2
SSD_N = 64
SSD_K = 5

N_EXP = 32
TOP_K = 4
LIMIT = 7.0
ALPHA = 1.702

TM = 256
BM = 512
SEG = 8
LMAX = 1280
LANES = 128

C_FOUR, C_S5, C_QK, C_V, C_R, C_Z, C_XBC, C_MISC = 0, 256, 512, 768, 1024, 1280, 1536, 2048
D_P = 2176
MISC_A1 = 0
MISC_DT = 32

VMEM_LIMIT = 56 * 1024 * 1024


def _cparams(n_axes, vmem=None):
    return pltpu.CompilerParams(dimension_semantics=("arbitrary",) * n_axes,
                                vmem_limit_bytes=vmem)


def _silu(x):
    return x * jax.nn.sigmoid(x)


def _rms(x, g):
    return x * lax.rsqrt(jnp.mean(x * x, axis=-1, keepdims=True) + EPS) * g


def _dot(a, b, dims=None, precision=None):
    if dims is None:
        dims = (((a.ndim - 1,), (0,)), ((), ()))
    return lax.dot_general(a, b, dims, precision=precision, preferred_element_type=F32)


_NT = (((1,), (1,)), ((), ()))
_TN = (((0,), (0,)), ((), ()))


def _pack_bf16_pairs(a):
    n = a.shape[1] // 2
    hi = pltpu.bitcast(a[:, :n].astype(BF16).astype(F32), U32)
    lo = pltpu.bitcast(a[:, n:].astype(BF16).astype(F32), U32)
    return hi | (lo >> 16)


def _unpack_bf16_pairs(w):
    hi = pltpu.bitcast(w & jnp.uint32(0xFFFF0000), F32)
    lo = pltpu.bitcast(w << 16, F32)
    return jnp.concatenate([hi, lo], axis=1)


def _ada_kernel(cond_ref, w_ref, b_ref, o_ref):
    c = _silu(cond_ref[...])
    o_ref[...] = _dot(c, w_ref[...], precision=HI) + b_ref[...]


def _ada(cond, ada_w, ada_b):
    depth = ada_w.shape[0]
    rows = cond.shape[0]
    tn = 1536
    return pl.pallas_call(
        _ada_kernel,
        grid=(depth, (N_MOD * D) // tn),
        in_specs=[pl.BlockSpec((rows, D), lambda l, n: (0, 0)),
                  pl.BlockSpec((None, D, tn), lambda l, n: (l, 0, n)),
                  pl.BlockSpec((None, 1, tn), lambda l, n: (l, 0, n))],
        out_specs=pl.BlockSpec((None, rows, tn), lambda l, n: (l, 0, n)),
        out_shape=jax.ShapeDtypeStruct((depth, rows, N_MOD * D), F32),
        compiler_params=_cparams(2, VMEM_LIMIT),
        name="ada_mod",
    )(cond, ada_w, ada_b.reshape(depth, 1, N_MOD * D))


def _pre_kernel(x_ref, mod_ref, g_ref, w_ref, cw_ref, cb_ref, csm_ref, p_ref, fl_ref, fc_ref):
    j = pl.program_id(1)
    x = x_ref[...]
    hn = _rms(x, g_ref[...]) * (1.0 + mod_ref[1:2, :]) + mod_ref[0:1, :]
    p = _dot(hn.astype(BF16), w_ref[...])
    p_ref[:, :C_XBC] = p[:, :C_XBC]
    p_ref[:, C_MISC:] = p[:, C_MISC:]

    xbc = p[:, C_XBC:C_MISC]
    period = jnp.where(j == 0, TM, GRID_W)
    pos = lax.broadcasted_iota(I32, (TM, 1), 0) & (period - 1)
    acc = jnp.zeros_like(xbc) + cb_ref[...]
    for k in range(SSD_K):
        d = k - SSD_K // 2
        sh = xbc if d == 0 else pltpu.roll(xbc, (TM - d) % TM, axis=0)
        ok = (pos + d >= 0) & (pos + d < period)
        acc = acc + jnp.where(ok, sh, 0.0) * cw_ref[k:k + 1, :]
    p_ref[:, C_XBC:C_MISC] = _silu(acc)

    xcs = _dot(p[:, C_FOUR:C_FOUR + DG].astype(BF16), csm_ref[...])

    @pl.when(j == 0)
    def _():
        fc_ref[0] = xcs[:, :DG].astype(BF16)
        fc_ref[1] = xcs[:, DG:].astype(BF16)

    @pl.when(j > 0)
    def _():
        fl_ref[0] = xcs[:, :DG].astype(BF16)
        fl_ref[1] = xcs[:, DG:].astype(BF16)


def _pre(x_all, mod, g, w_in_p, conv_w, conv_b, csm, seq, ctx):
    b, s, _ = x_all.shape
    nt = s // TM
    return pl.pallas_call(
        _pre_kernel,
        grid=(b, nt),
        in_specs=[pl.BlockSpec((None, TM, D), lambda i, j: (i, j, 0)),
                  pl.BlockSpec((None, None, 8, D), lambda i, j: (i, jnp.minimum(j, 1), 0, 0)),
                  pl.BlockSpec((1, D), lambda i, j: (0, 0)),
                  pl.BlockSpec((D, D_P), lambda i, j: (0, 0)),
                  pl.BlockSpec((8, 512), lambda i, j: (0, 0)),
                  pl.BlockSpec((1, 512), lambda i, j: (0, 0)),
                  pl.BlockSpec((DG, 2 * DG), lambda i, j: (0, 0))],
        out_specs=[pl.BlockSpec((None, TM, D_P), lambda i, j: (i, j, 0)),
                   pl.BlockSpec((None, 2, TM, DG), lambda i, j: (i, 0, jnp.maximum(j - 1, 0), 0)),
                   pl.BlockSpec((None, 2, ctx, DG), lambda i, j: (i, 0, 0, 0))],
        out_shape=[jax.ShapeDtypeStruct((b, s, D_P), F32),
                   jax.ShapeDtypeStruct((b, 2, seq, DG), BF16),
                   jax.ShapeDtypeStruct((b, 2, ctx, DG), BF16)],
        compiler_params=_cparams(2, VMEM_LIMIT),
        name="pre_mixer",
    )(x_all, mod, g, w_in_p, conv_w, conv_b, csm)


def _four_kernel(f_ref, x_ref, w_ref, o_ref):
    y = _dot(f_ref[...], x_ref[...])
    o_ref[...] = _dot(y.astype(BF16), w_ref[...])


def _fourier(fmat, x2, four_w):
    ln = fmat.shape[0]
    b = x2.shape[0]
    tf = min(512, ln)
    return pl.pallas_call(
        _four_kernel,
        grid=(ln // tf, b),
        in_specs=[pl.BlockSpec((tf, 2 * ln), lambda i, j: (i, 0)),
                  pl.BlockSpec((None, 2 * ln, DG), lambda i, j: (j, 0, 0)),
                  pl.BlockSpec((DG, DG), lambda i, j: (0, 0))],
        out_specs=pl.BlockSpec((None, tf, DG), lambda i, j: (j, i, 0)),
        out_shape=jax.ShapeDtypeStruct((b, ln, DG), F32),
        compiler_params=_cparams(2, VMEM_LIMIT),
        name="fourier_dft",
    )(fmat, x2, four_w)


def _dft_matrix(ln):
    f = jnp.arange(ln, dtype=I32)[:, None]
    t = jnp.arange(ln, dtype=I32)[None, :]
    ang = ((f * t) % ln).astype(F32) * (2.0 * math.pi / ln)
    scale = 1.0 / math.sqrt(ln * 64.0)
    return (jnp.concatenate([jnp.cos(ang), jnp.sin(ang)], axis=1) * scale).astype(BF16)


def _channel_dft():
    i = np.arange(DG)
    same = (i[:, None] // 64) == (i[None, :] // 64)
    ang = 2.0 * np.pi * ((i[:, None] % 64) * (i[None, :] % 64) % 64) / 64.0
    c = np.where(same, np.cos(ang), 0.0)
    s = np.where(same, np.sin(ang), 0.0)
    return jnp.asarray(np.concatenate([c, -s], axis=1), dtype=BF16)


def _s5_kernel(u_ref, p_ref, mq_ref, a_ref, y_ref, hl_ref, he_ref, *, nb, nch, ncc):
    u = u_ref[...].astype(BF16)
    hl_ref[...] = _dot(u, p_ref[...])
    a1f, a2f = a_ref[0:1, :LANES], a_ref[1:2, :LANES]
    a1b, a2b = a_ref[0:1, LANES:], a_ref[1:2, LANES:]

    def step(i, carry):
        hf, hb = carry
        rf = pl.multiple_of(i * nb, nb)
        nbk = jnp.where(i < ncc, ncc - 1 - i, nch + ncc - 1 - i)
        rb = pl.multiple_of(nbk * nb, nb)
        he_ref[pl.ds(rf, nb), :LANES] = hf
        he_ref[pl.ds(rb, nb), LANES:] = hb
        hf = a1f * hf + a2f * pltpu.roll(hf, 64, axis=1) + hl_ref[pl.ds(rf, nb), :LANES]
        hb = a1b * hb + a2b * pltpu.roll(hb, 64, axis=1) + hl_ref[pl.ds(rb, nb), LANES:]
        return hf, hb

    z = jnp.zeros((nb, LANES), F32)
    lax.fori_loop(0, nch, step, (z, z))
    y_ref[...] = (_dot(u, mq_ref[:S5_W, :])
                  + _dot(he_ref[...].astype(BF16), mq_ref[S5_W:, :]))


def _s5(u_g, ptab, mq, atab, nb, nch, ncc):
    g, r, _ = u_g.shape
    return pl.pallas_call(
        functools.partial(_s5_kernel, nb=nb, nch=nch, ncc=ncc),
        grid=(g,),
        in_specs=[pl.BlockSpec((None, r, S5_W), lambda i: (i, 0, 0)),
                  pl.BlockSpec((None, S5_W, 2 * LANES), lambda i: (i, 0, 0)),
                  pl.BlockSpec((None, S5_W + 2 * LANES, S5_W), lambda i: (i, 0, 0)),
                  pl.BlockSpec((None, 8, 2 * LANES), lambda i: (i, 0, 0))],
        out_specs=pl.BlockSpec((None, r, S5_W), lambda i: (i, 0, 0)),
        out_shape=jax.ShapeDtypeStruct((g, r, S5_W), F32),
        scratch_shapes=[pltpu.VMEM((r, 2 * LANES), F32), pltpu.VMEM((r, 2 * LANES), F32)],
        compiler_params=_cparams(1, VMEM_LIMIT),
        name="s5_mixer",
    )(u_g, ptab, mq, atab)


def _s5_tables(lam_re, lam_im, log_dt, b_re, b_im, c_re, c_im):
    t = S5_T
    dt = jnp.exp(log_dt)[..., None]
    ldr, ldi = lam_re * dt, lam_im * dt
    jj = jnp.arange(t + 1, dtype=F32)[None, None, :, None]
    mag = jnp.exp(ldr[:, :, None, :] * jj)
    pr, pi = mag * jnp.cos(ldi[:, :, None, :] * jj), mag * jnp.sin(ldi[:, :, None, :] * jj)
    ar, ai = pr[:, :, 1], pi[:, :, 1]
    den = lam_re * lam_re + lam_im * lam_im
    qr = ((ar - 1.0) * lam_re + ai * lam_im) / den
    qi = (ai * lam_re - (ar - 1.0) * lam_im) / den
    bbr = qr[..., None] * b_re - qi[..., None] * b_im
    bbi = qr[..., None] * b_im + qi[..., None] * b_re
    cpr = c_re[:, :, None] * pr[:, :, :, None, :] - c_im[:, :, None] * pi[:, :, :, None, :]
    cpi = c_re[:, :, None] * pi[:, :, :, None, :] + c_im[:, :, None] * pr[:, :, :, None, :]
    klag = (jnp.einsum('rgjop,rgph->rgjoh', cpr[:, :, :t], bbr, precision=HI)
            - jnp.einsum('rgjop,rgph->rgjoh', cpi[:, :, :t], bbi, precision=HI))
    s_i = jnp.arange(t)[:, None]
    t_i = jnp.arange(t)[None, :]

    def toeplitz(k, d):
        m = k[:, jnp.clip(d, 0, t - 1)]
        m = jnp.where((d >= 0)[None, :, :, None, None], m, 0.0)
        return jnp.transpose(m, (0, 1, 4, 2, 3))

    m_sum = (toeplitz(klag[0], t_i - s_i) + toeplitz(klag[1], s_i - t_i)).reshape(S5_G, S5_W, S5_W)
    pbr = pr[..., None] * bbr[:, :, None] - pi[..., None] * bbi[:, :, None]
    pbi = pr[..., None] * bbi[:, :, None] + pi[..., None] * bbr[:, :, None]

    def ptab(r, exps):
        re = jnp.transpose(pbr[r][:, exps], (0, 1, 3, 2))
        im = jnp.transpose(pbi[r][:, exps], (0, 1, 3, 2))
        return jnp.concatenate([re, im], axis=-1).reshape(S5_G, S5_W, 2 * S5_P)

    ar_t = np.arange(t)
    p2 = jnp.concatenate([ptab(0, t - 1 - ar_t), ptab(1, ar_t)], axis=-1)

    def qtab(r, exps):
        re = jnp.transpose(cpr[r][:, exps], (0, 3, 1, 2)).reshape(S5_G, S5_P, S5_W)
        im = jnp.transpose(cpi[r][:, exps], (0, 3, 1, 2)).reshape(S5_G, S5_P, S5_W)
        return jnp.concatenate([re, -im], axis=1)

    mq = jnp.concatenate([m_sum, qtab(0, ar_t + 1), qtab(1, t - ar_t)], axis=1)
    art, ait = pr[:, :, t], pi[:, :, t]
    a1 = jnp.concatenate([art[0], art[0], art[1], art[1]], axis=-1)
    a2 = jnp.concatenate([-ait[0], ait[0], -ait[1], ait[1]], axis=-1)
    atab = jnp.concatenate([a1[:, None], a2[:, None], jnp.zeros((S5_G, 6, 4 * S5_P), F32)], axis=1)
    return p2.astype(BF16), mq.astype(BF16), atab


def _split3(x):
    x1 = x.astype(BF16)
    r1 = x - x1.astype(F32)
    x2 = r1.astype(BF16)
    x3 = (r1 - x2.astype(F32)).astype(BF16)
    return x1, x2, x3


def _dot01(m, x):
    x1, x2, x3 = _split3(x)
    return (_dot(m, x3) + _dot(m, x2)) + _dot(m, x1)


def _tile_cumsum_mask(rev):
    t = lax.broadcasted_iota(I32, (TM, TM), 0)
    s = lax.broadcasted_iota(I32, (TM, TM), 1)
    order = (s >= t) if rev else (s <= t)
    return (order & ((t // CHUNK) == (s // CHUNK))).astype(BF16)


def _head_lane_mask(rev, width):
    t = lax.broadcasted_iota(I32, (CHUNK, width), 0)
    s = lax.broadcasted_iota(I32, (CHUNK, width), 1) % CHUNK
    return (s >= t) if rev else (s <= t)


def _gla_kernel(qk_ref, v_ref, misc_ref, a2_ref, ab_ref, o_ref, st_ref, *, rev):
    j = pl.program_id(1)

    @pl.when(j == 0)
    def _():
        st_ref[...] = jnp.zeros_like(st_ref)

    lane_k = lax.broadcasted_iota(I32, (1, LANES), 1) // GLA_DK
    lane_v = lax.broadcasted_iota(I32, (1, DG), 1) // GLA_DV
    causal = _head_lane_mask(rev, DG)
    st_diag = (lax.broadcasted_iota(I32, (DG, 1), 0) // GLA_DV) == lane_k

    z = _dot(misc_ref[...].astype(BF16), a2_ref[...]) + ab_ref[...]
    la_t = -(jnp.maximum(-z, 0.0) + jnp.log1p(jnp.exp(-jnp.abs(z)))) * (1.0 / GLA_TAU)
    b_t = _dot01(_tile_cumsum_mask(rev), la_t)
    st = st_ref[...]
    chunks = range(TM // CHUNK)
    for c in (reversed(chunks) if rev else chunks):
        rows = slice(c * CHUNK, (c + 1) * CHUNK)
        la, bcum = la_t[rows], b_t[rows]
        q = qk_ref[rows, :LANES] * (GLA_DK ** -0.5)
        k = qk_ref[rows, LANES:]
        v = v_ref[rows, :]
        tot = jnp.sum(la, axis=0, keepdims=True)
        q_in = (q * jnp.exp(bcum)).astype(BF16)
        k_in = k * jnp.exp(-bcum)
        k_dec = (k * jnp.exp(tot - bcum)).astype(BF16)
        k4 = jnp.concatenate([jnp.where(lane_k == h, k_in, 0.0) for h in range(GLA_H)], axis=0)
        att = jnp.where(causal, _dot(q_in, k4.astype(BF16), _NT), 0.0)
        vb = v.astype(BF16)
        vbd = jnp.concatenate([jnp.where(lane_v == h, vb, jnp.zeros_like(vb)) for h in range(GLA_H)], axis=0)
        o = _dot(att.astype(BF16), vbd) + _dot(q_in, jnp.where(st_diag, st, 0.0).astype(BF16), _NT)
        o_ref[rows, :] = o
        st = st * jnp.exp(tot) + _dot(vb, k_dec, _TN)
    st_ref[...] = st


def _tile_order(rev, j, nt):
    return jnp.where(j == 0, 0, nt - j) if rev else j


def _gla(p, a2m, abm, rev):
    b, s, _ = p.shape
    nt = s // TM
    return pl.pallas_call(
        functools.partial(_gla_kernel, rev=rev),
        grid=(b, nt),
        in_specs=[pl.BlockSpec((None, TM, DG), lambda i, j: (i, _tile_order(rev, j, nt), C_QK // DG)),
                  pl.BlockSpec((None, TM, DG), lambda i, j: (i, _tile_order(rev, j, nt), C_V // DG)),
                  pl.BlockSpec((None, TM, LANES), lambda i, j: (i, _tile_order(rev, j, nt), C_MISC // LANES)),
                  pl.BlockSpec((LANES, LANES), lambda i, j: (0, 0)),
                  pl.BlockSpec((1, LANES), lambda i, j: (0, 0))],
        out_specs=pl.BlockSpec((None, TM, DG), lambda i, j: (i, _tile_order(rev, j, nt), 0)),
        out_shape=jax.ShapeDtypeStruct((b, s, DG), F32),
        scratch_shapes=[pltpu.VMEM((DG, LANES), F32)],
        compiler_params=_cparams(2, VMEM_LIMIT),
        name="gla_bwd" if rev else "gla_fwd",
    )(p, p, p, a2m, abm)


def _ssd_kernel(xbc_ref, misc_ref, dtb_ref, av_ref, y_ref, sb_ref, *, rev):
    j = pl.program_id(1)

    @pl.when(j == 0)
    def _():
        sb_ref[...] = jnp.zeros_like(sb_ref)

    lane = lax.broadcasted_iota(I32, (1, LANES), 1)
    lane_g = lane // SSD_N
    lane_h = lax.broadcasted_iota(I32, (1, DG), 1) // SSD_HD
    row_h = lax.broadcasted_iota(I32, (DG, 1), 0) // SSD_HD
    blockmask = row_h == lane_h
    causal = _head_lane_mask(rev, DG)
    diag = (lax.broadcasted_iota(I32, (CHUNK, DG), 0)
            == lax.broadcasted_iota(I32, (CHUNK, DG), 1) % CHUNK)

    def per_head(x):
        xr = pltpu.roll(x, SSD_N, axis=1)
        return jnp.concatenate([jnp.where(lane < SSD_N, x, xr), jnp.where(lane < SSD_N, xr, x)], axis=1)

    zm = misc_ref[...] + dtb_ref[...]
    sp = jnp.maximum(zm, 0.0) + jnp.log1p(jnp.exp(-jnp.abs(zm)))
    dt_t = jnp.zeros((TM, DG), F32)
    for h in range(SSD_H):
        col = jnp.sum(jnp.where(lane == MISC_DT + SSD_H * int(rev) + h, sp, 0.0), axis=1, keepdims=True)
        dt_t = jnp.where(lane_h == h, col, dt_t)
    da_t = dt_t * av_ref[...]
    cum_t = _dot01(_tile_cumsum_mask(rev), da_t)

    sb = sb_ref[...]
    chunks = range(TM // CHUNK)
    for c in (reversed(chunks) if rev else chunks):
        rows = slice(c * CHUNK, (c + 1) * CHUNK)
        da, cum = da_t[rows], cum_t[rows]
        bmc = xbc_ref[rows, DG:DG + LANES]
        cmc = xbc_ref[rows, DG + LANES:]
        xdt = (xbc_ref[rows, :DG] * dt_t[rows]).astype(BF16)
        tot = jnp.sum(da, axis=0, keepdims=True)
        cum_row = jnp.sum(jnp.where(diag, cum, 0.0), axis=0, keepdims=True)
        decay = jnp.where(causal, jnp.exp(jnp.minimum(cum - cum_row, 0.0)), 0.0)
        b4 = jnp.concatenate([jnp.where(lane_g == h // (SSD_H // SSD_G), bmc, 0.0) for h in range(SSD_H)],
                             axis=0)
        scores = _dot(cmc.astype(BF16), b4.astype(BF16), _NT) * decay
        xbd = jnp.concatenate([jnp.where(lane_h == h, xdt, jnp.zeros_like(xdt)) for h in range(SSD_H)],
                              axis=0)
        cdec = per_head(cmc) * jnp.exp(cum)
        y = _dot(scores.astype(BF16), xbd) + _dot(cdec.astype(BF16), sb.astype(BF16))
        y_ref[rows, :] = y
        bdec = per_head(bmc) * jnp.exp(tot - cum)
        upd = _dot(bdec.astype(BF16), xdt, _TN)
        sb = sb * jnp.exp(tot) + jnp.where(blockmask, upd, 0.0)
    sb_ref[...] = sb


def _ssd(p, dtb, av, rev):
    b, s, _ = p.shape
    nt = s // TM
    return pl.pallas_call(
        functools.partial(_ssd_kernel, rev=rev),
        grid=(b, nt),
        in_specs=[pl.BlockSpec((None, TM, 512), lambda i, j: (i, _tile_order(rev, j, nt), C_XBC // 512)),
                  pl.BlockSpec((None, TM, LANES), lambda i, j: (i, _tile_order(rev, j, nt), C_MISC // LANES)),
                  pl.BlockSpec((1, LANES), lambda i, j: (0, 0)),
                  pl.BlockSpec((1, DG), lambda i, j: (0, 0))],
        out_specs=pl.BlockSpec((None, TM, DG), lambda i, j: (i, _tile_order(rev, j, nt), 0)),
        out_shape=jax.ShapeDtypeStruct((b, s, DG), F32),
        scratch_shapes=[pltpu.VMEM((DG, DG), F32)],
        compiler_params=_cparams(2, VMEM_LIMIT),
        name="ssd_bwd" if rev else "ssd_fwd",
    )(p, p, dtb, av)


def _ssd_tables(a_log, dt_bias, r):
    dtb = jnp.zeros((1, LANES), F32).at[0, MISC_DT + SSD_H * r:MISC_DT + SSD_H * (r + 1)].set(dt_bias[r])
    av = jnp.repeat(-jnp.exp(a_log[r]), SSD_HD)[None, :]
    return dtb, av


def _gelu_tanh(x):
    return 0.5 * x * (1.0 + jnp.tanh(math.sqrt(2.0 / math.pi) * (x + 0.044715 * x * x * x)))


def _post_kernel(x_ref, mod_ref, us5_ref, r_ref, z_ref, xs_ref, fl_ref, fc_ref, s5y_ref,
                 gf_ref, gb_ref, sf_ref, sbk_ref, vec_ref, wglu_ref, avg_ref, wout_ref,
                 gffn_ref, rw_ref, rb_ref, tri_ref, upper_ref,
                 xo_ref, loc_ref, ri_ref, rg_ref, tab_ref):
    j = pl.program_id(1)
    s5_d, gla_g, ssd_d, ssd_g = vec_ref[0:1, :], vec_ref[1:2, :], vec_ref[2:3, :], vec_ref[3:4, :]
    four = jnp.where(j == 0, fc_ref[...], fl_ref[...])
    y5 = s5y_ref[...] + s5_d * us5_ref[...]
    act = _gelu_tanh(y5)
    s5o = act * jax.nn.sigmoid(_dot(act.astype(BF16), wglu_ref[...]))
    og = gf_ref[...] + gb_ref[...]
    ms = _dot(og * og, avg_ref[...], precision=HI)
    glo = og * lax.rsqrt(ms + EPS) * gla_g * _silu(r_ref[...])
    ys = (sf_ref[...] + sbk_ref[...] + ssd_d * xs_ref[...]) * _silu(z_ref[...])
    sso = _rms(ys, ssd_g)
    mix = jnp.concatenate([four, s5o, glo, sso], axis=1).astype(BF16)
    x = x_ref[...] + mod_ref[2:3, :] * _dot(mix, wout_ref[...])
    xo_ref[...] = x
    h = _rms(x, gffn_ref[...]) * (1.0 + mod_ref[4:5, :]) + mod_ref[3:4, :]

    logits = _dot(h, rw_ref[...], precision=HI) + rb_ref[...]
    lane = lax.broadcasted_iota(I32, (TM, LANES), 1)
    vals, idxs = [], []
    for _k in range(TOP_K):
        mx = jnp.max(logits, axis=1, keepdims=True)
        ix = jnp.min(jnp.where(logits == mx, lane, LANES), axis=1, keepdims=True)
        vals.append(mx)
        idxs.append(ix)
        logits = jnp.where(lane == ix, -jnp.inf, logits)
    es = [jnp.exp(v - vals[0]) for v in vals]
    den = es[0] + es[1] + es[2] + es[3]
    onehots = [(lane == ix) for ix in idxs]
    oh = sum(o.astype(F32) for o in onehots)

    ahead = _dot(tri_ref[...], oh.astype(BF16))
    cnt = jnp.sum(oh, axis=0, keepdims=True)
    cpad = jnp.floor((cnt + (SEG - 1.0)) * (1.0 / SEG)) * SEG
    lstart = _dot(jnp.broadcast_to(cpad, (8, LANES)).astype(BF16), upper_ref[...])[0:1]
    slot_of = lstart + ahead
    lane_r = lax.broadcasted_iota(I32, (1, LMAX), 1)
    ri = jnp.zeros((TM, LANES), I32)
    rg = jnp.zeros((TM, LANES), F32)
    pt = jnp.zeros((TM, LMAX), F32)
    for k in range(TOP_K):
        lpos = jnp.sum(jnp.where(onehots[k], slot_of, 0.0), axis=1, keepdims=True).astype(I32)
        pt = pt + (lane_r == lpos).astype(F32)
        ri = jnp.where(lane == k, idxs[k], ri)
        ri = jnp.where(lane == TOP_K + k, lpos, ri)
        rg = jnp.where(lane == k, es[k] / den, rg)
    ri_ref[...] = ri
    rg_ref[...] = rg
    loc_ref[...] = _pack_bf16_pairs(_dot(pt.astype(BF16), h.astype(BF16), _TN))
    row = lax.broadcasted_iota(I32, (8, LANES), 0)
    tab_ref[...] = jnp.where(row == 0, cnt.astype(I32), jnp.where(row == 1, lstart.astype(I32), 0))


def _post(x_all, mod, p, four_l, four_c, s5y, gla_f, gla_b, ssd_f, ssd_b, vecs, wglu, avg, wout, gffn,
          rw, rb, tri, upper, seq, ctx):
    b, s, _ = x_all.shape
    nt = s // TM
    tile = lambda w, col: pl.BlockSpec((None, TM, w), lambda i, j: (i, j, col))
    full = lambda a: pl.BlockSpec(a.shape, lambda i, j: (0,) * a.ndim)
    per_tile = lambda r, w: pl.BlockSpec((None, None, r, w), lambda i, j: (i, j, 0, 0))
    return pl.pallas_call(
        _post_kernel,
        grid=(b, nt),
        in_specs=[tile(D, 0),
                  pl.BlockSpec((None, None, 8, D), lambda i, j: (i, jnp.minimum(j, 1), 0, 0)),
                  tile(DG, C_S5 // DG), tile(DG, C_R // DG), tile(DG, C_Z // DG), tile(DG, C_XBC // DG),
                  pl.BlockSpec((None, TM, DG), lambda i, j: (i, jnp.maximum(j - 1, 0), 0)),
                  pl.BlockSpec((None, ctx, DG), lambda i, j: (i, 0, 0)),
                  tile(DG, 0), tile(DG, 0), tile(DG, 0), tile(DG, 0), tile(DG, 0),
                  full(vecs), full(wglu), full(avg), full(wout), full(gffn), full(rw), full(rb), full(tri),
                  full(upper)],
        out_specs=[tile(D, 0), per_tile(LMAX, D // 2), tile(LANES, 0), tile(LANES, 0), per_tile(8, LANES)],
        out_shape=[jax.ShapeDtypeStruct((b, s, D), F32),
                   jax.ShapeDtypeStruct((b, nt, LMAX, D // 2), U32),
                   jax.ShapeDtypeStruct((b, s, LANES), I32),
                   jax.ShapeDtypeStruct((b, s, LANES), F32),
                   jax.ShapeDtypeStruct((b, nt, 8, LANES), I32)],
        compiler_params=_cparams(2, VMEM_LIMIT),
        name="post_mixer_router",
    )(x_all, mod, p, p, p, p, four_l, four_c, s5y, gla_f, gla_b, ssd_f, ssd_b,
      vecs, wglu, avg, wout, gffn, rw, rb, tri, upper)


T_NSEG, T_LSTART, T_GSTART = 0, 32, 64


def _segment_copies(tab_ref, make_copy, action):
    def per_expert(e, c):
        lstart = pl.multiple_of(tab_ref[0, 0, T_LSTART + e], SEG)
        gstart = pl.multiple_of(tab_ref[0, 0, T_GSTART + e], SEG)

        def per_piece(q, c2):
            action(make_copy(lstart + q * SEG, gstart + q * SEG))
            return c2
        return lax.fori_loop(0, tab_ref[0, 0, T_NSEG + e], per_piece, c)
    lax.fori_loop(0, N_EXP, per_expert, 0)


def _dispatch_kernel(tab_ref, loc_ref, buf_in_ref, buf_ref, sem):
    del buf_in_ref
    i = pl.program_id(0)

    def make_copy(lrow, grow):
        return pltpu.make_async_copy(loc_ref.at[i, pl.ds(lrow, SEG)], buf_ref.at[pl.ds(grow, SEG)], sem)

    _segment_copies(tab_ref, make_copy, lambda cp: cp.start())
    _segment_copies(tab_ref, make_copy, lambda cp: cp.wait())


def _dispatch(tab3, loc, buf0):
    ntile = loc.shape[0]
    return pl.pallas_call(
        _dispatch_kernel,
        grid=(ntile,),
        in_specs=[pl.BlockSpec((1, 1, LANES), lambda i: (i, 0, 0), memory_space=pltpu.SMEM),
                  pl.BlockSpec(memory_space=pl.ANY),
                  pl.BlockSpec(memory_space=pl.ANY)],
        out_specs=pl.BlockSpec(memory_space=pl.ANY),
        out_shape=jax.ShapeDtypeStruct(buf0.shape, U32),
        scratch_shapes=[pltpu.SemaphoreType.DMA(())],
        input_output_aliases={2: 0},
        compiler_params=_cparams(1, VMEM_LIMIT),
        name="moe_dispatch",
    )(tab3, loc, buf0)


def _expert_kernel(be_ref, nu_ref, x_ref, wgu_ref, wd_ref, bg_ref, bu_ref, bd_ref, perm_ref, y_ref,
                   wg_sc, wu_sc, wd_sc):
    i = pl.program_id(0)
    used = i < nu_ref[0]
    new_expert = (i == 0) | (be_ref[i] != be_ref[jnp.maximum(i - 1, 0)])

    @pl.when(used & new_expert)
    def _():
        for c in range(2 * D // DG):
            d = _dot(wgu_ref[:, c * DG:(c + 1) * DG].astype(BF16), perm_ref[...])
            wg_sc[:, c * LANES:(c + 1) * LANES] = d[:, :LANES].astype(BF16)
            wu_sc[:, c * LANES:(c + 1) * LANES] = d[:, LANES:].astype(BF16)
        wd_sc[...] = wd_ref[...].astype(BF16)

    @pl.when(used)
    def _():
        x = _unpack_bf16_pairs(x_ref[...]).astype(BF16)
        g = _dot(x, wg_sc[...]) + bg_ref[...]
        u = _dot(x, wu_sc[...]) + bu_ref[...]
        gate = jnp.minimum(g, LIMIT)
        up = jnp.clip(u, -LIMIT, LIMIT)
        act = (up + 1.0) * gate * jax.nn.sigmoid(ALPHA * gate)
        y = _dot(act.astype(BF16), wd_sc[...]) + bd_ref[...]
        y_ref[...] = _pack_bf16_pairs(y)

    @pl.when(jnp.logical_not(used))
    def _():
        y_ref[...] = jnp.zeros_like(y_ref)


def _deinterleave_perm():
    pm = np.zeros((DG, DG), np.float32)
    o = np.arange(LANES)
    pm[2 * o, o] = 1.0
    pm[2 * o + 1, LANES + o] = 1.0
    return jnp.asarray(pm, BF16)


def _experts(blk_e, n_used, buf, wgu, wd, bg, bu, bd):
    nblk = buf.shape[0] // BM
    row = lambda i, be, nu: (jnp.minimum(i, nu[0] - 1), 0)
    wsel = lambda i, be, nu: (be[jnp.minimum(i, nu[0] - 1)], 0, 0)
    grid_spec = pltpu.PrefetchScalarGridSpec(
        num_scalar_prefetch=2,
        grid=(nblk,),
        in_specs=[pl.BlockSpec((BM, D // 2), row),
                  pl.BlockSpec((None, D, 2 * D), wsel), pl.BlockSpec((None, D, D), wsel),
                  pl.BlockSpec((None, 1, D), wsel), pl.BlockSpec((None, 1, D), wsel),
                  pl.BlockSpec((None, 1, D), wsel),
                  pl.BlockSpec((DG, DG), lambda i, be, nu: (0, 0))],
        out_specs=pl.BlockSpec((BM, D // 2), lambda i, be, nu: (i, 0)),
        scratch_shapes=[pltpu.VMEM((D, D), BF16), pltpu.VMEM((D, D), BF16), pltpu.VMEM((D, D), BF16)],
    )
    return pl.pallas_call(
        _expert_kernel,
        grid_spec=grid_spec,
        out_shape=jax.ShapeDtypeStruct(buf.shape, U32),
        compiler_params=_cparams(1, VMEM_LIMIT),
        name="moe_experts",
    )(blk_e, n_used, buf, wgu, wd, bg, bu, bd, _deinterleave_perm())


def _combine_kernel(tab_ref, tabn_ref, x_ref, ri_ref, rg_ref, mod_ref, fg_ref, yb_ref, o_ref, yloc, sem,
                    *, final):
    i = pl.program_id(0)
    n = pl.num_programs(0)
    slot = i % 2

    def make_copy(sl):
        return lambda lrow, grow: pltpu.make_async_copy(
            yb_ref.at[pl.ds(grow, SEG)], yloc.at[sl, pl.ds(lrow, SEG)], sem.at[sl])

    @pl.when(i == 0)
    def _():
        yloc[...] = jnp.zeros_like(yloc)
        _segment_copies(tab_ref, make_copy(0), lambda cp: cp.start())

    @pl.when(i + 1 < n)
    def _():
        _segment_copies(tabn_ref, make_copy(1 - slot), lambda cp: cp.start())

    _segment_copies(tab_ref, make_copy(slot), lambda cp: cp.wait())

    lane_r = lax.broadcasted_iota(I32, (1, LMAX), 1)
    g = jnp.zeros((TM, LMAX), F32)
    for k in range(TOP_K):
        g = g + jnp.where(lane_r == ri_ref[:, TOP_K + k:TOP_K + k + 1], rg_ref[:, k:k + 1], 0.0)
    g_hi = g.astype(BF16)
    g_lo = (g - g_hi.astype(F32)).astype(BF16)
    y = _unpack_bf16_pairs(yloc[slot]).astype(BF16)
    f = _dot(g_hi, y) + _dot(g_lo, y)
    x = x_ref[...] + mod_ref[5:6, :] * f
    o_ref[...] = _rms(x, fg_ref[...]) if final else x


def _combine(tab3, x_flat, ri_flat, rg_flat, mod, fg, ybuf, nt, final):
    n = x_flat.shape[0]
    steps = n // TM
    tspec = lambda f: pl.BlockSpec((1, 1, LANES), f, memory_space=pltpu.SMEM)
    return pl.pallas_call(
        functools.partial(_combine_kernel, final=final),
        grid=(steps,),
        in_specs=[tspec(lambda i: (i, 0, 0)),
                  tspec(lambda i: (jnp.minimum(i + 1, steps - 1), 0, 0)),
                  pl.BlockSpec((TM, D), lambda i: (i, 0)),
                  pl.BlockSpec((TM, LANES), lambda i: (i, 0)),
                  pl.BlockSpec((TM, LANES), lambda i: (i, 0)),
                  pl.BlockSpec((None, None, 8, D), lambda i: (i // nt, jnp.minimum(i % nt, 1), 0, 0)),
                  pl.BlockSpec((1, D), lambda i: (0, 0)),
                  pl.BlockSpec(memory_space=pl.ANY)],
        out_specs=pl.BlockSpec((TM, D), lambda i: (i, 0)),
        out_shape=jax.ShapeDtypeStruct((n, D), F32),
        scratch_shapes=[pltpu.VMEM((2, LMAX, D // 2), U32), pltpu.SemaphoreType.DMA((2,))],
        compiler_params=_cparams(1, VMEM_LIMIT),
        name="moe_combine",
    )(tab3, tab3, x_flat, ri_flat, rg_flat, mod, fg, ybuf)


def _repack_w_in(w):
    four, s5, q, k, v, r, a1, z, xbc, dt = jnp.split(
        w, np.cumsum([256, 256, 128, 128, 256, 256, 32, 256, 512, 8])[:-1].tolist(), axis=-1)
    misc = jnp.concatenate([a1, dt, jnp.zeros((D, LANES - 40), w.dtype)], axis=-1)
    return jnp.concatenate([four, s5, q, k, v, r, z, xbc, misc], axis=-1).astype(BF16)


def kernel(x, c, ctx, c_ctx, ada_w, ada_b, norm_mix_g, norm_ffn_g, w_in, w_out, four_w, s5_lam_re, s5_lam_im, s5_log_dt, s5_b_re, s5_b_im, s5_c_re, s5_c_im, s5_d, s5_w_glu, gla_a2, gla_a_b, gla_norm_g, ssd_conv_w, ssd_conv_b, ssd_a_log, ssd_dt_bias, ssd_d, ssd_norm_g, router_w, router_b, exp_w_gu, exp_b_gu, exp_w_dn, exp_b_dn, final_g):
    bsz, seq, _ = x.shape
    clen = ctx.shape[1]
    depth = ada_w.shape[0]
    assert clen == TM and seq % TM == 0 and bsz == 8
    s = clen + seq
    nt = s // TM
    n_tok = bsz * s
    nch, ncc = s // S5_T, clen // S5_T

    cond = jnp.concatenate([c, c_ctx[None, :], jnp.zeros((16 - bsz - 1, D), F32)], axis=0)
    mods = _ada(cond, ada_w, ada_b).reshape(depth, 16, N_MOD, D)
    m_lat = mods[:, :bsz]
    m_ctx = jnp.broadcast_to(mods[:, bsz:bsz + 1], m_lat.shape)
    mods = jnp.stack([m_ctx, m_lat], axis=2)
    mods = jnp.pad(mods, ((0, 0), (0, 0), (0, 0), (0, 2), (0, 0)))

    x_all = jnp.concatenate([ctx, x], axis=1)
    csm = _channel_dft()
    f_lat, f_ctx = _dft_matrix(seq), _dft_matrix(clen)
    avg = jnp.asarray(np.kron(np.eye(GLA_H), np.full((GLA_DV, GLA_DV), 1.0 / GLA_DV)), F32)
    tri = jnp.asarray(np.tril(np.ones((TM, TM)), -1), BF16)
    upper = jnp.asarray(np.triu(np.ones((LANES, LANES)), 1), BF16)

    ntile = n_tok // TM
    nblk = -(-(n_tok * TOP_K + ntile * N_EXP * (SEG - 1) + N_EXP * (BM - 1)) // BM)
    n_rows = nblk * BM

    for l in range(depth):
        mod = mods[l]
        w_in_p = _repack_w_in(w_in[l])
        conv_w = jnp.pad(ssd_conv_w[l], ((0, 8 - SSD_K), (0, 0)))
        p, four_l, four_c = _pre(x_all, mod, norm_mix_g[l][None], w_in_p, conv_w, ssd_conv_b[l][None],
                                 csm, seq, clen)
        fw = four_w[l].astype(BF16)
        fy_l = _fourier(f_lat, four_l.reshape(bsz, 2 * seq, DG), fw)
        fy_c = _fourier(f_ctx, four_c.reshape(bsz, 2 * clen, DG), fw)

        ptab, mq, atab = _s5_tables(s5_lam_re[l], s5_lam_im[l], s5_log_dt[l], s5_b_re[l], s5_b_im[l],
                                    s5_c_re[l], s5_c_im[l])
        u5 = p[:, :, C_S5:C_S5 + DG].reshape(bsz, nch, S5_T, S5_G, S5_CH)
        u5 = jnp.transpose(u5, (3, 1, 0, 2, 4)).reshape(S5_G, nch * bsz, S5_W)
        y5 = _s5(u5, ptab, mq, atab, bsz, nch, ncc).reshape(S5_G, nch, bsz, S5_T, S5_CH)
        y5 = jnp.transpose(y5, (2, 1, 3, 0, 4)).reshape(bsz, s, DG)

        gla_o, ssd_y = [], []
        for r in range(2):
            a2m = jnp.zeros((LANES, LANES), F32).at[GLA_RANK * r:GLA_RANK * (r + 1)].set(gla_a2[l, r])
            gla_o.append(_gla(p, a2m.astype(BF16), gla_a_b[l, r][None, :], rev=bool(r)))
            ssd_y.append(_ssd(p, *_ssd_tables(ssd_a_log[l], ssd_dt_bias[l], r), rev=bool(r)))

        vecs = jnp.stack([s5_d[l], jnp.tile(gla_norm_g[l], GLA_H), jnp.repeat(ssd_d[l], SSD_HD),
                          ssd_norm_g[l]] + [jnp.zeros((DG,), F32)] * 4)
        rw = jnp.pad(router_w[l], ((0, 0), (0, LANES - N_EXP)))
        rb = jnp.concatenate([router_b[l], jnp.full((LANES - N_EXP,), -1e30, F32)])[None]
        x_mid, loc, ri, rg, tab = _post(x_all, mod, p, fy_l, fy_c, y5, gla_o[0], gla_o[1], ssd_y[0], ssd_y[1],
                                        vecs, s5_w_glu[l].astype(BF16), avg, w_out[l].astype(BF16),
                                        norm_ffn_g[l][None], rw, rb, tri, upper, seq, clen)

        tab = tab.reshape(ntile, 8, LANES)
        cnt, lstart = tab[:, 0, :N_EXP], tab[:, 1, :N_EXP]
        cpad = (cnt + SEG - 1) // SEG * SEG
        in_expert = jnp.cumsum(cpad, axis=0) - cpad
        region = (jnp.sum(cpad, axis=0) + BM - 1) // BM * BM
        pad_end = jnp.cumsum(region)
        gstart = (pad_end - region)[None, :] + in_expert
        tab3 = jnp.concatenate([cpad // SEG, lstart, gstart, jnp.zeros_like(cnt)], axis=1)
        tab3 = tab3.astype(I32).reshape(ntile, 1, LANES)
        blk_start = jnp.arange(nblk, dtype=I32) * BM
        blk_e = jnp.minimum(jnp.sum((pad_end[None, :] <= blk_start[:, None]).astype(I32), axis=1), N_EXP - 1)
        n_used = (pad_end[-1] // BM).astype(I32)[None]

        buf = _dispatch(tab3, loc.reshape(ntile, LMAX, D // 2), jnp.zeros((n_rows, D // 2), U32))
        bgu = exp_b_gu[l].reshape(N_EXP, 1, D, 2)
        ybuf = _experts(blk_e, n_used, buf, exp_w_gu[l], exp_w_dn[l], bgu[..., 0], bgu[..., 1],
                        exp_b_dn[l][:, None, :])
        x_all = _combine(tab3, x_mid.reshape(n_tok, D), ri.reshape(n_tok, LANES), rg.reshape(n_tok, LANES),
                         mod, final_g[None], ybuf, nt, final=(l == depth - 1)).reshape(bsz, s, D)

    return x_all[:, clen:, :]
```

```python
import functools
import math

import jax
import jax.numpy as jnp
import numpy as np
from jax import lax
from jax.experimental import pallas as pl
from jax.experimental.pallas import tpu as pltpu

F32 = jnp.float32
BF16 = jnp.bfloat16
I32 = jnp.int32
U32 = jnp.uint32
HI = lax.Precision.HIGHEST

D = 1024
DG = 256
N_MOD = 6
GRID_W = 64
EPS = 1e-6

S5_CH = 16
S5_G = DG // S5_CH
S5_P = 64
S5_T = 32
S5_W = S5_T * S5_CH

GLA_H = 4
GLA_DK = 32
GLA_DV = 64
GLA_RANK = 16
GLA_TAU = 16.0
CHUNK = 64

SSD_H = 4
SSD_HD = 64
SSD_G = 2
SSD_N = 64
SSD_K = 5

N_EXP = 32
TOP_K = 4
LIMIT = 7.0
ALPHA = 1.702

TM = 256
BM = 512
SEG = 8
LMAX = 1280
LANES = 128

C_FOUR, C_S5, C_QK, C_V, C_R, C_Z, C_XBC, C_MISC = 0, 256, 512, 768, 1024, 1280, 1536, 2048
D_P = 2176
MISC_A1 = 0
MISC_DT = 32

VMEM_LIMIT = 56 * 1024 * 1024


def _cparams(n_axes, vmem=None):
    return pltpu.CompilerParams(dimension_semantics=("arbitrary",) * n_axes,
                                vmem_limit_bytes=vmem)


def _silu(x):
    return x * jax.nn.sigmoid(x)


def _rms(x, g):
    return x * lax.rsqrt(jnp.mean(x * x, axis=-1, keepdims=True) + EPS) * g


def _dot(a, b, dims=None, precision=None):
    if dims is None:
        dims = (((a.ndim - 1,), (0,)), ((), ()))
    return lax.dot_general(a, b, dims, precision=precision, preferred_element_type=F32)


_NT = (((1,), (1,)), ((), ()))
_TN = (((0,), (0,)), ((), ()))


def _pack_bf16_pairs(a):
    n = a.shape[1] // 2
    hi = pltpu.bitcast(a[:, :n].astype(BF16).astype(F32), U32)
    lo = pltpu.bitcast(a[:, n:].astype(BF16).astype(F32), U32)
    return hi | (lo >> 16)


def _unpack_bf16_pairs(w):
    hi = pltpu.bitcast(w & jnp.uint32(0xFFFF0000), F32)
    lo = pltpu.bitcast(w << 16, F32)
    return jnp.concatenate([hi, lo], axis=1)


def _ada_kernel(cond_ref, w_ref, b_ref, o_ref):
    c = _silu(cond_ref[...])
    o_ref[...] = _dot(c, w_ref[...], precision=HI) + b_ref[...]


def _ada(cond, ada_w, ada_b):
    depth = ada_w.shape[0]
    rows = cond.shape[0]
    tn = 1536
    return pl.pallas_call(
        _ada_kernel,
        grid=(depth, (N_MOD * D) // tn),
        in_specs=[pl.BlockSpec((rows, D), lambda l, n: (0, 0)),
                  pl.BlockSpec((None, D, tn), lambda l, n: (l, 0, n)),
                  pl.BlockSpec((None, 1, tn), lambda l, n: (l, 0, n))],
        out_specs=pl.BlockSpec((None, rows, tn), lambda l, n: (l, 0, n)),
        out_shape=jax.ShapeDtypeStruct((depth, rows, N_MOD * D), F32),
        compiler_params=_cparams(2, VMEM_LIMIT),
        name="ada_mod",
    )(cond, ada_w, ada_b.reshape(depth, 1, N_MOD * D))


def _pre_kernel(x_ref, mod_ref, g_ref, w_ref, cw_ref, cb_ref, csm_ref, p_ref, fl_ref, fc_ref):
    j = pl.program_id(1)
    x = x_ref[...]
    hn = _rms(x, g_ref[...]) * (1.0 + mod_ref[1:2, :]) + mod_ref[0:1, :]
    p = _dot(hn.astype(BF16), w_ref[...])
    p_ref[:, :C_XBC] = p[:, :C_XBC]
    p_ref[:, C_MISC:] = p[:, C_MISC:]

    xbc = p[:, C_XBC:C_MISC]
    period = jnp.where(j == 0, TM, GRID_W)
    pos = lax.broadcasted_iota(I32, (TM, 1), 0) & (period - 1)
    acc = jnp.zeros_like(xbc) + cb_ref[...]
    for k in range(SSD_K):
        d = k - SSD_K // 2
        sh = xbc if d == 0 else pltpu.roll(xbc, (TM - d) % TM, axis=0)
        ok = (pos + d >= 0) & (pos + d < period)
        acc = acc + jnp.where(ok, sh, 0.0) * cw_ref[k:k + 1, :]
    p_ref[:, C_XBC:C_MISC] = _silu(acc)

    xcs = _dot(p[:, C_FOUR:C_FOUR + DG].astype(BF16), csm_ref[...])

    @pl.when(j == 0)
    def _():
        fc_ref[0] = xcs[:, :DG].astype(BF16)
        fc_ref[1] = xcs[:, DG:].astype(BF16)

    @pl.when(j > 0)
    def _():
        fl_ref[0] = xcs[:, :DG].astype(BF16)
        fl_ref[1] = xcs[:, DG:].astype(BF16)


def _pre(x_all, mod, g, w_in_p, conv_w, conv_b, csm, seq, ctx):
    b, s, _ = x_all.shape
    nt = s // TM
    return pl.pallas_call(
        _pre_kernel,
        grid=(b, nt),
        in_specs=[pl.BlockSpec((None, TM, D), lambda i, j: (i, j, 0)),
                  pl.BlockSpec((None, None, 8, D), lambda i, j: (i, jnp.minimum(j, 1), 0, 0)),
                  pl.BlockSpec((1, D), lambda i, j: (0, 0)),
                  pl.BlockSpec((D, D_P), lambda i, j: (0, 0)),
                  pl.BlockSpec((8, 512), lambda i, j: (0, 0)),
                  pl.BlockSpec((1, 512), lambda i, j: (0, 0)),
                  pl.BlockSpec((DG, 2 * DG), lambda i, j: (0, 0))],
        out_specs=[pl.BlockSpec((None, TM, D_P), lambda i, j: (i, j, 0)),
                   pl.BlockSpec((None, 2, TM, DG), lambda i, j: (i, 0, jnp.maximum(j - 1, 0), 0)),
                   pl.BlockSpec((None, 2, ctx, DG), lambda i, j: (i, 0, 0, 0))],
        out_shape=[jax.ShapeDtypeStruct((b, s, D_P), F32),
                   jax.ShapeDtypeStruct((b, 2, seq, DG), BF16),
                   jax.ShapeDtypeStruct((b, 2, ctx, DG), BF16)],
        compiler_params=_cparams(2, VMEM_LIMIT),
        name="pre_mixer",
    )(x_all, mod, g, w_in_p, conv_w, conv_b, csm)


def _four_kernel(f_ref, x_ref, w_ref, o_ref):
    y = _dot(f_ref[...], x_ref[...])
    o_ref[...] = _dot(y.astype(BF16), w_ref[...])


def _fourier(fmat, x2, four_w):
    ln = fmat.shape[0]
    b = x2.shape[0]
    tf = min(512, ln)
    return pl.pallas_call(
        _four_kernel,
        grid=(ln // tf, b),
        in_specs=[pl.BlockSpec((tf, 2 * ln), lambda i, j: (i, 0)),
                  pl.BlockSpec((None, 2 * ln, DG), lambda i, j: (j, 0, 0)),
                  pl.BlockSpec((DG, DG), lambda i, j: (0, 0))],
        out_specs=pl.BlockSpec((None, tf, DG), lambda i, j: (j, i, 0)),
        out_shape=jax.ShapeDtypeStruct((b, ln, DG), F32),
        compiler_params=_cparams(2, VMEM_LIMIT),
        name="fourier_dft",
    )(fmat, x2, four_w)


def _dft_matrix(ln):
    f = jnp.arange(ln, dtype=I32)[:, None]
    t = jnp.arange(ln, dtype=I32)[None, :]
    ang = ((f * t) % ln).astype(F32) * (2.0 * math.pi / ln)
    scale = 1.0 / math.sqrt(ln * 64.0)
    return (jnp.concatenate([jnp.cos(ang), jnp.sin(ang)], axis=1) * scale).astype(BF16)


def _channel_dft():
    i = np.arange(DG)
    same = (i[:, None] // 64) == (i[None, :] // 64)
    ang = 2.0 * np.pi * ((i[:, None] % 64) * (i[None, :] % 64) % 64) / 64.0
    c = np.where(same, np.cos(ang), 0.0)
    s = np.where(same, np.sin(ang), 0.0)
    return jnp.asarray(np.concatenate([c, -s], axis=1), dtype=BF16)


def _s5_kernel(u_ref, p_ref, mq_ref, a_ref, y_ref, hl_ref, he_ref, *, nb, nch, ncc):
    u = u_ref[...].astype(BF16)
    hl_ref[...] = _dot(u, p_ref[...])
    a1f, a2f = a_ref[0:1, :LANES], a_ref[1:2, :LANES]
    a1b, a2b = a_ref[0:1, LANES:], a_ref[1:2, LANES:]

    def step(i, carry):
        hf, hb = carry
        rf = pl.multiple_of(i * nb, nb)
        nbk = jnp.where(i < ncc, ncc - 1 - i, nch + ncc - 1 - i)
        rb = pl.multiple_of(nbk * nb, nb)
        he_ref[pl.ds(rf, nb), :LANES] = hf
        he_ref[pl.ds(rb, nb), LANES:] = hb
        hf = a1f * hf + a2f * pltpu.roll(hf, 64, axis=1) + hl_ref[pl.ds(rf, nb), :LANES]
        hb = a1b * hb + a2b * pltpu.roll(hb, 64, axis=1) + hl_ref[pl.ds(rb, nb), LANES:]
        return hf, hb

    z = jnp.zeros((nb, LANES), F32)
    lax.fori_loop(0, nch, step, (z, z))
    y_ref[...] = (_dot(u, mq_ref[:S5_W, :])
                  + _dot(he_ref[...].astype(BF16), mq_ref[S5_W:, :]))


def _s5(u_g, ptab, mq, atab, nb, nch, ncc):
    g, r, _ = u_g.shape
    return pl.pallas_call(
        functools.partial(_s5_kernel, nb=nb, nch=nch, ncc=ncc),
        grid=(g,),
        in_specs=[pl.BlockSpec((None, r, S5_W), lambda i: (i, 0, 0)),
                  pl.BlockSpec((None, S5_W, 2 * LANES), lambda i: (i, 0, 0)),
                  pl.BlockSpec((None, S5_W + 2 * LANES, S5_W), lambda i: (i, 0, 0)),
                  pl.BlockSpec((None, 8, 2 * LANES), lambda i: (i, 0, 0))],
        out_specs=pl.BlockSpec((None, r, S5_W), lambda i: (i, 0, 0)),
        out_shape=jax.ShapeDtypeStruct((g, r, S5_W), F32),
        scratch_shapes=[pltpu.VMEM((r, 2 * LANES), F32), pltpu.VMEM((r, 2 * LANES), F32)],
        compiler_params=_cparams(1, VMEM_LIMIT),
        name="s5_mixer",
    )(u_g, ptab, mq, atab)


def _s5_tables(lam_re, lam_im, log_dt, b_re, b_im, c_re, c_im):
    t = S5_T
    dt = jnp.exp(log_dt)[..., None]
    ldr, ldi = lam_re * dt, lam_im * dt
    jj = jnp.arange(t + 1, dtype=F32)[None, None, :, None]
    mag = jnp.exp(ldr[:, :, None, :] * jj)
    pr, pi = mag * jnp.cos(ldi[:, :, None, :] * jj), mag * jnp.sin(ldi[:, :, None, :] * jj)
    ar, ai = pr[:, :, 1], pi[:, :, 1]
    den = lam_re * lam_re + lam_im * lam_im
    qr = ((ar - 1.0) * lam_re + ai * lam_im) / den
    qi = (ai * lam_re - (ar - 1.0) * lam_im) / den
    bbr = qr[..., None] * b_re - qi[..., None] * b_im
    bbi = qr[..., None] * b_im + qi[..., None] * b_re
    cpr = c_re[:, :, None] * pr[:, :, :, None, :] - c_im[:, :, None] * pi[:, :, :, None, :]
    cpi = c_re[:, :, None] * pi[:, :, :, None, :] + c_im[:, :, None] * pr[:, :, :, None, :]
    klag = (jnp.einsum('rgjop,rgph->rgjoh', cpr[:, :, :t], bbr, precision=HI)
            - jnp.einsum('rgjop,rgph->rgjoh', cpi[:, :, :t], bbi, precision=HI))
    s_i = jnp.arange(t)[:, None]
    t_i = jnp.arange(t)[None, :]

    def toeplitz(k, d):
        m = k[:, jnp.clip(d, 0, t - 1)]
        m = jnp.where((d >= 0)[None, :, :, None, None], m, 0.0)
        return jnp.transpose(m, (0, 1, 4, 2, 3))

    m_sum = (toeplitz(klag[0], t_i - s_i) + toeplitz(klag[1], s_i - t_i)).reshape(S5_G, S5_W, S5_W)
    pbr = pr[..., None] * bbr[:, :, None] - pi[..., None] * bbi[:, :, None]
    pbi = pr[..., None] * bbi[:, :, None] + pi[..., None] * bbr[:, :, None]

    def ptab(r, exps):
        re = jnp.transpose(pbr[r][:, exps], (0, 1, 3, 2))
        im = jnp.transpose(pbi[r][:, exps], (0, 1, 3, 2))
        return jnp.concatenate([re, im], axis=-1).reshape(S5_G, S5_W, 2 * S5_P)

    ar_t = np.arange(t)
    p2 = jnp.concatenate([ptab(0, t - 1 - ar_t), ptab(1, ar_t)], axis=-1)

    def qtab(r, exps):
        re = jnp.transpose(cpr[r][:, exps], (0, 3, 1, 2)).reshape(S5_G, S5_P, S5_W)
        im = jnp.transpose(cpi[r][:, exps], (0, 3, 1, 2)).reshape(S5_G, S5_P, S5_W)
        return jnp.concatenate([re, -im], axis=1)

    mq = jnp.concatenate([m_sum, qtab(0, ar_t + 1), qtab(1, t - ar_t)], axis=1)
    art, ait = pr[:, :, t], pi[:, :, t]
    a1 = jnp.concatenate([art[0], art[0], art[1], art[1]], axis=-1)
    a2 = jnp.concatenate([-ait[0], ait[0], -ait[1], ait[1]], axis=-1)
    atab = jnp.concatenate([a1[:, None], a2[:, None], jnp.zeros((S5_G, 6, 4 * S5_P), F32)], axis=1)
    return p2.astype(BF16), mq.astype(BF16), atab


def _split3(x):
    x1 = x.astype(BF16)
    r1 = x - x1.astype(F32)
    x2 = r1.astype(BF16)
    x3 = (r1 - x2.astype(F32)).astype(BF16)
    return x1, x2, x3


def _dot01(m, x):
    x1, x2, x3 = _split3(x)
    return (_dot(m, x3) + _dot(m, x2)) + _dot(m, x1)


def _tile_cumsum_mask(rev):
    t = lax.broadcasted_iota(I32, (TM, TM), 0)
    s = lax.broadcasted_iota(I32, (TM, TM), 1)
    order = (s >= t) if rev else (s <= t)
    return (order & ((t // CHUNK) == (s // CHUNK))).astype(BF16)


def _head_lane_mask(rev, width):
    t = lax.broadcasted_iota(I32, (CHUNK, width), 0)
    s = lax.broadcasted_iota(I32, (CHUNK, width), 1) % CHUNK
    return (s >= t) if rev else (s <= t)


def _gla_kernel(qk_ref, v_ref, misc_ref, a2_ref, ab_ref, o_ref, st_ref, *, rev):
    j = pl.program_id(1)

    @pl.when(j == 0)
    def _():
        st_ref[...] = jnp.zeros_like(st_ref)

    lane_k = lax.broadcasted_iota(I32, (1, LANES), 1) // GLA_DK
    lane_v = lax.broadcasted_iota(I32, (1, DG), 1) // GLA_DV
    causal = _head_lane_mask(rev, DG)
    st_diag = (lax.broadcasted_iota(I32, (DG, 1), 0) // GLA_DV) == lane_k

    z = _dot(misc_ref[...].astype(BF16), a2_ref[...]) + ab_ref[...]
    la_t = -(jnp.maximum(-z, 0.0) + jnp.log1p(jnp.exp(-jnp.abs(z)))) * (1.0 / GLA_TAU)
    b_t = _dot01(_tile_cumsum_mask(rev), la_t)
    st = st_ref[...]
    chunks = range(TM // CHUNK)
    for c in (reversed(chunks) if rev else chunks):
        rows = slice(c * CHUNK, (c + 1) * CHUNK)
        la, bcum = la_t[rows], b_t[rows]
        q = qk_ref[rows, :LANES] * (GLA_DK ** -0.5)
        k = qk_ref[rows, LANES:]
        v = v_ref[rows, :]
        tot = jnp.sum(la, axis=0, keepdims=True)
        q_in = (q * jnp.exp(bcum)).astype(BF16)
        k_in = k * jnp.exp(-bcum)
        k_dec = (k * jnp.exp(tot - bcum)).astype(BF16)
        k4 = jnp.concatenate([jnp.where(lane_k == h, k_in, 0.0) for h in range(GLA_H)], axis=0)
        att = jnp.where(causal, _dot(q_in, k4.astype(BF16), _NT), 0.0)
        vb = v.astype(BF16)
        vbd = jnp.concatenate([jnp.where(lane_v == h, vb, jnp.zeros_like(vb)) for h in range(GLA_H)], axis=0)
        o = _dot(att.astype(BF16), vbd) + _dot(q_in, jnp.where(st_diag, st, 0.0).astype(BF16), _NT)
        o_ref[rows, :] = o
        st = st * jnp.exp(tot) + _dot(vb, k_dec, _TN)
    st_ref[...] = st


def _tile_order(rev, j, nt):
    return jnp.where(j == 0, 0, nt - j) if rev else j


def _gla(p, a2m, abm, rev):
    b, s, _ = p.shape
    nt = s // TM
    return pl.pallas_call(
        functools.partial(_gla_kernel, rev=rev),
        grid=(b, nt),
        in_specs=[pl.BlockSpec((None, TM, DG), lambda i, j: (i, _tile_order(rev, j, nt), C_QK // DG)),
                  pl.BlockSpec((None, TM, DG), lambda i, j: (i, _tile_order(rev, j, nt), C_V // DG)),
                  pl.BlockSpec((None, TM, LANES), lambda i, j: (i, _tile_order(rev, j, nt), C_MISC // LANES)),
                  pl.BlockSpec((LANES, LANES), lambda i, j: (0, 0)),
                  pl.BlockSpec((1, LANES), lambda i, j: (0, 0))],
        out_specs=pl.BlockSpec((None, TM, DG), lambda i, j: (i, _tile_order(rev, j, nt), 0)),
        out_shape=jax.ShapeDtypeStruct((b, s, DG), F32),
        scratch_shapes=[pltpu.VMEM((DG, LANES), F32)],
        compiler_params=_cparams(2, VMEM_LIMIT),
        name="gla_bwd" if rev else "gla_fwd",
    )(p, p, p, a2m, abm)


def _ssd_kernel(xbc_ref, misc_ref, dtb_ref, av_ref, y_ref, sb_ref, *, rev):
    j = pl.program_id(1)

    @pl.when(j == 0)
    def _():
        sb_ref[...] = jnp.zeros_like(sb_ref)

    lane = lax.broadcasted_iota(I32, (1, LANES), 1)
    lane_g = lane // SSD_N
    lane_h = lax.broadcasted_iota(I32, (1, DG), 1) // SSD_HD
    row_h = lax.broadcasted_iota(I32, (DG, 1), 0) // SSD_HD
    blockmask = row_h == lane_h
    causal = _head_lane_mask(rev, DG)
    diag = (lax.broadcasted_iota(I32, (CHUNK, DG), 0)
            == lax.broadcasted_iota(I32, (CHUNK, DG), 1) % CHUNK)

    def per_head(x):
        xr = pltpu.roll(x, SSD_N, axis=1)
        return jnp.concatenate([jnp.where(lane < SSD_N, x, xr), jnp.where(lane < SSD_N, xr, x)], axis=1)

    zm = misc_ref[...] + dtb_ref[...]
    sp = jnp.maximum(zm, 0.0) + jnp.log1p(jnp.exp(-jnp.abs(zm)))
    dt_t = jnp.zeros((TM, DG), F32)
    for h in range(SSD_H):
        col = jnp.sum(jnp.where(lane == MISC_DT + SSD_H * int(rev) + h, sp, 0.0), axis=1, keepdims=True)
        dt_t = jnp.where(lane_h == h, col, dt_t)
    da_t = dt_t * av_ref[...]
    cum_t = _dot01(_tile_cumsum_mask(rev), da_t)

    sb = sb_ref[...]
    chunks = range(TM // CHUNK)
    for c in (reversed(chunks) if rev else chunks):
        rows = slice(c * CHUNK, (c + 1) * CHUNK)
        da, cum = da_t[rows], cum_t[rows]
        bmc = xbc_ref[rows, DG:DG + LANES]
        cmc = xbc_ref[rows, DG + LANES:]
        xdt = (xbc_ref[rows, :DG] * dt_t[rows]).astype(BF16)
        tot = jnp.sum(da, axis=0, keepdims=True)
        cum_row = jnp.sum(jnp.where(diag, cum, 0.0), axis=0, keepdims=True)
        decay = jnp.where(causal, jnp.exp(jnp.minimum(cum - cum_row, 0.0)), 0.0)
        b4 = jnp.concatenate([jnp.where(lane_g == h // (SSD_H // SSD_G), bmc, 0.0) for h in range(SSD_H)],
                             axis=0)
        scores = _dot(cmc.astype(BF16), b4.astype(BF16), _NT) * decay
        xbd = jnp.concatenate([jnp.where(lane_h == h, xdt, jnp.zeros_like(xdt)) for h in range(SSD_H)],
                              axis=0)
        cdec = per_head(cmc) * jnp.exp(cum)
        y = _dot(scores.astype(BF16), xbd) + _dot(cdec.astype(BF16), sb.astype(BF16))
        y_ref[rows, :] = y
        bdec = per_head(bmc) * jnp.exp(tot - cum)
        upd = _dot(bdec.astype(BF16), xdt, _TN)
        sb = sb * jnp.exp(tot) + jnp.where(blockmask, upd, 0.0)
    sb_ref[...] = sb


def _ssd(p, dtb, av, rev):
    b, s, _ = p.shape
    nt = s // TM
    return pl.pallas_call(
        functools.partial(_ssd_kernel, rev=rev),
        grid=(b, nt),
        in_specs=[pl.BlockSpec((None, TM, 512), lambda i, j: (i, _tile_order(rev, j, nt), C_XBC // 512)),
                  pl.BlockSpec((None, TM, LANES), lambda i, j: (i, _tile_order(rev, j, nt), C_MISC // LANES)),
                  pl.BlockSpec((1, LANES), lambda i, j: (0, 0)),
                  pl.BlockSpec((1, DG), lambda i, j: (0, 0))],
        out_specs=pl.BlockSpec((None, TM, DG), lambda i, j: (i, _tile_order(rev, j, nt), 0)),
        out_shape=jax.ShapeDtypeStruct((b, s, DG), F32),
        scratch_shapes=[pltpu.VMEM((DG, DG), F32)],
        compiler_params=_cparams(2, VMEM_LIMIT),
        name="ssd_bwd" if rev else "ssd_fwd",
    )(p, p, dtb, av)


def _ssd_tables(a_log, dt_bias, r):
    dtb = jnp.zeros((1, LANES), F32).at[0, MISC_DT + SSD_H * r:MISC_DT + SSD_H * (r + 1)].set(dt_bias[r])
    av = jnp.repeat(-jnp.exp(a_log[r]), SSD_HD)[None, :]
    return dtb, av


def _gelu_tanh(x):
    return 0.5 * x * (1.0 + jnp.tanh(math.sqrt(2.0 / math.pi) * (x + 0.044715 * x * x * x)))


def _post_kernel(x_ref, mod_ref, us5_ref, r_ref, z_ref, xs_ref, fl_ref, fc_ref, s5y_ref,
                 gf_ref, gb_ref, sf_ref, sbk_ref, vec_ref, wglu_ref, avg_ref, wout_ref,
                 gffn_ref, rw_ref, rb_ref, tri_ref, upper_ref,
                 xo_ref, loc_ref, ri_ref, rg_ref, tab_ref):
    j = pl.program_id(1)
    s5_d, gla_g, ssd_d, ssd_g = vec_ref[0:1, :], vec_ref[1:2, :], vec_ref[2:3, :], vec_ref[3:4, :]
    four = jnp.where(j == 0, fc_ref[...], fl_ref[...])
    y5 = s5y_ref[...] + s5_d * us5_ref[...]
    act = _gelu_tanh(y5)
    s5o = act * jax.nn.sigmoid(_dot(act.astype(BF16), wglu_ref[...]))
    og = gf_ref[...] + gb_ref[...]
    ms = _dot(og * og, avg_ref[...], precision=HI)
    glo = og * lax.rsqrt(ms + EPS) * gla_g * _silu(r_ref[...])
    ys = (sf_ref[...] + sbk_ref[...] + ssd_d * xs_ref[...]) * _silu(z_ref[...])
    sso = _rms(ys, ssd_g)
    mix = jnp.concatenate([four, s5o, glo, sso], axis=1).astype(BF16)
    x = x_ref[...] + mod_ref[2:3, :] * _dot(mix, wout_ref[...])
    xo_ref[...] = x
    h = _rms(x, gffn_ref[...]) * (1.0 + mod_ref[4:5, :]) + mod_ref[3:4, :]

    logits = _dot(h, rw_ref[...], precision=HI) + rb_ref[...]
    lane = lax.broadcasted_iota(I32, (TM, LANES), 1)
    vals, idxs = [], []
    for _k in range(TOP_K):
        mx = jnp.max(logits, axis=1, keepdims=True)
        ix = jnp.min(jnp.where(logits == mx, lane, LANES), axis=1, keepdims=True)
        vals.append(mx)
        idxs.append(ix)
        logits = jnp.where(lane == ix, -jnp.inf, logits)
    es = [jnp.exp(v - vals[0]) for v in vals]
    den = es[0] + es[1] + es[2] + es[3]
    onehots = [(lane == ix) for ix in idxs]
    oh = sum(o.astype(F32) for o in onehots)

    ahead = _dot(tri_ref[...], oh.astype(BF16))
    cnt = jnp.sum(oh, axis=0, keepdims=True)
    cpad = jnp.floor((cnt + (SEG - 1.0)) * (1.0 / SEG)) * SEG
    lstart = _dot(jnp.broadcast_to(cpad, (8, LANES)).astype(BF16), upper_ref[...])[0:1]
    slot_of = lstart + ahead
    lane_r = lax.broadcasted_iota(I32, (1, LMAX), 1)
    ri = jnp.zeros((TM, LANES), I32)
    rg = jnp.zeros((TM, LANES), F32)
    pt = jnp.zeros((TM, LMAX), F32)
    for k in range(TOP_K):
        lpos = jnp.sum(jnp.where(onehots[k], slot_of, 0.0), axis=1, keepdims=True).astype(I32)
        pt = pt + (lane_r == lpos).astype(F32)
        ri = jnp.where(lane == k, idxs[k], ri)
        ri = jnp.where(lane == TOP_K + k, lpos, ri)
        rg = jnp.where(lane == k, es[k] / den, rg)
    ri_ref[...] = ri
    rg_ref[...] = rg
    loc_ref[...] = _pack_bf16_pairs(_dot(pt.astype(BF16), h.astype(BF16), _TN))
    row = lax.broadcasted_iota(I32, (8, LANES), 0)
    tab_ref[...] = jnp.where(row == 0, cnt.astype(I32), jnp.where(row == 1, lstart.astype(I32), 0))


def _post(x_all, mod, p, four_l, four_c, s5y, gla_f, gla_b, ssd_f, ssd_b, vecs, wglu, avg, wout, gffn,
          rw, rb, tri, upper, seq, ctx):
    b, s, _ = x_all.shape
    nt = s // TM
    tile = lambda w, col: pl.BlockSpec((None, TM, w), lambda i, j: (i, j, col))
    full = lambda a: pl.BlockSpec(a.shape, lambda i, j: (0,) * a.ndim)
    per_tile = lambda r, w: pl.BlockSpec((None, None, r, w), lambda i, j: (i, j, 0, 0))
    return pl.pallas_call(
        _post_kernel,
        grid=(b, nt),
        in_specs=[tile(D, 0),
                  pl.BlockSpec((None, None, 8, D), lambda i, j: (i, jnp.minimum(j, 1), 0, 0)),
                  tile(DG, C_S5 // DG), tile(DG, C_R // DG), tile(DG, C_Z // DG), tile(DG, C_XBC // DG),
                  pl.BlockSpec((None, TM, DG), lambda i, j: (i, jnp.maximum(j - 1, 0), 0)),
                  pl.BlockSpec((None, ctx, DG), lambda i, j: (i, 0, 0)),
                  tile(DG, 0), tile(DG, 0), tile(DG, 0), tile(DG, 0), tile(DG, 0),
                  full(vecs), full(wglu), full(avg), full(wout), full(gffn), full(rw), full(rb), full(tri),
                  full(upper)],
        out_specs=[tile(D, 0), per_tile(LMAX, D // 2), tile(LANES, 0), tile(LANES, 0), per_tile(8, LANES)],
        out_shape=[jax.ShapeDtypeStruct((b, s, D), F32),
                   jax.ShapeDtypeStruct((b, nt, LMAX, D // 2), U32),
                   jax.ShapeDtypeStruct((b, s, LANES), I32),
                   jax.ShapeDtypeStruct((b, s, LANES), F32),
                   jax.ShapeDtypeStruct((b, nt, 8, LANES), I32)],
        compiler_params=_cparams(2, VMEM_LIMIT),
        name="post_mixer_router",
    )(x_all, mod, p, p, p, p, four_l, four_c, s5y, gla_f, gla_b, ssd_f, ssd_b,
      vecs, wglu, avg, wout, gffn, rw, rb, tri, upper)


T_NSEG, T_LSTART, T_GSTART = 0, 32, 64


def _segment_copies(tab_ref, make_copy, action):
    def per_expert(e, c):
        lstart = pl.multiple_of(tab_ref[0, 0, T_LSTART + e], SEG)
        gstart = pl.multiple_of(tab_ref[0, 0, T_GSTART + e], SEG)

        def per_piece(q, c2):
            action(make_copy(lstart + q * SEG, gstart + q * SEG))
            return c2
        return lax.fori_loop(0, tab_ref[0, 0, T_NSEG + e], per_piece, c)
    lax.fori_loop(0, N_EXP, per_expert, 0)


def _dispatch_kernel(tab_ref, loc_ref, buf_in_ref, buf_ref, sem):
    del buf_in_ref

    def make_copy(lrow, grow):
        return pltpu.make_async_copy(loc_ref.at[0, pl.ds(lrow, SEG)], buf_ref.at[pl.ds(grow, SEG)], sem)

    _segment_copies(tab_ref, make_copy, lambda cp: cp.start())
    _segment_copies(tab_ref, make_copy, lambda cp: cp.wait())


def _dispatch(tab3, loc, buf0):
    ntile = loc.shape[0]
    return pl.pallas_call(
        _dispatch_kernel,
        grid=(ntile,),
        in_specs=[pl.BlockSpec((1, 1, LANES), lambda i: (i, 0, 0), memory_space=pltpu.SMEM),
                  pl.BlockSpec((1, LMAX, D // 2), lambda i: (i, 0, 0)),
                  pl.BlockSpec(memory_space=pl.ANY)],
        out_specs=pl.BlockSpec(memory_space=pl.ANY),
        out_shape=jax.ShapeDtypeStruct(buf0.shape, U32),
        scratch_shapes=[pltpu.SemaphoreType.DMA(())],
        input_output_aliases={2: 0},
        compiler_params=_cparams(1, VMEM_LIMIT),
        name="moe_dispatch",
    )(tab3, loc, buf0)


def _expert_kernel(be_ref, nu_ref, x_ref, wgu_ref, wd_ref, bg_ref, bu_ref, bd_ref, perm_ref, y_ref,
                   wg_sc, wu_sc, wd_sc):
    i = pl.program_id(0)
    used = i < nu_ref[0]
    new_expert = (i == 0) | (be_ref[i] != be_ref[jnp.maximum(i - 1, 0)])

    @pl.when(used & new_expert)
    def _():
        for c in range(2 * D // DG):
            d = _dot(wgu_ref[:, c * DG:(c + 1) * DG].astype(BF16), perm_ref[...])
            wg_sc[:, c * LANES:(c + 1) * LANES] = d[:, :LANES].astype(BF16)
            wu_sc[:, c * LANES:(c + 1) * LANES] = d[:, LANES:].astype(BF16)
        wd_sc[...] = wd_ref[...].astype(BF16)

    @pl.when(used)
    def _():
        x = _unpack_bf16_pairs(x_ref[...]).astype(BF16)
        g = _dot(x, wg_sc[...]) + bg_ref[...]
        u = _dot(x, wu_sc[...]) + bu_ref[...]
        gate = jnp.minimum(g, LIMIT)
        up = jnp.clip(u, -LIMIT, LIMIT)
        act = (up + 1.0) * gate * jax.nn.sigmoid(ALPHA * gate)
        y = _dot(act.astype(BF16), wd_sc[...]) + bd_ref[...]
        y_ref[...] = _pack_bf16_pairs(y)

    @pl.when(jnp.logical_not(used))
    def _():
        y_ref[...] = jnp.zeros_like(y_ref)


def _deinterleave_perm():
    pm = np.zeros((DG, DG), np.float32)
    o = np.arange(LANES)
    pm[2 * o, o] = 1.0
    pm[2 * o + 1, LANES + o] = 1.0
    return jnp.asarray(pm, BF16)


def _experts(blk_e, n_used, buf, wgu, wd, bg, bu, bd):
    nblk = buf.shape[0] // BM
    row = lambda i, be, nu: (jnp.minimum(i, nu[0] - 1), 0)
    wsel = lambda i, be, nu: (be[jnp.minimum(i, nu[0] - 1)], 0, 0)
    grid_spec = pltpu.PrefetchScalarGridSpec(
        num_scalar_prefetch=2,
        grid=(nblk,),
        in_specs=[pl.BlockSpec((BM, D // 2), row),
                  pl.BlockSpec((None, D, 2 * D), wsel), pl.BlockSpec((None, D, D), wsel),
                  pl.BlockSpec((None, 1, D), wsel), pl.BlockSpec((None, 1, D), wsel),
                  pl.BlockSpec((None, 1, D), wsel),
                  pl.BlockSpec((DG, DG), lambda i, be, nu: (0, 0))],
        out_specs=pl.BlockSpec((BM, D // 2), lambda i, be, nu: (i, 0)),
        scratch_shapes=[pltpu.VMEM((D, D), BF16), pltpu.VMEM((D, D), BF16), pltpu.VMEM((D, D), BF16)],
    )
    return pl.pallas_call(
        _expert_kernel,
        grid_spec=grid_spec,
        out_shape=jax.ShapeDtypeStruct(buf.shape, U32),
        compiler_params=_cparams(1, VMEM_LIMIT),
        name="moe_experts",
    )(blk_e, n_used, buf, wgu, wd, bg, bu, bd, _deinterleave_perm())


def _combine_kernel(tab_ref, tabn_ref, x_ref, ri_ref, rg_ref, mod_ref, fg_ref, yb_ref, o_ref, yloc, sem,
                    *, final):
    i = pl.program_id(0)
    n = pl.num_programs(0)
    slot = i % 2

    def make_copy(sl):
        return lambda lrow, grow: pltpu.make_async_copy(
            yb_ref.at[pl.ds(grow, SEG)], yloc.at[sl, pl.ds(lrow, SEG)], sem.at[sl])

    @pl.when(i == 0)
    def _():
        yloc[...] = jnp.zeros_like(yloc)
        _segment_copies(tab_ref, make_copy(0), lambda cp: cp.start())

    @pl.when(i + 1 < n)
    def _():
        _segment_copies(tabn_ref, make_copy(1 - slot), lambda cp: cp.start())

    _segment_copies(tab_ref, make_copy(slot), lambda cp: cp.wait())

    lane_r = lax.broadcasted_iota(I32, (1, LMAX), 1)
    g = jnp.zeros((TM, LMAX), F32)
    for k in range(TOP_K):
        g = g + jnp.where(lane_r == ri_ref[:, TOP_K + k:TOP_K + k + 1], rg_ref[:, k:k + 1], 0.0)
    g_hi = g.astype(BF16)
    g_lo = (g - g_hi.astype(F32)).astype(BF16)
    y = _unpack_bf16_pairs(yloc[slot]).astype(BF16)
    f = _dot(g_hi, y) + _dot(g_lo, y)
    x = x_ref[...] + mod_ref[5:6, :] * f
    o_ref[...] = _rms(x, fg_ref[...]) if final else x


def _combine(tab3, x_flat, ri_flat, rg_flat, mod, fg, ybuf, nt, final):
    n = x_flat.shape[0]
    steps = n // TM
    tspec = lambda f: pl.BlockSpec((1, 1, LANES), f, memory_space=pltpu.SMEM)
    return pl.pallas_call(
        functools.partial(_combine_kernel, final=final),
        grid=(steps,),
        in_specs=[tspec(lambda i: (i, 0, 0)),
                  tspec(lambda i: (jnp.minimum(i + 1, steps - 1), 0, 0)),
                  pl.BlockSpec((TM, D), lambda i: (i, 0)),
                  pl.BlockSpec((TM, LANES), lambda i: (i, 0)),
                  pl.BlockSpec((TM, LANES), lambda i: (i, 0)),
                  pl.BlockSpec((None, None, 8, D), lambda i: (i // nt, jnp.minimum(i % nt, 1), 0, 0)),
                  pl.BlockSpec((1, D), lambda i: (0, 0)),
                  pl.BlockSpec(memory_space=pl.ANY)],
        out_specs=pl.BlockSpec((TM, D), lambda i: (i, 0)),
        out_shape=jax.ShapeDtypeStruct((n, D), F32),
        scratch_shapes=[pltpu.VMEM((2, LMAX, D // 2), U32), pltpu.SemaphoreType.DMA((2,))],
        compiler_params=_cparams(1, VMEM_LIMIT),
        name="moe_combine",
    )(tab3, tab3, x_flat, ri_flat, rg_flat, mod, fg, ybuf)


def _repack_w_in(w):
    four, s5, q, k, v, r, a1, z, xbc, dt = jnp.split(
        w, np.cumsum([256, 256, 128, 128, 256, 256, 32, 256, 512, 8])[:-1].tolist(), axis=-1)
    misc = jnp.concatenate([a1, dt, jnp.zeros((D, LANES - 40), w.dtype)], axis=-1)
    return jnp.concatenate([four, s5, q, k, v, r, z, xbc, misc], axis=-1).astype(BF16)


def kernel(x, c, ctx, c_ctx, ada_w, ada_b, norm_mix_g, norm_ffn_g, w_in, w_out, four_w, s5_lam_re, s5_lam_im, s5_log_dt, s5_b_re, s5_b_im, s5_c_re, s5_c_im, s5_d, s5_w_glu, gla_a2, gla_a_b, gla_norm_g, ssd_conv_w, ssd_conv_b, ssd_a_log, ssd_dt_bias, ssd_d, ssd_norm_g, router_w, router_b, exp_w_gu, exp_b_gu, exp_w_dn, exp_b_dn, final_g):
    bsz, seq, _ = x.shape
    clen = ctx.shape[1]
    depth = ada_w.shape[0]
    assert clen == TM and seq % TM == 0 and bsz == 8
    s = clen + seq
    nt = s // TM
    n_tok = bsz * s
    nch, ncc = s // S5_T, clen // S5_T

    cond = jnp.concatenate([c, c_ctx[None, :], jnp.zeros((16 - bsz - 1, D), F32)], axis=0)
    mods = _ada(cond, ada_w, ada_b).reshape(depth, 16, N_MOD, D)
    m_lat = mods[:, :bsz]
    m_ctx = jnp.broadcast_to(mods[:, bsz:bsz + 1], m_lat.shape)
    mods = jnp.stack([m_ctx, m_lat], axis=2)
    mods = jnp.pad(mods, ((0, 0), (0, 0), (0, 0), (0, 2), (0, 0)))

    x_all = jnp.concatenate([ctx, x], axis=1)
    csm = _channel_dft()
    f_lat, f_ctx = _dft_matrix(seq), _dft_matrix(clen)
    avg = jnp.asarray(np.kron(np.eye(GLA_H), np.full((GLA_DV, GLA_DV), 1.0 / GLA_DV)), F32)
    tri = jnp.asarray(np.tril(np.ones((TM, TM)), -1), BF16)
    upper = jnp.asarray(np.triu(np.ones((LANES, LANES)), 1), BF16)

    ntile = n_tok // TM
    nblk = -(-(n_tok * TOP_K + ntile * N_EXP * (SEG - 1) + N_EXP * (BM - 1)) // BM)
    n_rows = nblk * BM

    for l in range(depth):
        mod = mods[l]
        w_in_p = _repack_w_in(w_in[l])
        conv_w = jnp.pad(ssd_conv_w[l], ((0, 8 - SSD_K), (0, 0)))
        p, four_l, four_c = _pre(x_all, mod, norm_mix_g[l][None], w_in_p, conv_w, ssd_conv_b[l][None],
                                 csm, seq, clen)
        fw = four_w[l].astype(BF16)
        fy_l = _fourier(f_lat, four_l.reshape(bsz, 2 * seq, DG), fw)
        fy_c = _fourier(f_ctx, four_c.reshape(bsz, 2 * clen, DG), fw)

        ptab, mq, atab = _s5_tables(s5_lam_re[l], s5_lam_im[l], s5_log_dt[l], s5_b_re[l], s5_b_im[l],
                                    s5_c_re[l], s5_c_im[l])
        u5 = p[:, :, C_S5:C_S5 + DG].reshape(bsz, nch, S5_T, S5_G, S5_CH)
        u5 = jnp.transpose(u5, (3, 1, 0, 2, 4)).reshape(S5_G, nch * bsz, S5_W)
        y5 = _s5(u5, ptab, mq, atab, bsz, nch, ncc).reshape(S5_G, nch, bsz, S5_T, S5_CH)
        y5 = jnp.transpose(y5, (2, 1, 3, 0, 4)).reshape(bsz, s, DG)

        gla_o, ssd_y = [], []
        for r in range(2):
            a2m = jnp.zeros((LANES, LANES), F32).at[GLA_RANK * r:GLA_RANK * (r + 1)].set(gla_a2[l, r])
            gla_o.append(_gla(p, a2m.astype(BF16), gla_a_b[l, r][None, :], rev=bool(r)))
            ssd_y.append(_ssd(p, *_ssd_tables(ssd_a_log[l], ssd_dt_bias[l], r), rev=bool(r)))

        vecs = jnp.stack([s5_d[l], jnp.tile(gla_norm_g[l], GLA_H), jnp.repeat(ssd_d[l], SSD_HD),
                          ssd_norm_g[l]] + [jnp.zeros((DG,), F32)] * 4)
        rw = jnp.pad(router_w[l], ((0, 0), (0, LANES - N_EXP)))
        rb = jnp.concatenate([router_b[l], jnp.full((LANES - N_EXP,), -1e30, F32)])[None]
        x_mid, loc, ri, rg, tab = _post(x_all, mod, p, fy_l, fy_c, y5, gla_o[0], gla_o[1], ssd_y[0], ssd_y[1],
                                        vecs, s5_w_glu[l].astype(BF16), avg, w_out[l].astype(BF16),
                                        norm_ffn_g[l][None], rw, rb, tri, upper, seq, clen)

        tab = tab.reshape(ntile, 8, LANES)
        cnt, lstart = tab[:, 0, :N_EXP], tab[:, 1, :N_EXP]
        cpad = (cnt + SEG - 1) // SEG * SEG
        in_expert = jnp.cumsum(cpad, axis=0) - cpad
        region = (jnp.sum(cpad, axis=0) + BM - 1) // BM * BM
        pad_end = jnp.cumsum(region)
        gstart = (pad_end - region)[None, :] + in_expert
        tab3 = jnp.concatenate([cpad // SEG, lstart, gstart, jnp.zeros_like(cnt)], axis=1)
        tab3 = tab3.astype(I32).reshape(ntile, 1, LANES)
        blk_start = jnp.arange(nblk, dtype=I32) * BM
        blk_e = jnp.minimum(jnp.sum((pad_end[None, :] <= blk_start[:, None]).astype(I32), axis=1), N_EXP - 1)
        n_used = (pad_end[-1] // BM).astype(I32)[None]

        buf = _dispatch(tab3, loc.reshape(ntile, LMAX, D // 2), jnp.zeros((n_rows, D // 2), U32))
        bgu = exp_b_gu[l].reshape(N_EXP, 1, D, 2)
        ybuf = _experts(blk_e, n_used, buf, exp_w_gu[l], exp_w_dn[l], bgu[..., 0], bgu[..., 1],
                        exp_b_dn[l][:, None, :])
        x_all = _combine(tab3, x_mid.reshape(n_tok, D), ri.reshape(n_tok, LANES), rg.reshape(n_tok, LANES),
                         mod, final_g[None], ybuf, nt, final=(l == depth - 1)).reshape(bsz, s, D)

    return x_all[:, clen:, :]
```

```python
import functools
import math

import jax
import jax.numpy as jnp
import numpy as np
from jax import lax
from jax.experimental import pallas as pl
from jax.experimental.pallas import tpu as pltpu

F32 = jnp.float32
BF16 = jnp.bfloat16
I32 = jnp.int32
U32 = jnp.uint32
HI = lax.Precision.HIGHEST

D = 1024
DG = 256
N_MOD = 6
GRID_W = 64
EPS = 1e-6

S5_CH = 16
S5_G = DG // S5_CH
S5_P = 64
S5_T = 128

GLA_H = 4
GLA_DK = 32
GLA_DV = 64
GLA_RANK = 16
GLA_TAU = 16.0
CHUNK = 64

SSD_H = 4
SSD_HD = 64
SSD_G = 2
SSD_N = 64
SSD_K = 5

N_EXP = 32
TOP_K = 4
LIMIT = 7.0
ALPHA = 1.702

TM = 256
BM = 512
SEG = 8
LMAX = 1280
LANES = 128

C_FOUR, C_S5, C_QK, C_V, C_R, C_Z, C_XBC, C_MISC = 0, 256, 512, 768, 1024, 1280, 1536, 2048
D_P = 2176
MISC_A1 = 0
MISC_DT = 32

VMEM_LIMIT = 56 * 1024 * 1024


def _cparams(n_axes, vmem=None):
    return pltpu.CompilerParams(dimension_semantics=("arbitrary",) * n_axes,
                                vmem_limit_bytes=vmem)


def _silu(x):
    return x * jax.nn.sigmoid(x)


def _rms(x, g):
    return x * lax.rsqrt(jnp.mean(x * x, axis=-1, keepdims=True) + EPS) * g


def _dot(a, b, dims=None, precision=None):
    if dims is None:
        dims = (((a.ndim - 1,), (0,)), ((), ()))
    return lax.dot_general(a, b, dims, precision=precision, preferred_element_type=F32)


_NT = (((1,), (1,)), ((), ()))
_TN = (((0,), (0,)), ((), ()))


def _pack_bf16_pairs(a):
    n = a.shape[1] // 2
    hi = pltpu.bitcast(a[:, :n].astype(BF16).astype(F32), U32)
    lo = pltpu.bitcast(a[:, n:].astype(BF16).astype(F32), U32)
    return hi | (lo >> 16)


def _unpack_bf16_pairs(w):
    hi = pltpu.bitcast(w & jnp.uint32(0xFFFF0000), F32)
    lo = pltpu.bitcast(w << 16, F32)
    return jnp.concatenate([hi, lo], axis=1)


def _ada_kernel(cond_ref, w_ref, b_ref, o_ref):
    c = _silu(cond_ref[...])
    o_ref[...] = _dot(c, w_ref[...], precision=HI) + b_ref[...]


def _ada(cond, ada_w, ada_b):
    depth = ada_w.shape[0]
    rows = cond.shape[0]
    tn = 1536
    return pl.pallas_call(
        _ada_kernel,
        grid=(depth, (N_MOD * D) // tn),
        in_specs=[pl.BlockSpec((rows, D), lambda l, n: (0, 0)),
                  pl.BlockSpec((None, D, tn), lambda l, n: (l, 0, n)),
                  pl.BlockSpec((None, 1, tn), lambda l, n: (l, 0, n))],
        out_specs=pl.BlockSpec((None, rows, tn), lambda l, n: (l, 0, n)),
        out_shape=jax.ShapeDtypeStruct((depth, rows, N_MOD * D), F32),
        compiler_params=_cparams(2, VMEM_LIMIT),
        name="ada_mod",
    )(cond, ada_w, ada_b.reshape(depth, 1, N_MOD * D))


def _pre_kernel(x_ref, mod_ref, g_ref, w_ref, cw_ref, cb_ref, csm_ref, p_ref, fl_ref, fc_ref, ut_ref):
    j = pl.program_id(1)
    x = x_ref[...]
    hn = _rms(x, g_ref[...]) * (1.0 + mod_ref[1:2, :]) + mod_ref[0:1, :]
    p = _dot(hn.astype(BF16), w_ref[...])
    p_ref[:, :C_XBC] = p[:, :C_XBC]
    p_ref[:, C_MISC:] = p[:, C_MISC:]
    ut_ref[...] = p[:, C_S5:C_S5 + DG].T.astype(BF16)

    xbc = p[:, C_XBC:C_MISC]
    period = jnp.where(j == 0, TM, GRID_W)
    pos = lax.broadcasted_iota(I32, (TM, 1), 0) & (period - 1)
    acc = jnp.zeros_like(xbc) + cb_ref[...]
    for k in range(SSD_K):
        d = k - SSD_K // 2
        sh = xbc if d == 0 else pltpu.roll(xbc, (TM - d) % TM, axis=0)
        ok = (pos + d >= 0) & (pos + d < period)
        acc = acc + jnp.where(ok, sh, 0.0) * cw_ref[k:k + 1, :]
    p_ref[:, C_XBC:C_MISC] = _silu(acc)

    xcs = _dot(p[:, C_FOUR:C_FOUR + DG].astype(BF16), csm_ref[...])

    @pl.when(j == 0)
    def _():
        fc_ref[0] = xcs[:, :DG].astype(BF16)
        fc_ref[1] = xcs[:, DG:].astype(BF16)

    @pl.when(j > 0)
    def _():
        fl_ref[0] = xcs[:, :DG].astype(BF16)
        fl_ref[1] = xcs[:, DG:].astype(BF16)


def _pre(x_all, mod, g, w_in_p, conv_w, conv_b, csm, seq, ctx):
    b, s, _ = x_all.shape
    nt = s // TM
    return pl.pallas_call(
        _pre_kernel,
        grid=(b, nt),
        in_specs=[pl.BlockSpec((None, TM, D), lambda i, j: (i, j, 0)),
                  pl.BlockSpec((None, None, 8, D), lambda i, j: (i, jnp.minimum(j, 1), 0, 0)),
                  pl.BlockSpec((1, D), lambda i, j: (0, 0)),
                  pl.BlockSpec((D, D_P), lambda i, j: (0, 0)),
                  pl.BlockSpec((8, 512), lambda i, j: (0, 0)),
                  pl.BlockSpec((1, 512), lambda i, j: (0, 0)),
                  pl.BlockSpec((DG, 2 * DG), lambda i, j: (0, 0))],
        out_specs=[pl.BlockSpec((None, TM, D_P), lambda i, j: (i, j, 0)),
                   pl.BlockSpec((None, 2, TM, DG), lambda i, j: (i, 0, jnp.maximum(j - 1, 0), 0)),
                   pl.BlockSpec((None, 2, ctx, DG), lambda i, j: (i, 0, 0, 0)),
                   pl.BlockSpec((DG, TM), lambda i, j: (0, i * nt + j))],
        out_shape=[jax.ShapeDtypeStruct((b, s, D_P), F32),
                   jax.ShapeDtypeStruct((b, 2, seq, DG), BF16),
                   jax.ShapeDtypeStruct((b, 2, ctx, DG), BF16),
                   jax.ShapeDtypeStruct((DG, b * s), BF16)],
        compiler_params=_cparams(2, VMEM_LIMIT),
        name="pre_mixer",
    )(x_all, mod, g, w_in_p, conv_w, conv_b, csm)


def _four_kernel(f_ref, x_ref, w_ref, o_ref):
    y = _dot(f_ref[...], x_ref[...])
    o_ref[...] = _dot(y.astype(BF16), w_ref[...])


def _fourier(fmat, x2, four_w):
    ln = fmat.shape[0]
    b = x2.shape[0]
    tf = min(512, ln)
    return pl.pallas_call(
        _four_kernel,
        grid=(ln // tf, b),
        in_specs=[pl.BlockSpec((tf, 2 * ln), lambda i, j: (i, 0)),
                  pl.BlockSpec((None, 2 * ln, DG), lambda i, j: (j, 0, 0)),
                  pl.BlockSpec((DG, DG), lambda i, j: (0, 0))],
        out_specs=pl.BlockSpec((None, tf, DG), lambda i, j: (j, i, 0)),
        out_shape=jax.ShapeDtypeStruct((b, ln, DG), F32),
        compiler_params=_cparams(2, VMEM_LIMIT),
        name="fourier_dft",
    )(fmat, x2, four_w)


def _dft_matrix(ln):
    f = jnp.arange(ln, dtype=I32)[:, None]
    t = jnp.arange(ln, dtype=I32)[None, :]
    ang = ((f * t) % ln).astype(F32) * (2.0 * math.pi / ln)
    scale = 1.0 / math.sqrt(ln * 64.0)
    return (jnp.concatenate([jnp.cos(ang), jnp.sin(ang)], axis=1) * scale).astype(BF16)


def _channel_dft():
    i = np.arange(DG)
    same = (i[:, None] // 64) == (i[None, :] // 64)
    ang = 2.0 * np.pi * ((i[:, None] % 64) * (i[None, :] % 64) % 64) / 64.0
    c = np.where(same, np.cos(ang), 0.0)
    s = np.where(same, np.sin(ang), 0.0)
    return jnp.asarray(np.concatenate([c, -s], axis=1), dtype=BF16)


def _s5_kernel(u_ref, kf_ref, kb_ref, p_ref, q_ref, a_ref, y_ref, toep, hlf, hlb, hef, heb,
               *, nb, nch, ncc):
    t_i = lax.broadcasted_iota(I32, (S5_T, S5_T), 0)
    j_i = lax.broadcasted_iota(I32, (S5_T, S5_T), 1)

    def build(h, c):
        for hp in range(S5_CH):
            vf = jnp.broadcast_to(kf_ref[pl.ds(h * S5_CH + hp, 1), :], (S5_T, S5_T))
            vb = jnp.broadcast_to(kb_ref[pl.ds(h * S5_CH + hp, 1), :], (S5_T, S5_T))
            blk = (jnp.where(j_i >= t_i, pltpu.roll(vf, 0, 1, stride=1, stride_axis=0), 0.0)
                   + jnp.where(j_i <= t_i, pltpu.roll(vb, 0, 1, stride=1, stride_axis=0), 0.0))
            toep[pl.ds(pl.multiple_of(h * S5_T, S5_T), S5_T), hp * S5_T:(hp + 1) * S5_T] = blk.astype(BF16)
        return c

    lax.fori_loop(0, S5_CH, build, 0)

    u = jnp.concatenate([u_ref[h] for h in range(S5_CH)], axis=1)
    hl = _dot(u, p_ref[...])
    hlf[...] = hl[:, :LANES]
    hlb[...] = hl[:, LANES:]
    a1f, a2f = a_ref[0:1, :LANES], a_ref[1:2, :LANES]
    a1b, a2b = a_ref[0:1, LANES:], a_ref[1:2, LANES:]

    def step(i, carry):
        sf, sb = carry
        nbk = jnp.where(i < ncc, ncc - 1 - i, nch + ncc - 1 - i)
        rows_f = pl.ds(i, nb, stride=nch)
        rows_b = pl.ds(nbk, nb, stride=nch)
        hef[rows_f, :] = sf
        heb[rows_b, :] = sb
        sf = a1f * sf + a2f * pltpu.roll(sf, 64, axis=1) + hlf[rows_f, :]
        sb = a1b * sb + a2b * pltpu.roll(sb, 64, axis=1) + hlb[rows_b, :]
        return sf, sb

    z = jnp.zeros((nb, LANES), F32)
    lax.fori_loop(0, nch, step, (z, z))
    he = jnp.concatenate([hef[...], heb[...]], axis=1).astype(BF16)
    y = _dot(u, toep[...]) + _dot(he, q_ref[...])
    for hp in range(S5_CH):
        y_ref[hp] = y[:, hp * S5_T:(hp + 1) * S5_T]


def _s5(ut3, kf, kb, ptab, qtab, atab, nb, nch, ncc):
    r = ut3.shape[1]
    w = S5_CH * S5_T
    grp = lambda rows, cols: pl.BlockSpec((None, rows, cols), lambda i: (i, 0, 0))
    return pl.pallas_call(
        functools.partial(_s5_kernel, nb=nb, nch=nch, ncc=ncc),
        grid=(S5_G,),
        in_specs=[pl.BlockSpec((S5_CH, r, S5_T), lambda i: (i, 0, 0)),
                  grp(S5_CH * S5_CH, S5_T), grp(S5_CH * S5_CH, S5_T),
                  grp(w, 2 * LANES), grp(2 * LANES, w), grp(8, 2 * LANES)],
        out_specs=pl.BlockSpec((S5_CH, r, S5_T), lambda i: (i, 0, 0)),
        out_shape=jax.ShapeDtypeStruct((DG, r, S5_T), F32),
        scratch_shapes=[pltpu.VMEM((w, w), BF16)] + [pltpu.VMEM((r, LANES), F32)] * 4,
        compiler_params=_cparams(1, VMEM_LIMIT),
        name="s5_mixer",
    )(ut3, kf, kb, ptab, qtab, atab)


def _s5_tables(lam_re, lam_im, log_dt, b_re, b_im, c_re, c_im):
    t = S5_T
    dt = jnp.exp(log_dt)[..., None]
    ldr, ldi = lam_re * dt, lam_im * dt
    jj = jnp.arange(t + 1, dtype=F32)[None, None, :, None]
    mag = jnp.exp(ldr[:, :, None, :] * jj)
    pr, pi = mag * jnp.cos(ldi[:, :, None, :] * jj), mag * jnp.sin(ldi[:, :, None, :] * jj)
    ar, ai = pr[:, :, 1], pi[:, :, 1]
    den = lam_re * lam_re + lam_im * lam_im
    qr = ((ar - 1.0) * lam_re + ai * lam_im) / den
    qi = (ai * lam_re - (ar - 1.0) * lam_im) / den
    bbr = qr[..., None] * b_re - qi[..., None] * b_im
    bbi = qr[..., None] * b_im + qi[..., None] * b_re
    cpr = c_re[:, :, None] * pr[:, :, :, None, :] - c_im[:, :, None] * pi[:, :, :, None, :]
    cpi = c_re[:, :, None] * pi[:, :, :, None, :] + c_im[:, :, None] * pr[:, :, :, None, :]
    klag = (jnp.einsum('rgjop,rgph->rgjoh', cpr[:, :, :t], bbr, precision=HI)
            - jnp.einsum('rgjop,rgph->rgjoh', cpi[:, :, :t], bbi, precision=HI))
    kf = jnp.transpose(klag[0], (0, 3, 2, 1)).reshape(S5_G, S5_CH * S5_CH, t)
    kb = jnp.transpose(klag[1], (0, 3, 2, 1))[..., (t - np.arange(t)) % t].reshape(S5_G, S5_CH * S5_CH, t)
    pbr = pr[..., None] * bbr[:, :, None] - pi[..., None] * bbi[:, :, None]
    pbi = pr[..., None] * bbi[:, :, None] + pi[..., None] * bbr[:, :, None]
    ar_t = np.arange(t)

    def ptab(r, exps):
        re = jnp.transpose(pbr[r][:, exps], (0, 3, 1, 2))
        im = jnp.transpose(pbi[r][:, exps], (0, 3, 1, 2))
        return jnp.concatenate([re, im], axis=-1).reshape(S5_G, S5_CH * t, 2 * S5_P)

    p2 = jnp.concatenate([ptab(0, t - 1 - ar_t), ptab(1, ar_t)], axis=-1)

    def qtab(r, exps):
        re = jnp.transpose(cpr[r][:, exps], (0, 3, 2, 1)).reshape(S5_G, S5_P, S5_CH * t)
        im = jnp.transpose(cpi[r][:, exps], (0, 3, 2, 1)).reshape(S5_G, S5_P, S5_CH * t)
        return jnp.concatenate([re, -im], axis=1)

    q2 = jnp.concatenate([qtab(0, ar_t + 1), qtab(1, t - ar_t)], axis=1)
    art, ait = pr[:, :, t], pi[:, :, t]
    a1 = jnp.concatenate([art[0], art[0], art[1], art[1]], axis=-1)
    a2 = jnp.concatenate([-ait[0], ait[0], -ait[1], ait[1]], axis=-1)
    atab = jnp.concatenate([a1[:, None], a2[:, None], jnp.zeros((S5_G, 6, 4 * S5_P), F32)], axis=1)
    return kf, kb, p2.astype(BF16), q2.astype(BF16), atab


def _split3(x):
    x1 = x.astype(BF16)
    r1 = x - x1.astype(F32)
    x2 = r1.astype(BF16)
    x3 = (r1 - x2.astype(F32)).astype(BF16)
    return x1, x2, x3


def _dot01(m, x):
    x1, x2, x3 = _split3(x)
    return (_dot(m, x3) + _dot(m, x2)) + _dot(m, x1)


def _tile_cumsum_mask(rev):
    t = lax.broadcasted_iota(I32, (TM, TM), 0)
    s = lax.broadcasted_iota(I32, (TM, TM), 1)
    order = (s >= t) if rev else (s <= t)
    return (order & ((t // CHUNK) == (s // CHUNK))).astype(BF16)


def _head_lane_mask(rev, width):
    t = lax.broadcasted_iota(I32, (CHUNK, width), 0)
    s = lax.broadcasted_iota(I32, (CHUNK, width), 1) % CHUNK
    return (s >= t) if rev else (s <= t)


def _gla_kernel(qk_ref, v_ref, misc_ref, a2_ref, ab_ref, o_ref, st_ref, *, rev):
    j = pl.program_id(1)

    @pl.when(j == 0)
    def _():
        st_ref[...] = jnp.zeros_like(st_ref)

    lane_k = lax.broadcasted_iota(I32, (1, LANES), 1) // GLA_DK
    lane_v = lax.broadcasted_iota(I32, (1, DG), 1) // GLA_DV
    causal = _head_lane_mask(rev, DG)
    st_diag = (lax.broadcasted_iota(I32, (DG, 1), 0) // GLA_DV) == lane_k

    z = _dot(misc_ref[...].astype(BF16), a2_ref[...]) + ab_ref[...]
    la_t = -(jnp.maximum(-z, 0.0) + jnp.log1p(jnp.exp(-jnp.abs(z)))) * (1.0 / GLA_TAU)
    b_t = _dot01(_tile_cumsum_mask(rev), la_t)
    st = st_ref[...]
    chunks = range(TM // CHUNK)
    for c in (reversed(chunks) if rev else chunks):
        rows = slice(c * CHUNK, (c + 1) * CHUNK)
        la, bcum = la_t[rows], b_t[rows]
        q = qk_ref[rows, :LANES] * (GLA_DK ** -0.5)
        k = qk_ref[rows, LANES:]
        v = v_ref[rows, :]
        tot = jnp.sum(la, axis=0, keepdims=True)
        q_in = (q * jnp.exp(bcum)).astype(BF16)
        k_in = k * jnp.exp(-bcum)
        k_dec = (k * jnp.exp(tot - bcum)).astype(BF16)
        k4 = jnp.concatenate([jnp.where(lane_k == h, k_in, 0.0) for h in range(GLA_H)], axis=0)
        att = jnp.where(causal, _dot(q_in, k4.astype(BF16), _NT), 0.0)
        vb = v.astype(BF16)
        vbd = jnp.concatenate([jnp.where(lane_v == h, vb, jnp.zeros_like(vb)) for h in range(GLA_H)], axis=0)
        o = _dot(att.astype(BF16), vbd) + _dot(q_in, jnp.where(st_diag, st, 0.0).astype(BF16), _NT)
        o_ref[rows, :] = o
        st = st * jnp.exp(tot) + _dot(vb, k_dec, _TN)
    st_ref[...] = st


def _tile_order(rev, j, nt):
    return jnp.where(j == 0, 0, nt - j) if rev else j


def _gla(p, a2m, abm, rev):
    b, s, _ = p.shape
    nt = s // TM
    return pl.pallas_call(
        functools.partial(_gla_kernel, rev=rev),
        grid=(b, nt),
        in_specs=[pl.BlockSpec((None, TM, DG), lambda i, j: (i, _tile_order(rev, j, nt), C_QK // DG)),
                  pl.BlockSpec((None, TM, DG), lambda i, j: (i, _tile_order(rev, j, nt), C_V // DG)),
                  pl.BlockSpec((None, TM, LANES), lambda i, j: (i, _tile_order(rev, j, nt), C_MISC // LANES)),
                  pl.BlockSpec((LANES, LANES), lambda i, j: (0, 0)),
                  pl.BlockSpec((1, LANES), lambda i, j: (0, 0))],
        out_specs=pl.BlockSpec((None, TM, DG), lambda i, j: (i, _tile_order(rev, j, nt), 0)),
        out_shape=jax.ShapeDtypeStruct((b, s, DG), F32),
        scratch_shapes=[pltpu.VMEM((DG, LANES), F32)],
        compiler_params=_cparams(2, VMEM_LIMIT),
        name="gla_bwd" if rev else "gla_fwd",
    )(p, p, p, a2m, abm)


def _ssd_kernel(xbc_ref, misc_ref, dtb_ref, av_ref, y_ref, sb_ref, *, rev):
    j = pl.program_id(1)

    @pl.when(j == 0)
    def _():
        sb_ref[...] = jnp.zeros_like(sb_ref)

    lane = lax.broadcasted_iota(I32, (1, LANES), 1)
    lane_g = lane // SSD_N
    lane_h = lax.broadcasted_iota(I32, (1, DG), 1) // SSD_HD
    row_h = lax.broadcasted_iota(I32, (DG, 1), 0) // SSD_HD
    blockmask = row_h == lane_h
    causal = _head_lane_mask(rev, DG)
    diag = (lax.broadcasted_iota(I32, (CHUNK, DG), 0)
            == lax.broadcasted_iota(I32, (CHUNK, DG), 1) % CHUNK)

    def per_head(x):
        xr = pltpu.roll(x, SSD_N, axis=1)
        return jnp.concatenate([jnp.where(lane < SSD_N, x, xr), jnp.where(lane < SSD_N, xr, x)], axis=1)

    zm = misc_ref[...] + dtb_ref[...]
    sp = jnp.maximum(zm, 0.0) + jnp.log1p(jnp.exp(-jnp.abs(zm)))
    dt_t = jnp.zeros((TM, DG), F32)
    for h in range(SSD_H):
        col = jnp.sum(jnp.where(lane == MISC_DT + SSD_H * int(rev) + h, sp, 0.0), axis=1, keepdims=True)
        dt_t = jnp.where(lane_h == h, col, dt_t)
    da_t = dt_t * av_ref[...]
    cum_t = _dot01(_tile_cumsum_mask(rev), da_t)

    sb = sb_ref[...]
    chunks = range(TM // CHUNK)
    for c in (reversed(chunks) if rev else chunks):
        rows = slice(c * CHUNK, (c + 1) * CHUNK)
        da, cum = da_t[rows], cum_t[rows]
        bmc = xbc_ref[rows, DG:DG + LANES]
        cmc = xbc_ref[rows, DG + LANES:]
        xdt = (xbc_ref[rows, :DG] * dt_t[rows]).astype(BF16)
        tot = jnp.sum(da, axis=0, keepdims=True)
        cum_row = jnp.sum(jnp.where(diag, cum, 0.0), axis=0, keepdims=True)
        decay = jnp.where(causal, jnp.exp(jnp.minimum(cum - cum_row, 0.0)), 0.0)
        b4 = jnp.concatenate([jnp.where(lane_g == h // (SSD_H // SSD_G), bmc, 0.0) for h in range(SSD_H)],
                             axis=0)
        scores = _dot(cmc.astype(BF16), b4.astype(BF16), _NT) * decay
        xbd = jnp.concatenate([jnp.where(lane_h == h, xdt, jnp.zeros_like(xdt)) for h in range(SSD_H)],
                              axis=0)
        cdec = per_head(cmc) * jnp.exp(cum)
        y = _dot(scores.astype(BF16), xbd) + _dot(cdec.astype(BF16), sb.astype(BF16))
        y_ref[rows, :] = y
        bdec = per_head(bmc) * jnp.exp(tot - cum)
        upd = _dot(bdec.astype(BF16), xdt, _TN)
        sb = sb * jnp.exp(tot) + jnp.where(blockmask, upd, 0.0)
    sb_ref[...] = sb


def _ssd(p, dtb, av, rev):
    b, s, _ = p.shape
    nt = s // TM
    return pl.pallas_call(
        functools.partial(_ssd_kernel, rev=rev),
        grid=(b, nt),
        in_specs=[pl.BlockSpec((None, TM, 512), lambda i, j: (i, _tile_order(rev, j, nt), C_XBC // 512)),
                  pl.BlockSpec((None, TM, LANES), lambda i, j: (i, _tile_order(rev, j, nt), C_MISC // LANES)),
                  pl.BlockSpec((1, LANES), lambda i, j: (0, 0)),
                  pl.BlockSpec((1, DG), lambda i, j: (0, 0))],
        out_specs=pl.BlockSpec((None, TM, DG), lambda i, j: (i, _tile_order(rev, j, nt), 0)),
        out_shape=jax.ShapeDtypeStruct((b, s, DG), F32),
        scratch_shapes=[pltpu.VMEM((DG, DG), F32)],
        compiler_params=_cparams(2, VMEM_LIMIT),
        name="ssd_bwd" if rev else "ssd_fwd",
    )(p, p, dtb, av)


def _ssd_tables(a_log, dt_bias, r):
    dtb = jnp.zeros((1, LANES), F32).at[0, MISC_DT + SSD_H * r:MISC_DT + SSD_H * (r + 1)].set(dt_bias[r])
    av = jnp.repeat(-jnp.exp(a_log[r]), SSD_HD)[None, :]
    return dtb, av


def _gelu_tanh(x):
    return 0.5 * x * (1.0 + jnp.tanh(math.sqrt(2.0 / math.pi) * (x + 0.044715 * x * x * x)))


def _post_kernel(x_ref, mod_ref, us5_ref, r_ref, z_ref, xs_ref, fl_ref, fc_ref, s5y_ref,
                 gf_ref, gb_ref, sf_ref, sbk_ref, vec_ref, wglu_ref, avg_ref, wout_ref,
                 gffn_ref, rw_ref, rb_ref, tri_ref, upper_ref,
                 xo_ref, loc_ref, ri_ref, rg_ref, tab_ref):
    j = pl.program_id(1)
    s5_d, gla_g, ssd_d, ssd_g = vec_ref[0:1, :], vec_ref[1:2, :], vec_ref[2:3, :], vec_ref[3:4, :]
    four = jnp.where(j == 0, fc_ref[...], fl_ref[...])
    y5 = s5y_ref[...].T + s5_d * us5_ref[...]
    act = _gelu_tanh(y5)
    s5o = act * jax.nn.sigmoid(_dot(act.astype(BF16), wglu_ref[...]))
    og = gf_ref[...] + gb_ref[...]
    ms = _dot(og * og, avg_ref[...], precision=HI)
    glo = og * lax.rsqrt(ms + EPS) * gla_g * _silu(r_ref[...])
    ys = (sf_ref[...] + sbk_ref[...] + ssd_d * xs_ref[...]) * _silu(z_ref[...])
    sso = _rms(ys, ssd_g)
    mix = jnp.concatenate([four, s5o, glo, sso], axis=1).astype(BF16)
    x = x_ref[...] + mod_ref[2:3, :] * _dot(mix, wout_ref[...])
    xo_ref[...] = x
    h = _rms(x, gffn_ref[...]) * (1.0 + mod_ref[4:5, :]) + mod_ref[3:4, :]

    logits = _dot(h, rw_ref[...], precision=HI) + rb_ref[...]
    lane = lax.broadcasted_iota(I32, (TM, LANES), 1)
    vals, idxs = [], []
    for _k in range(TOP_K):
        mx = jnp.max(logits, axis=1, keepdims=True)
        ix = jnp.min(jnp.where(logits == mx, lane, LANES), axis=1, keepdims=True)
        vals.append(mx)
        idxs.append(ix)
        logits = jnp.where(lane == ix, -jnp.inf, logits)
    es = [jnp.exp(v - vals[0]) for v in vals]
    den = es[0] + es[1] + es[2] + es[3]
    onehots = [(lane == ix) for ix in idxs]
    oh = sum(o.astype(F32) for o in onehots)

    ahead = _dot(tri_ref[...], oh.astype(BF16))
    cnt = jnp.sum(oh, axis=0, keepdims=True)
    cpad = jnp.floor((cnt + (SEG - 1.0)) * (1.0 / SEG)) * SEG
    lstart = _dot(jnp.broadcast_to(cpad, (8, LANES)).astype(BF16), upper_ref[...])[0:1]
    slot_of = lstart + ahead
    lane_r = lax.broadcasted_iota(I32, (1, LMAX), 1)
    ri = jnp.zeros((TM, LANES), I32)
    rg = jnp.zeros((TM, LANES), F32)
    pt = jnp.zeros((TM, LMAX), F32)
    for k in range(TOP_K):
        lpos = jnp.sum(jnp.where(onehots[k], slot_of, 0.0), axis=1, keepdims=True).astype(I32)
        pt = pt + (lane_r == lpos).astype(F32)
        ri = jnp.where(lane == k, idxs[k], ri)
        ri = jnp.where(lane == TOP_K + k, lpos, ri)
        rg = jnp.where(lane == k, es[k] / den, rg)
    ri_ref[...] = ri
    rg_ref[...] = rg
    loc_ref[...] = _pack_bf16_pairs(_dot(pt.astype(BF16), h.astype(BF16), _TN))
    row = lax.broadcasted_iota(I32, (8, LANES), 0)
    tab_ref[...] = jnp.where(row == 0, cnt.astype(I32), jnp.where(row == 1, lstart.astype(I32), 0))


def _post(x_all, mod, p, four_l, four_c, s5y, gla_f, gla_b, ssd_f, ssd_b, vecs, wglu, avg, wout, gffn,
          rw, rb, tri, upper, seq, ctx):
    b, s, _ = x_all.shape
    nt = s // TM
    tile = lambda w, col: pl.BlockSpec((None, TM, w), lambda i, j: (i, j, col))
    full = lambda a: pl.BlockSpec(a.shape, lambda i, j: (0,) * a.ndim)
    per_tile = lambda r, w: pl.BlockSpec((None, None, r, w), lambda i, j: (i, j, 0, 0))
    return pl.pallas_call(
        _post_kernel,
        grid=(b, nt),
        in_specs=[tile(D, 0),
                  pl.BlockSpec((None, None, 8, D), lambda i, j: (i, jnp.minimum(j, 1), 0, 0)),
                  tile(DG, C_S5 // DG), tile(DG, C_R // DG), tile(DG, C_Z // DG), tile(DG, C_XBC // DG),
                  pl.BlockSpec((None, TM, DG), lambda i, j: (i, jnp.maximum(j - 1, 0), 0)),
                  pl.BlockSpec((None, ctx, DG), lambda i, j: (i, 0, 0)),
                  pl.BlockSpec((DG, TM), lambda i, j: (0, i * nt + j)),
                  tile(DG, 0), tile(DG, 0), tile(DG, 0), tile(DG, 0),
                  full(vecs), full(wglu), full(avg), full(wout), full(gffn), full(rw), full(rb), full(tri),
                  full(upper)],
        out_specs=[tile(D, 0), per_tile(LMAX, D // 2), tile(LANES, 0), tile(LANES, 0), per_tile(8, LANES)],
        out_shape=[jax.ShapeDtypeStruct((b, s, D), F32),
                   jax.ShapeDtypeStruct((b, nt, LMAX, D // 2), U32),
                   jax.ShapeDtypeStruct((b, s, LANES), I32),
                   jax.ShapeDtypeStruct((b, s, LANES), F32),
                   jax.ShapeDtypeStruct((b, nt, 8, LANES), I32)],
        compiler_params=_cparams(2, VMEM_LIMIT),
        name="post_mixer_router",
    )(x_all, mod, p, p, p, p, four_l, four_c, s5y, gla_f, gla_b, ssd_f, ssd_b,
      vecs, wglu, avg, wout, gffn, rw, rb, tri, upper)


T_NSEG, T_LSTART, T_GSTART = 0, 32, 64


def _segment_copies(tab_ref, make_copy, action):
    def per_expert(e, c):
        lstart = pl.multiple_of(tab_ref[0, 0, T_LSTART + e], SEG)
        gstart = pl.multiple_of(tab_ref[0, 0, T_GSTART + e], SEG)

        def per_piece(q, c2):
            action(make_copy(lstart + q * SEG, gstart + q * SEG))
            return c2
        return lax.fori_loop(0, tab_ref[0, 0, T_NSEG + e], per_piece, c)
    lax.fori_loop(0, N_EXP, per_expert, 0)


def _dispatch_kernel(tab_ref, loc_ref, buf_in_ref, buf_ref, sem):
    del buf_in_ref

    def make_copy(lrow, grow):
        return pltpu.make_async_copy(loc_ref.at[0, pl.ds(lrow, SEG)], buf_ref.at[pl.ds(grow, SEG)], sem)

    _segment_copies(tab_ref, make_copy, lambda cp: cp.start())
    _segment_copies(tab_ref, make_copy, lambda cp: cp.wait())


def _dispatch(tab3, loc, buf0):
    ntile = loc.shape[0]
    return pl.pallas_call(
        _dispatch_kernel,
        grid=(ntile,),
        in_specs=[pl.BlockSpec((1, 1, LANES), lambda i: (i, 0, 0), memory_space=pltpu.SMEM),
                  pl.BlockSpec((1, LMAX, D // 2), lambda i: (i, 0, 0)),
                  pl.BlockSpec(memory_space=pl.ANY)],
        out_specs=pl.BlockSpec(memory_space=pl.ANY),
        out_shape=jax.ShapeDtypeStruct(buf0.shape, U32),
        scratch_shapes=[pltpu.SemaphoreType.DMA(())],
        input_output_aliases={2: 0},
        compiler_params=_cparams(1, VMEM_LIMIT),
        name="moe_dispatch",
    )(tab3, loc, buf0)


def _expert_kernel(be_ref, nu_ref, x_ref, wgu_ref, wd_ref, bg_ref, bu_ref, bd_ref, perm_ref, y_ref,
                   wg_sc, wu_sc, wd_sc):
    i = pl.program_id(0)
    used = i < nu_ref[0]
    new_expert = (i == 0) | (be_ref[i] != be_ref[jnp.maximum(i - 1, 0)])

    @pl.when(used & new_expert)
    def _():
        for c in range(2 * D // DG):
            d = _dot(wgu_ref[:, c * DG:(c + 1) * DG].astype(BF16), perm_ref[...])
            wg_sc[:, c * LANES:(c + 1) * LANES] = d[:, :LANES].astype(BF16)
            wu_sc[:, c * LANES:(c + 1) * LANES] = d[:, LANES:].astype(BF16)
        wd_sc[...] = wd_ref[...].astype(BF16)

    @pl.when(used)
    def _():
        x = _unpack_bf16_pairs(x_ref[...]).astype(BF16)
        g = _dot(x, wg_sc[...]) + bg_ref[...]
        u = _dot(x, wu_sc[...]) + bu_ref[...]
        gate = jnp.minimum(g, LIMIT)
        up = jnp.clip(u, -LIMIT, LIMIT)
        act = (up + 1.0) * gate * jax.nn.sigmoid(ALPHA * gate)
        y = _dot(act.astype(BF16), wd_sc[...]) + bd_ref[...]
        y_ref[...] = _pack_bf16_pairs(y)

    @pl.when(jnp.logical_not(used))
    def _():
        y_ref[...] = jnp.zeros_like(y_ref)


def _deinterleave_perm():
    pm = np.zeros((DG, DG), np.float32)
    o = np.arange(LANES)
    pm[2 * o, o] = 1.0
    pm[2 * o + 1, LANES + o] = 1.0
    return jnp.asarray(pm, BF16)


def _experts(blk_e, n_used, buf, wgu, wd, bg, bu, bd):
    nblk = buf.shape[0] // BM
    row = lambda i, be, nu: (jnp.minimum(i, nu[0] - 1), 0)
    wsel = lambda i, be, nu: (be[jnp.minimum(i, nu[0] - 1)], 0, 0)
    grid_spec = pltpu.PrefetchScalarGridSpec(
        num_scalar_prefetch=2,
        grid=(nblk,),
        in_specs=[pl.BlockSpec((BM, D // 2), row),
                  pl.BlockSpec((None, D, 2 * D), wsel), pl.BlockSpec((None, D, D), wsel),
                  pl.BlockSpec((None, 1, D), wsel), pl.BlockSpec((None, 1, D), wsel),
                  pl.BlockSpec((None, 1, D), wsel),
                  pl.BlockSpec((DG, DG), lambda i, be, nu: (0, 0))],
        out_specs=pl.BlockSpec((BM, D // 2), lambda i, be, nu: (i, 0)),
        scratch_shapes=[pltpu.VMEM((D, D), BF16), pltpu.VMEM((D, D), BF16), pltpu.VMEM((D, D), BF16)],
    )
    return pl.pallas_call(
        _expert_kernel,
        grid_spec=grid_spec,
        out_shape=jax.ShapeDtypeStruct(buf.shape, U32),
        compiler_params=_cparams(1, VMEM_LIMIT),
        name="moe_experts",
    )(blk_e, n_used, buf, wgu, wd, bg, bu, bd, _deinterleave_perm())


def _combine_kernel(tab_ref, tabn_ref, x_ref, ri_ref, rg_ref, mod_ref, fg_ref, yb_ref, o_ref, yloc, sem,
                    *, final):
    i = pl.program_id(0)
    n = pl.num_programs(0)
    slot = i % 2

    def make_copy(sl):
        return lambda lrow, grow: pltpu.make_async_copy(
            yb_ref.at[pl.ds(grow, SEG)], yloc.at[sl, pl.ds(lrow, SEG)], sem.at[sl])

    @pl.when(i == 0)
    def _():
        yloc[...] = jnp.zeros_like(yloc)
        _segment_copies(tab_ref, make_copy(0), lambda cp: cp.start())

    @pl.when(i + 1 < n)
    def _():
        _segment_copies(tabn_ref, make_copy(1 - slot), lambda cp: cp.start())

    _segment_copies(tab_ref, make_copy(slot), lambda cp: cp.wait())

    lane_r = lax.broadcasted_iota(I32, (1, LMAX), 1)
    g = jnp.zeros((TM, LMAX), F32)
    for k in range(TOP_K):
        g = g + jnp.where(lane_r == ri_ref[:, TOP_K + k:TOP_K + k + 1], rg_ref[:, k:k + 1], 0.0)
    g_hi = g.astype(BF16)
    g_lo = (g - g_hi.astype(F32)).astype(BF16)
    y = _unpack_bf16_pairs(yloc[slot]).astype(BF16)
    f = _dot(g_hi, y) + _dot(g_lo, y)
    x = x_ref[...] + mod_ref[5:6, :] * f
    o_ref[...] = _rms(x, fg_ref[...]) if final else x


def _combine(tab3, x_flat, ri_flat, rg_flat, mod, fg, ybuf, nt, final):
    n = x_flat.shape[0]
    steps = n // TM
    tspec = lambda f: pl.BlockSpec((1, 1, LANES), f, memory_space=pltpu.SMEM)
    return pl.pallas_call(
        functools.partial(_combine_kernel, final=final),
        grid=(steps,),
        in_specs=[tspec(lambda i: (i, 0, 0)),
                  tspec(lambda i: (jnp.minimum(i + 1, steps - 1), 0, 0)),
                  pl.BlockSpec((TM, D), lambda i: (i, 0)),
                  pl.BlockSpec((TM, LANES), lambda i: (i, 0)),
                  pl.BlockSpec((TM, LANES), lambda i: (i, 0)),
                  pl.BlockSpec((None, None, 8, D), lambda i: (i // nt, jnp.minimum(i % nt, 1), 0, 0)),
                  pl.BlockSpec((1, D), lambda i: (0, 0)),
                  pl.BlockSpec(memory_space=pl.ANY)],
        out_specs=pl.BlockSpec((TM, D), lambda i: (i, 0)),
        out_shape=jax.ShapeDtypeStruct((n, D), F32),
        scratch_shapes=[pltpu.VMEM((2, LMAX, D // 2), U32), pltpu.SemaphoreType.DMA((2,))],
        compiler_params=_cparams(1, VMEM_LIMIT),
        name="moe_combine",
    )(tab3, tab3, x_flat, ri_flat, rg_flat, mod, fg, ybuf)


def _repack_w_in(w):
    four, s5, q, k, v, r, a1, z, xbc, dt = jnp.split(
        w, np.cumsum([256, 256, 128, 128, 256, 256, 32, 256, 512, 8])[:-1].tolist(), axis=-1)
    misc = jnp.concatenate([a1, dt, jnp.zeros((D, LANES - 40), w.dtype)], axis=-1)
    return jnp.concatenate([four, s5, q, k, v, r, z, xbc, misc], axis=-1).astype(BF16)


def kernel(x, c, ctx, c_ctx, ada_w, ada_b, norm_mix_g, norm_ffn_g, w_in, w_out, four_w, s5_lam_re, s5_lam_im, s5_log_dt, s5_b_re, s5_b_im, s5_c_re, s5_c_im, s5_d, s5_w_glu, gla_a2, gla_a_b, gla_norm_g, ssd_conv_w, ssd_conv_b, ssd_a_log, ssd_dt_bias, ssd_d, ssd_norm_g, router_w, router_b, exp_w_gu, exp_b_gu, exp_w_dn, exp_b_dn, final_g):
    bsz, seq, _ = x.shape
    clen = ctx.shape[1]
    depth = ada_w.shape[0]
    assert clen == TM and seq % TM == 0 and bsz == 8
    s = clen + seq
    nt = s // TM
    n_tok = bsz * s
    nch, ncc = s // S5_T, clen // S5_T

    cond = jnp.concatenate([c, c_ctx[None, :], jnp.zeros((16 - bsz - 1, D), F32)], axis=0)
    mods = _ada(cond, ada_w, ada_b).reshape(depth, 16, N_MOD, D)
    m_lat = mods[:, :bsz]
    m_ctx = jnp.broadcast_to(mods[:, bsz:bsz + 1], m_lat.shape)
    mods = jnp.stack([m_ctx, m_lat], axis=2)
    mods = jnp.pad(mods, ((0, 0), (0, 0), (0, 0), (0, 2), (0, 0)))

    x_all = jnp.concatenate([ctx, x], axis=1)
    csm = _channel_dft()
    f_lat, f_ctx = _dft_matrix(seq), _dft_matrix(clen)
    avg = jnp.asarray(np.kron(np.eye(GLA_H), np.full((GLA_DV, GLA_DV), 1.0 / GLA_DV)), F32)
    tri = jnp.asarray(np.tril(np.ones((TM, TM)), -1), BF16)
    upper = jnp.asarray(np.triu(np.ones((LANES, LANES)), 1), BF16)

    ntile = n_tok // TM
    nblk = -(-(n_tok * TOP_K + ntile * N_EXP * (SEG - 1) + N_EXP * (BM - 1)) // BM)
    n_rows = nblk * BM

    for l in range(depth):
        mod = mods[l]
        w_in_p = _repack_w_in(w_in[l])
        conv_w = jnp.pad(ssd_conv_w[l], ((0, 8 - SSD_K), (0, 0)))
        p, four_l, four_c, ut = _pre(x_all, mod, norm_mix_g[l][None], w_in_p, conv_w, ssd_conv_b[l][None],
                                     csm, seq, clen)
        fw = four_w[l].astype(BF16)
        fy_l = _fourier(f_lat, four_l.reshape(bsz, 2 * seq, DG), fw)
        fy_c = _fourier(f_ctx, four_c.reshape(bsz, 2 * clen, DG), fw)

        s5_tabs = _s5_tables(s5_lam_re[l], s5_lam_im[l], s5_log_dt[l], s5_b_re[l], s5_b_im[l],
                                    s5_c_re[l], s5_c_im[l])
        y5 = _s5(ut.reshape(DG, n_tok // S5_T, S5_T), *s5_tabs, bsz, nch, ncc).reshape(DG, n_tok)

        gla_o, ssd_y = [], []
        for r in range(2):
            a2m = jnp.zeros((LANES, LANES), F32).at[GLA_RANK * r:GLA_RANK * (r + 1)].set(gla_a2[l, r])
            gla_o.append(_gla(p, a2m.astype(BF16), gla_a_b[l, r][None, :], rev=bool(r)))
            ssd_y.append(_ssd(p, *_ssd_tables(ssd_a_log[l], ssd_dt_bias[l], r), rev=bool(r)))

        vecs = jnp.stack([s5_d[l], jnp.tile(gla_norm_g[l], GLA_H), jnp.repeat(ssd_d[l], SSD_HD),
                          ssd_norm_g[l]] + [jnp.zeros((DG,), F32)] * 4)
        rw = jnp.pad(router_w[l], ((0, 0), (0, LANES - N_EXP)))
        rb = jnp.concatenate([router_b[l], jnp.full((LANES - N_EXP,), -1e30, F32)])[None]
        x_mid, loc, ri, rg, tab = _post(x_all, mod, p, fy_l, fy_c, y5, gla_o[0], gla_o[1], ssd_y[0], ssd_y[1],
                                        vecs, s5_w_glu[l].astype(BF16), avg, w_out[l].astype(BF16),
                                        norm_ffn_g[l][None], rw, rb, tri, upper, seq, clen)

        tab = tab.reshape(ntile, 8, LANES)
        cnt, lstart = tab[:, 0, :N_EXP], tab[:, 1, :N_EXP]
        cpad = (cnt + SEG - 1) // SEG * SEG
        in_expert = jnp.cumsum(cpad, axis=0) - cpad
        region = (jnp.sum(cpad, axis=0) + BM - 1) // BM * BM
        pad_end = jnp.cumsum(region)
        gstart = (pad_end - region)[None, :] + in_expert
        tab3 = jnp.concatenate([cpad // SEG, lstart, gstart, jnp.zeros_like(cnt)], axis=1)
        tab3 = tab3.astype(I32).reshape(ntile, 1, LANES)
        blk_start = jnp.arange(nblk, dtype=I32) * BM
        blk_e = jnp.minimum(jnp.sum((pad_end[None, :] <= blk_start[:, None]).astype(I32), axis=1), N_EXP - 1)
        n_used = (pad_end[-1] // BM).astype(I32)[None]

        buf = _dispatch(tab3, loc.reshape(ntile, LMAX, D // 2), jnp.zeros((n_rows, D // 2), U32))
        bgu = exp_b_gu[l].reshape(N_EXP, 1, D, 2)
        ybuf = _experts(blk_e, n_used, buf, exp_w_gu[l], exp_w_dn[l], bgu[..., 0], bgu[..., 1],
                        exp_b_dn[l][:, None, :])
        x_all = _combine(tab3, x_mid.reshape(n_tok, D), ri.reshape(n_tok, LANES), rg.reshape(n_tok, LANES),
                         mod, final_g[None], ybuf, nt, final=(l == depth - 1)).reshape(bsz, s, D)

    return x_all[:, clen:, :]
```

```python
import functools
import math

import jax
import jax.numpy as jnp
import numpy as np
from jax import lax
from jax.experimental import pallas as pl
from jax.experimental.pallas import tpu as pltpu

F32 = jnp.float32
BF16 = jnp.bfloat16
I32 = jnp.int32
U32 = jnp.uint32
HI = lax.Precision.HIGHEST

D = 1024
DG = 256
N_MOD = 6
GRID_W = 64
EPS = 1e-6

S5_CH = 16
S5_G = DG // S5_CH
S5_P = 64
S5_T = 128

GLA_H = 4
GLA_DK = 32
GLA_DV = 64
GLA_RANK = 16
GLA_TAU = 16.0
CHUNK = 64

SSD_H = 4
SSD_HD = 64
SSD_G = 2
SSD_N = 64
SSD_K = 5

N_EXP = 32
TOP_K = 4
LIMIT = 7.0
ALPHA = 1.702

TM = 256
BM = 512
EXPERT_COLS = 256
SEG = 8
LMAX = 1280
LANES = 128

C_FOUR, C_S5, C_QK, C_V, C_R, C_Z, C_XBC, C_MISC = 0, 256, 512, 768, 1024, 1280, 1536, 2048
D_P = 2176
MISC_A1 = 0
MISC_DT = 32

VMEM_LIMIT = 56 * 1024 * 1024


def _cparams(n_axes, vmem=None):
    return pltpu.CompilerParams(dimension_semantics=("arbitrary",) * n_axes,
                                vmem_limit_bytes=vmem)


def _silu(x):
    return x * jax.nn.sigmoid(x)


def _rms(x, g):
    return x * lax.rsqrt(jnp.mean(x * x, axis=-1, keepdims=True) + EPS) * g


def _dot(a, b, dims=None, precision=None):
    if dims is None:
        dims = (((a.ndim - 1,), (0,)), ((), ()))
    return lax.dot_general(a, b, dims, precision=precision, preferred_element_type=F32)


_NT = (((1,), (1,)), ((), ()))
_TN = (((0,), (0,)), ((), ()))


def _pack_bf16_pairs(a):
    n = a.shape[1] // 2
    hi = pltpu.bitcast(a[:, :n].astype(BF16).astype(F32), U32)
    lo = pltpu.bitcast(a[:, n:].astype(BF16).astype(F32), U32)
    return hi | (lo >> 16)


def _unpack_bf16_pairs(w):
    hi = pltpu.bitcast(w & jnp.uint32(0xFFFF0000), F32)
    lo = pltpu.bitcast(w << 16, F32)
    return jnp.concatenate([hi, lo], axis=1)


def _ada_kernel(cond_ref, w_ref, b_ref, o_ref):
    c = _silu(cond_ref[...])
    o_ref[...] = _dot(c, w_ref[...], precision=HI) + b_ref[...]


def _ada(cond, ada_w, ada_b):
    depth = ada_w.shape[0]
    rows = cond.shape[0]
    tn = 1536
    return pl.pallas_call(
        _ada_kernel,
        grid=(depth, (N_MOD * D) // tn),
        in_specs=[pl.BlockSpec((rows, D), lambda l, n: (0, 0)),
                  pl.BlockSpec((None, D, tn), lambda l, n: (l, 0, n)),
                  pl.BlockSpec((None, 1, tn), lambda l, n: (l, 0, n))],
        out_specs=pl.BlockSpec((None, rows, tn), lambda l, n: (l, 0, n)),
        out_shape=jax.ShapeDtypeStruct((depth, rows, N_MOD * D), F32),
        compiler_params=_cparams(2, VMEM_LIMIT),
        name="ada_mod",
    )(cond, ada_w, ada_b.reshape(depth, 1, N_MOD * D))


def _pre_kernel(x_ref, mod_ref, g_ref, w_ref, cw_ref, cb_ref, csm_ref, p_ref, fl_ref, fc_ref, ut_ref):
    j = pl.program_id(1)
    x = x_ref[...]
    hn = _rms(x, g_ref[...]) * (1.0 + mod_ref[1:2, :]) + mod_ref[0:1, :]
    p = _dot(hn.astype(BF16), w_ref[...])
    p_ref[:, :C_XBC] = p[:, :C_XBC]
    p_ref[:, C_MISC:] = p[:, C_MISC:]
    ut_ref[...] = p[:, C_S5:C_S5 + DG].T.astype(BF16)

    xbc = p[:, C_XBC:C_MISC]
    period = jnp.where(j == 0, TM, GRID_W)
    pos = lax.broadcasted_iota(I32, (TM, 1), 0) & (period - 1)
    acc = jnp.zeros_like(xbc) + cb_ref[...]
    for k in range(SSD_K):
        d = k - SSD_K // 2
        sh = xbc if d == 0 else pltpu.roll(xbc, (TM - d) % TM, axis=0)
        ok = (pos + d >= 0) & (pos + d < period)
        acc = acc + jnp.where(ok, sh, 0.0) * cw_ref[k:k + 1, :]
    p_ref[:, C_XBC:C_MISC] = _silu(acc)

    xcs = _dot(p[:, C_FOUR:C_FOUR + DG].astype(BF16), csm_ref[...])

    @pl.when(j == 0)
    def _():
        fc_ref[0] = xcs[:, :DG].astype(BF16)
        fc_ref[1] = xcs[:, DG:].astype(BF16)

    @pl.when(j > 0)
    def _():
        fl_ref[0] = xcs[:, :DG].astype(BF16)
        fl_ref[1] = xcs[:, DG:].astype(BF16)


def _pre(x_all, mod, g, w_in_p, conv_w, conv_b, csm, seq, ctx):
    b, s, _ = x_all.shape
    nt = s // TM
    return pl.pallas_call(
        _pre_kernel,
        grid=(b, nt),
        in_specs=[pl.BlockSpec((None, TM, D), lambda i, j: (i, j, 0)),
                  pl.BlockSpec((None, None, 8, D), lambda i, j: (i, jnp.minimum(j, 1), 0, 0)),
                  pl.BlockSpec((1, D), lambda i, j: (0, 0)),
                  pl.BlockSpec((D, D_P), lambda i, j: (0, 0)),
                  pl.BlockSpec((8, 512), lambda i, j: (0, 0)),
                  pl.BlockSpec((1, 512), lambda i, j: (0, 0)),
                  pl.BlockSpec((DG, 2 * DG), lambda i, j: (0, 0))],
        out_specs=[pl.BlockSpec((None, TM, D_P), lambda i, j: (i, j, 0)),
                   pl.BlockSpec((None, 2, TM, DG), lambda i, j: (i, 0, jnp.maximum(j - 1, 0), 0)),
                   pl.BlockSpec((None, 2, ctx, DG), lambda i, j: (i, 0, 0, 0)),
                   pl.BlockSpec((DG, TM), lambda i, j: (0, i * nt + j))],
        out_shape=[jax.ShapeDtypeStruct((b, s, D_P), F32),
                   jax.ShapeDtypeStruct((b, 2, seq, DG), BF16),
                   jax.ShapeDtypeStruct((b, 2, ctx, DG), BF16),
                   jax.ShapeDtypeStruct((DG, b * s), BF16)],
        compiler_params=_cparams(2, VMEM_LIMIT),
        name="pre_mixer",
    )(x_all, mod, g, w_in_p, conv_w, conv_b, csm)


def _four_kernel(f_ref, x_ref, w_ref, o_ref):
    y = _dot(f_ref[...], x_ref[...])
    o_ref[...] = _dot(y.astype(BF16), w_ref[...])


def _fourier(fmat, x2, four_w):
    ln = fmat.shape[0]
    b = x2.shape[0]
    tf = min(512, ln)
    return pl.pallas_call(
        _four_kernel,
        grid=(ln // tf, b),
        in_specs=[pl.BlockSpec((tf, 2 * ln), lambda i, j: (i, 0)),
                  pl.BlockSpec((None, 2 * ln, DG), lambda i, j: (j, 0, 0)),
                  pl.BlockSpec((DG, DG), lambda i, j: (0, 0))],
        out_specs=pl.BlockSpec((None, tf, DG), lambda i, j: (j, i, 0)),
        out_shape=jax.ShapeDtypeStruct((b, ln, DG), F32),
        compiler_params=_cparams(2, VMEM_LIMIT),
        name="fourier_dft",
    )(fmat, x2, four_w)


def _dft_matrix(ln):
    f = jnp.arange(ln, dtype=I32)[:, None]
    t = jnp.arange(ln, dtype=I32)[None, :]
    ang = ((f * t) % ln).astype(F32) * (2.0 * math.pi / ln)
    scale = 1.0 / math.sqrt(ln * 64.0)
    return (jnp.concatenate([jnp.cos(ang), jnp.sin(ang)], axis=1) * scale).astype(BF16)


def _channel_dft():
    i = np.arange(DG)
    same = (i[:, None] // 64) == (i[None, :] // 64)
    ang = 2.0 * np.pi * ((i[:, None] % 64) * (i[None, :] % 64) % 64) / 64.0
    c = np.where(same, np.cos(ang), 0.0)
    s = np.where(same, np.sin(ang), 0.0)
    return jnp.asarray(np.concatenate([c, -s], axis=1), dtype=BF16)


def _s5_kernel(u_ref, kf_ref, kb_ref, p_ref, q_ref, a_ref, y_ref, toep, hlf, hlb, hef, heb,
               *, nb, nch, ncc):
    t_i = lax.broadcasted_iota(I32, (S5_T, S5_T), 0)
    j_i = lax.broadcasted_iota(I32, (S5_T, S5_T), 1)

    def build(h, c):
        for hp in range(S5_CH):
            vf = jnp.broadcast_to(kf_ref[pl.ds(h * S5_CH + hp, 1), :], (S5_T, S5_T))
            vb = jnp.broadcast_to(kb_ref[pl.ds(h * S5_CH + hp, 1), :], (S5_T, S5_T))
            blk = (jnp.where(j_i >= t_i, pltpu.roll(vf, 0, 1, stride=1, stride_axis=0), 0.0)
                   + jnp.where(j_i <= t_i, pltpu.roll(vb, 0, 1, stride=1, stride_axis=0), 0.0))
            toep[pl.ds(pl.multiple_of(h * S5_T, S5_T), S5_T), hp * S5_T:(hp + 1) * S5_T] = blk.astype(BF16)
        return c

    lax.fori_loop(0, S5_CH, build, 0)

    u = jnp.concatenate([u_ref[h] for h in range(S5_CH)], axis=1)
    hl = _dot(u, p_ref[...])
    hlf[...] = hl[:, :LANES]
    hlb[...] = hl[:, LANES:]
    a1f, a2f = a_ref[0:1, :LANES], a_ref[1:2, :LANES]
    a1b, a2b = a_ref[0:1, LANES:], a_ref[1:2, LANES:]

    def step(i, carry):
        sf, sb = carry
        nbk = jnp.where(i < ncc, ncc - 1 - i, nch + ncc - 1 - i)
        rows_f = pl.ds(i, nb, stride=nch)
        rows_b = pl.ds(nbk, nb, stride=nch)
        hef[rows_f, :] = sf
        heb[rows_b, :] = sb
        sf = a1f * sf + a2f * pltpu.roll(sf, 64, axis=1) + hlf[rows_f, :]
        sb = a1b * sb + a2b * pltpu.roll(sb, 64, axis=1) + hlb[rows_b, :]
        return sf, sb

    z = jnp.zeros((nb, LANES), F32)
    lax.fori_loop(0, nch, step, (z, z))
    he = jnp.concatenate([hef[...], heb[...]], axis=1).astype(BF16)
    y = _dot(u, toep[...]) + _dot(he, q_ref[...])
    for hp in range(S5_CH):
        y_ref[hp] = y[:, hp * S5_T:(hp + 1) * S5_T]


def _s5(ut3, kf, kb, ptab, qtab, atab, nb, nch, ncc):
    r = ut3.shape[1]
    w = S5_CH * S5_T
    grp = lambda rows, cols: pl.BlockSpec((None, rows, cols), lambda i: (i, 0, 0))
    return pl.pallas_call(
        functools.partial(_s5_kernel, nb=nb, nch=nch, ncc=ncc),
        grid=(S5_G,),
        in_specs=[pl.BlockSpec((S5_CH, r, S5_T), lambda i: (i, 0, 0)),
                  grp(S5_CH * S5_CH, S5_T), grp(S5_CH * S5_CH, S5_T),
                  grp(w, 2 * LANES), grp(2 * LANES, w), grp(8, 2 * LANES)],
        out_specs=pl.BlockSpec((S5_CH, r, S5_T), lambda i: (i, 0, 0)),
        out_shape=jax.ShapeDtypeStruct((DG, r, S5_T), F32),
        scratch_shapes=[pltpu.VMEM((w, w), BF16)] + [pltpu.VMEM((r, LANES), F32)] * 4,
        compiler_params=_cparams(1, VMEM_LIMIT),
        name="s5_mixer",
    )(ut3, kf, kb, ptab, qtab, atab)


def _s5_tables(lam_re, lam_im, log_dt, b_re, b_im, c_re, c_im):
    t = S5_T
    dt = jnp.exp(log_dt)[..., None]
    ldr, ldi = lam_re * dt, lam_im * dt
    jj = jnp.arange(t + 1, dtype=F32)[None, None, :, None]
    mag = jnp.exp(ldr[:, :, None, :] * jj)
    pr, pi = mag * jnp.cos(ldi[:, :, None, :] * jj), mag * jnp.sin(ldi[:, :, None, :] * jj)
    ar, ai = pr[:, :, 1], pi[:, :, 1]
    den = lam_re * lam_re + lam_im * lam_im
    qr = ((ar - 1.0) * lam_re + ai * lam_im) / den
    qi = (ai * lam_re - (ar - 1.0) * lam_im) / den
    bbr = qr[..., None] * b_re - qi[..., None] * b_im
    bbi = qr[..., None] * b_im + qi[..., None] * b_re
    cpr = c_re[:, :, None] * pr[:, :, :, None, :] - c_im[:, :, None] * pi[:, :, :, None, :]
    cpi = c_re[:, :, None] * pi[:, :, :, None, :] + c_im[:, :, None] * pr[:, :, :, None, :]
    klag = (jnp.einsum('rgjop,rgph->rgjoh', cpr[:, :, :t], bbr, precision=HI)
            - jnp.einsum('rgjop,rgph->rgjoh', cpi[:, :, :t], bbi, precision=HI))
    kf = jnp.transpose(klag[0], (0, 3, 2, 1)).reshape(S5_G, S5_CH * S5_CH, t)
    kb = jnp.transpose(klag[1], (0, 3, 2, 1))[..., (t - np.arange(t)) % t].reshape(S5_G, S5_CH * S5_CH, t)
    pbr = pr[..., None] * bbr[:, :, None] - pi[..., None] * bbi[:, :, None]
    pbi = pr[..., None] * bbi[:, :, None] + pi[..., None] * bbr[:, :, None]
    ar_t = np.arange(t)

    def ptab(r, exps):
        re = jnp.transpose(pbr[r][:, exps], (0, 3, 1, 2))
        im = jnp.transpose(pbi[r][:, exps], (0, 3, 1, 2))
        return jnp.concatenate([re, im], axis=-1).reshape(S5_G, S5_CH * t, 2 * S5_P)

    p2 = jnp.concatenate([ptab(0, t - 1 - ar_t), ptab(1, ar_t)], axis=-1)

    def qtab(r, exps):
        re = jnp.transpose(cpr[r][:, exps], (0, 3, 2, 1)).reshape(S5_G, S5_P, S5_CH * t)
        im = jnp.transpose(cpi[r][:, exps], (0, 3, 2, 1)).reshape(S5_G, S5_P, S5_CH * t)
        return jnp.concatenate([re, -im], axis=1)

    q2 = jnp.concatenate([qtab(0, ar_t + 1), qtab(1, t - ar_t)], axis=1)
    art, ait = pr[:, :, t], pi[:, :, t]
    a1 = jnp.concatenate([art[0], art[0], art[1], art[1]], axis=-1)
    a2 = jnp.concatenate([-ait[0], ait[0], -ait[1], ait[1]], axis=-1)
    atab = jnp.concatenate([a1[:, None], a2[:, None], jnp.zeros((S5_G, 6, 4 * S5_P), F32)], axis=1)
    return kf, kb, p2.astype(BF16), q2.astype(BF16), atab


def _split3(x):
    x1 = x.astype(BF16)
    r1 = x - x1.astype(F32)
    x2 = r1.astype(BF16)
    x3 = (r1 - x2.astype(F32)).astype(BF16)
    return x1, x2, x3


def _dot01(m, x):
    x1, x2, x3 = _split3(x)
    return (_dot(m, x3) + _dot(m, x2)) + _dot(m, x1)


def _tile_cumsum_mask(rev):
    t = lax.broadcasted_iota(I32, (TM, TM), 0)
    s = lax.broadcasted_iota(I32, (TM, TM), 1)
    order = (s >= t) if rev else (s <= t)
    return (order & ((t // CHUNK) == (s // CHUNK))).astype(BF16)


def _head_lane_mask(rev, width):
    t = lax.broadcasted_iota(I32, (CHUNK, width), 0)
    s = lax.broadcasted_iota(I32, (CHUNK, width), 1) % CHUNK
    return (s >= t) if rev else (s <= t)


def _gla_kernel(qk_ref, v_ref, misc_ref, a2_ref, ab_ref, o_ref, st_ref, *, rev):
    j = pl.program_id(1)

    @pl.when(j == 0)
    def _():
        st_ref[...] = jnp.zeros_like(st_ref)

    lane_k = lax.broadcasted_iota(I32, (1, LANES), 1) // GLA_DK
    lane_v = lax.broadcasted_iota(I32, (1, DG), 1) // GLA_DV
    causal = _head_lane_mask(rev, DG)
    st_diag = (lax.broadcasted_iota(I32, (DG, 1), 0) // GLA_DV) == lane_k

    z = _dot(misc_ref[...].astype(BF16), a2_ref[...]) + ab_ref[...]
    la_t = -(jnp.maximum(-z, 0.0) + jnp.log1p(jnp.exp(-jnp.abs(z)))) * (1.0 / GLA_TAU)
    b_t = _dot01(_tile_cumsum_mask(rev), la_t)
    st = st_ref[...]
    chunks = range(TM // CHUNK)
    for c in (reversed(chunks) if rev else chunks):
        rows = slice(c * CHUNK, (c + 1) * CHUNK)
        la, bcum = la_t[rows], b_t[rows]
        q = qk_ref[rows, :LANES] * (GLA_DK ** -0.5)
        k = qk_ref[rows, LANES:]
        v = v_ref[rows, :]
        tot = jnp.sum(la, axis=0, keepdims=True)
        q_in = (q * jnp.exp(bcum)).astype(BF16)
        k_in = k * jnp.exp(-bcum)
        k_dec = (k * jnp.exp(tot - bcum)).astype(BF16)
        k4 = jnp.concatenate([jnp.where(lane_k == h, k_in, 0.0) for h in range(GLA_H)], axis=0)
        att = jnp.where(causal, _dot(q_in, k4.astype(BF16), _NT), 0.0)
        vb = v.astype(BF16)
        vbd = jnp.concatenate([jnp.where(lane_v == h, vb, jnp.zeros_like(vb)) for h in range(GLA_H)], axis=0)
        o = _dot(att.astype(BF16), vbd) + _dot(q_in, jnp.where(st_diag, st, 0.0).astype(BF16), _NT)
        o_ref[rows, :] = o
        st = st * jnp.exp(tot) + _dot(vb, k_dec, _TN)
    st_ref[...] = st


def _tile_order(rev, j, nt):
    return jnp.where(j == 0, 0, nt - j) if rev else j


def _gla(p, a2m, abm, rev):
    b, s, _ = p.shape
    nt = s // TM
    return pl.pallas_call(
        functools.partial(_gla_kernel, rev=rev),
        grid=(b, nt),
        in_specs=[pl.BlockSpec((None, TM, DG), lambda i, j: (i, _tile_order(rev, j, nt), C_QK // DG)),
                  pl.BlockSpec((None, TM, DG), lambda i, j: (i, _tile_order(rev, j, nt), C_V // DG)),
                  pl.BlockSpec((None, TM, LANES), lambda i, j: (i, _tile_order(rev, j, nt), C_MISC // LANES)),
                  pl.BlockSpec((LANES, LANES), lambda i, j: (0, 0)),
                  pl.BlockSpec((1, LANES), lambda i, j: (0, 0))],
        out_specs=pl.BlockSpec((None, TM, DG), lambda i, j: (i, _tile_order(rev, j, nt), 0)),
        out_shape=jax.ShapeDtypeStruct((b, s, DG), F32),
        scratch_shapes=[pltpu.VMEM((DG, LANES), F32)],
        compiler_params=_cparams(2, VMEM_LIMIT),
        name="gla_bwd" if rev else "gla_fwd",
    )(p, p, p, a2m, abm)


def _ssd_kernel(xbc_ref, misc_ref, dtb_ref, av_ref, y_ref, sb_ref, *, rev):
    j = pl.program_id(1)

    @pl.when(j == 0)
    def _():
        sb_ref[...] = jnp.zeros_like(sb_ref)

    lane = lax.broadcasted_iota(I32, (1, LANES), 1)
    lane_g = lane // SSD_N
    lane_h = lax.broadcasted_iota(I32, (1, DG), 1) // SSD_HD
    row_h = lax.broadcasted_iota(I32, (DG, 1), 0) // SSD_HD
    blockmask = row_h == lane_h
    causal = _head_lane_mask(rev, DG)
    diag = (lax.broadcasted_iota(I32, (CHUNK, DG), 0)
            == lax.broadcasted_iota(I32, (CHUNK, DG), 1) % CHUNK)

    def per_head(x):
        xr = pltpu.roll(x, SSD_N, axis=1)
        return jnp.concatenate([jnp.where(lane < SSD_N, x, xr), jnp.where(lane < SSD_N, xr, x)], axis=1)

    zm = misc_ref[...] + dtb_ref[...]
    sp = jnp.maximum(zm, 0.0) + jnp.log1p(jnp.exp(-jnp.abs(zm)))
    dt_t = jnp.zeros((TM, DG), F32)
    for h in range(SSD_H):
        col = jnp.sum(jnp.where(lane == MISC_DT + SSD_H * int(rev) + h, sp, 0.0), axis=1, keepdims=True)
        dt_t = jnp.where(lane_h == h, col, dt_t)
    da_t = dt_t * av_ref[...]
    cum_t = _dot01(_tile_cumsum_mask(rev), da_t)

    sb = sb_ref[...]
    chunks = range(TM // CHUNK)
    for c in (reversed(chunks) if rev else chunks):
        rows = slice(c * CHUNK, (c + 1) * CHUNK)
        da, cum = da_t[rows], cum_t[rows]
        bmc = xbc_ref[rows, DG:DG + LANES]
        cmc = xbc_ref[rows, DG + LANES:]
        xdt = (xbc_ref[rows, :DG] * dt_t[rows]).astype(BF16)
        tot = jnp.sum(da, axis=0, keepdims=True)
        cum_row = jnp.sum(jnp.where(diag, cum, 0.0), axis=0, keepdims=True)
        decay = jnp.where(causal, jnp.exp(jnp.minimum(cum - cum_row, 0.0)), 0.0)
        b4 = jnp.concatenate([jnp.where(lane_g == h // (SSD_H // SSD_G), bmc, 0.0) for h in range(SSD_H)],
                             axis=0)
        scores = _dot(cmc.astype(BF16), b4.astype(BF16), _NT) * decay
        xbd = jnp.concatenate([jnp.where(lane_h == h, xdt, jnp.zeros_like(xdt)) for h in range(SSD_H)],
                              axis=0)
        cdec = per_head(cmc) * jnp.exp(cum)
        y = _dot(scores.astype(BF16), xbd) + _dot(cdec.astype(BF16), sb.astype(BF16))
        y_ref[rows, :] = y
        bdec = per_head(bmc) * jnp.exp(tot - cum)
        upd = _dot(bdec.astype(BF16), xdt, _TN)
        sb = sb * jnp.exp(tot) + jnp.where(blockmask, upd, 0.0)
    sb_ref[...] = sb


def _ssd(p, dtb, av, rev):
    b, s, _ = p.shape
    nt = s // TM
    return pl.pallas_call(
        functools.partial(_ssd_kernel, rev=rev),
        grid=(b, nt),
        in_specs=[pl.BlockSpec((None, TM, 512), lambda i, j: (i, _tile_order(rev, j, nt), C_XBC // 512)),
                  pl.BlockSpec((None, TM, LANES), lambda i, j: (i, _tile_order(rev, j, nt), C_MISC // LANES)),
                  pl.BlockSpec((1, LANES), lambda i, j: (0, 0)),
                  pl.BlockSpec((1, DG), lambda i, j: (0, 0))],
        out_specs=pl.BlockSpec((None, TM, DG), lambda i, j: (i, _tile_order(rev, j, nt), 0)),
        out_shape=jax.ShapeDtypeStruct((b, s, DG), F32),
        scratch_shapes=[pltpu.VMEM((DG, DG), F32)],
        compiler_params=_cparams(2, VMEM_LIMIT),
        name="ssd_bwd" if rev else "ssd_fwd",
    )(p, p, dtb, av)


def _ssd_tables(a_log, dt_bias, r):
    dtb = jnp.zeros((1, LANES), F32).at[0, MISC_DT + SSD_H * r:MISC_DT + SSD_H * (r + 1)].set(dt_bias[r])
    av = jnp.repeat(-jnp.exp(a_log[r]), SSD_HD)[None, :]
    return dtb, av


def _gelu_tanh(x):
    return 0.5 * x * (1.0 + jnp.tanh(math.sqrt(2.0 / math.pi) * (x + 0.044715 * x * x * x)))


def _post_kernel(x_ref, mod_ref, us5_ref, r_ref, z_ref, xs_ref, fl_ref, fc_ref, s5y_ref,
                 gf_ref, gb_ref, sf_ref, sbk_ref, vec_ref, wglu_ref, avg_ref, wout_ref,
                 gffn_ref, rw_ref, rb_ref, tri_ref, upper_ref,
                 xo_ref, loc_ref, ri_ref, rg_ref, tab_ref):
    j = pl.program_id(1)
    s5_d, gla_g, ssd_d, ssd_g = vec_ref[0:1, :], vec_ref[1:2, :], vec_ref[2:3, :], vec_ref[3:4, :]
    four = jnp.where(j == 0, fc_ref[...], fl_ref[...])
    y5 = s5y_ref[...].T + s5_d * us5_ref[...]
    act = _gelu_tanh(y5)
    s5o = act * jax.nn.sigmoid(_dot(act.astype(BF16), wglu_ref[...]))
    og = gf_ref[...] + gb_ref[...]
    sq = og * og
    sq_hi = sq.astype(BF16)
    sq_lo = (sq - sq_hi.astype(F32)).astype(BF16)
    ms = _dot(sq_lo, avg_ref[...]) + _dot(sq_hi, avg_ref[...])
    glo = og * lax.rsqrt(ms + EPS) * gla_g * _silu(r_ref[...])
    ys = (sf_ref[...] + sbk_ref[...] + ssd_d * xs_ref[...]) * _silu(z_ref[...])
    sso = _rms(ys, ssd_g)
    mix = jnp.concatenate([four, s5o, glo, sso], axis=1).astype(BF16)
    x = x_ref[...] + mod_ref[2:3, :] * _dot(mix, wout_ref[...])
    xo_ref[...] = x
    h = _rms(x, gffn_ref[...]) * (1.0 + mod_ref[4:5, :]) + mod_ref[3:4, :]

    h_hi = h.astype(BF16)
    h_lo = (h - h_hi.astype(F32)).astype(BF16)
    logits = ((_dot(h_hi, rw_ref[1]) + _dot(h_lo, rw_ref[0])) + _dot(h_hi, rw_ref[0])
              + rb_ref[...])
    lane = lax.broadcasted_iota(I32, (TM, LANES), 1)
    vals, idxs = [], []
    for _k in range(TOP_K):
        mx = jnp.max(logits, axis=1, keepdims=True)
        ix = jnp.min(jnp.where(logits == mx, lane, LANES), axis=1, keepdims=True)
        vals.append(mx)
        idxs.append(ix)
        logits = jnp.where(lane == ix, -jnp.inf, logits)
    es = [jnp.exp(v - vals[0]) for v in vals]
    den = es[0] + es[1] + es[2] + es[3]
    onehots = [(lane == ix) for ix in idxs]
    oh = sum(o.astype(F32) for o in onehots)

    ahead = _dot(tri_ref[...], oh.astype(BF16))
    cnt = jnp.sum(oh, axis=0, keepdims=True)
    cpad = jnp.floor((cnt + (SEG - 1.0)) * (1.0 / SEG)) * SEG
    lstart = _dot(jnp.broadcast_to(cpad, (8, LANES)).astype(BF16), upper_ref[...])[0:1]
    slot_of = lstart + ahead
    lane_r = lax.broadcasted_iota(I32, (1, LMAX), 1)
    ri = jnp.zeros((TM, LANES), I32)
    rg = jnp.zeros((TM, LANES), F32)
    pt = jnp.zeros((TM, LMAX), F32)
    for k in range(TOP_K):
        lpos = jnp.sum(jnp.where(onehots[k], slot_of, 0.0), axis=1, keepdims=True).astype(I32)
        pt = pt + (lane_r == lpos).astype(F32)
        ri = jnp.where(lane == k, idxs[k], ri)
        ri = jnp.where(lane == TOP_K + k, lpos, ri)
        rg = jnp.where(lane == k, es[k] / den, rg)
    ri_ref[...] = ri
    rg_ref[...] = rg
    loc_ref[...] = _pack_bf16_pairs(_dot(pt.astype(BF16), h_hi, _TN))
    row = lax.broadcasted_iota(I32, (8, LANES), 0)
    tab_ref[...] = jnp.where(row == 0, cnt.astype(I32), jnp.where(row == 1, lstart.astype(I32), 0))


def _post(x_all, mod, p, four_l, four_c, s5y, gla_f, gla_b, ssd_f, ssd_b, vecs, wglu, avg, wout, gffn,
          rw, rb, tri, upper, seq, ctx):
    b, s, _ = x_all.shape
    nt = s // TM
    tile = lambda w, col: pl.BlockSpec((None, TM, w), lambda i, j: (i, j, col))
    full = lambda a: pl.BlockSpec(a.shape, lambda i, j: (0,) * a.ndim)
    per_tile = lambda r, w: pl.BlockSpec((None, None, r, w), lambda i, j: (i, j, 0, 0))
    return pl.pallas_call(
        _post_kernel,
        grid=(b, nt),
        in_specs=[tile(D, 0),
                  pl.BlockSpec((None, None, 8, D), lambda i, j: (i, jnp.minimum(j, 1), 0, 0)),
                  tile(DG, C_S5 // DG), tile(DG, C_R // DG), tile(DG, C_Z // DG), tile(DG, C_XBC // DG),
                  pl.BlockSpec((None, TM, DG), lambda i, j: (i, jnp.maximum(j - 1, 0), 0)),
                  pl.BlockSpec((None, ctx, DG), lambda i, j: (i, 0, 0)),
                  pl.BlockSpec((DG, TM), lambda i, j: (0, i * nt + j)),
                  tile(DG, 0), tile(DG, 0), tile(DG, 0), tile(DG, 0),
                  full(vecs), full(wglu), full(avg), full(wout), full(gffn), full(rw), full(rb), full(tri),
                  full(upper)],
        out_specs=[tile(D, 0), per_tile(LMAX, D // 2), tile(LANES, 0), tile(LANES, 0), per_tile(8, LANES)],
        out_shape=[jax.ShapeDtypeStruct((b, s, D), F32),
                   jax.ShapeDtypeStruct((b, nt, LMAX, D // 2), U32),
                   jax.ShapeDtypeStruct((b, s, LANES), I32),
                   jax.ShapeDtypeStruct((b, s, LANES), F32),
                   jax.ShapeDtypeStruct((b, nt, 8, LANES), I32)],
        compiler_params=_cparams(2, VMEM_LIMIT),
        name="post_mixer_router",
    )(x_all, mod, p, p, p, p, four_l, four_c, s5y, gla_f, gla_b, ssd_f, ssd_b,
      vecs, wglu, avg, wout, gffn, rw, rb, tri, upper)


T_NSEG, T_LSTART, T_GSTART = 0, 32, 64


def _segment_copies(tab_ref, make_copy, action):
    def per_expert(e, c):
        lstart = pl.multiple_of(tab_ref[0, 0, T_LSTART + e], SEG)
        gstart = pl.multiple_of(tab_ref[0, 0, T_GSTART + e], SEG)

        def per_piece(q, c2):
            action(make_copy(lstart + q * SEG, gstart + q * SEG))
            return c2
        return lax.fori_loop(0, tab_ref[0, 0, T_NSEG + e], per_piece, c)
    lax.fori_loop(0, N_EXP, per_expert, 0)


def _dispatch_kernel(tab_ref, loc_ref, buf_in_ref, buf_ref, sem):
    del buf_in_ref

    def make_copy(lrow, grow):
        return pltpu.make_async_copy(loc_ref.at[0, pl.ds(lrow, SEG)], buf_ref.at[pl.ds(grow, SEG)], sem)

    _segment_copies(tab_ref, make_copy, lambda cp: cp.start())
    _segment_copies(tab_ref, make_copy, lambda cp: cp.wait())


def _dispatch(tab3, loc, buf0):
    ntile = loc.shape[0]
    return pl.pallas_call(
        _dispatch_kernel,
        grid=(ntile,),
        in_specs=[pl.BlockSpec((1, 1, LANES), lambda i: (i, 0, 0), memory_space=pltpu.SMEM),
                  pl.BlockSpec((1, LMAX, D // 2), lambda i: (i, 0, 0)),
                  pl.BlockSpec(memory_space=pl.ANY)],
        out_specs=pl.BlockSpec(memory_space=pl.ANY),
        out_shape=jax.ShapeDtypeStruct(buf0.shape, U32),
        scratch_shapes=[pltpu.SemaphoreType.DMA(())],
        input_output_aliases={2: 0},
        compiler_params=_cparams(1, VMEM_LIMIT),
        name="moe_dispatch",
    )(tab3, loc, buf0)


def _expert_kernel(be_ref, nu_ref, x_ref, wgu_ref, wd_ref, bg_ref, bu_ref, bd_ref, perm_ref, y_ref,
                   wg_sc, wu_sc, wd_sc):
    i = pl.program_id(0)
    used = i < nu_ref[0]
    new_expert = (i == 0) | (be_ref[i] != be_ref[jnp.maximum(i - 1, 0)])

    @pl.when(used & new_expert)
    def _():
        for c in range(2 * D // DG):
            d = _dot(wgu_ref[:, c * DG:(c + 1) * DG].astype(BF16), perm_ref[...])
            wg_sc[:, c * LANES:(c + 1) * LANES] = d[:, :LANES].astype(BF16)
            wu_sc[:, c * LANES:(c + 1) * LANES] = d[:, LANES:].astype(BF16)
        wd_sc[...] = wd_ref[...].astype(BF16)

    @pl.when(used)
    def _():
        x = _unpack_bf16_pairs(x_ref[...]).astype(BF16)
        y = jnp.zeros((BM, D), F32) + bd_ref[...]
        for c in range(D // EXPERT_COLS):
            cols = slice(c * EXPERT_COLS, (c + 1) * EXPERT_COLS)
            g = _dot(x, wg_sc[:, cols]) + bg_ref[:, cols]
            u = _dot(x, wu_sc[:, cols]) + bu_ref[:, cols]
            gate = jnp.minimum(g, LIMIT)
            up = jnp.clip(u, -LIMIT, LIMIT)
            act = (up + 1.0) * gate * jax.nn.sigmoid(ALPHA * gate)
            y = y + _dot(act.astype(BF16), wd_sc[cols, :])
        y_ref[...] = _pack_bf16_pairs(y)

    @pl.when(jnp.logical_not(used))
    def _():
        y_ref[...] = jnp.zeros_like(y_ref)


def _deinterleave_perm():
    pm = np.zeros((DG, DG), np.float32)
    o = np.arange(LANES)
    pm[2 * o, o] = 1.0
    pm[2 * o + 1, LANES + o] = 1.0
    return jnp.asarray(pm, BF16)


def _experts(blk_e, n_used, buf, wgu_all, wd_all, layer, bg, bu, bd):
    nblk = buf.shape[0] // BM
    row = lambda i, be, nu: (jnp.minimum(i, nu[0] - 1), 0)
    wsel = lambda i, be, nu: (be[jnp.minimum(i, nu[0] - 1)], 0, 0)
    wsel_l = lambda i, be, nu: (layer, be[jnp.minimum(i, nu[0] - 1)], 0, 0)
    grid_spec = pltpu.PrefetchScalarGridSpec(
        num_scalar_prefetch=2,
        grid=(nblk,),
        in_specs=[pl.BlockSpec((BM, D // 2), row),
                  pl.BlockSpec((None, None, D, 2 * D), wsel_l), pl.BlockSpec((None, None, D, D), wsel_l),
                  pl.BlockSpec((None, 1, D), wsel), pl.BlockSpec((None, 1, D), wsel),
                  pl.BlockSpec((None, 1, D), wsel),
                  pl.BlockSpec((DG, DG), lambda i, be, nu: (0, 0))],
        out_specs=pl.BlockSpec((BM, D // 2), lambda i, be, nu: (i, 0)),
        scratch_shapes=[pltpu.VMEM((D, D), BF16), pltpu.VMEM((D, D), BF16), pltpu.VMEM((D, D), BF16)],
    )
    return pl.pallas_call(
        _expert_kernel,
        grid_spec=grid_spec,
        out_shape=jax.ShapeDtypeStruct(buf.shape, U32),
        compiler_params=_cparams(1, VMEM_LIMIT),
        name="moe_experts",
    )(blk_e, n_used, buf, wgu_all, wd_all, bg, bu, bd, _deinterleave_perm())


def _combine_kernel(tab_ref, tabn_ref, x_ref, ri_ref, rg_ref, mod_ref, fg_ref, yb_ref, o_ref, yloc, sem,
                    *, final):
    i = pl.program_id(0)
    n = pl.num_programs(0)
    slot = i % 2

    def make_copy(sl):
        return lambda lrow, grow: pltpu.make_async_copy(
            yb_ref.at[pl.ds(grow, SEG)], yloc.at[sl, pl.ds(lrow, SEG)], sem.at[sl])

    @pl.when(i == 0)
    def _():
        yloc[...] = jnp.zeros_like(yloc)
        _segment_copies(tab_ref, make_copy(0), lambda cp: cp.start())

    @pl.when(i + 1 < n)
    def _():
        _segment_copies(tabn_ref, make_copy(1 - slot), lambda cp: cp.start())

    _segment_copies(tab_ref, make_copy(slot), lambda cp: cp.wait())

    lane_r = lax.broadcasted_iota(I32, (1, LMAX), 1)
    g = jnp.zeros((TM, LMAX), F32)
    for k in range(TOP_K):
        g = g + jnp.where(lane_r == ri_ref[:, TOP_K + k:TOP_K + k + 1], rg_ref[:, k:k + 1], 0.0)
    g_hi = g.astype(BF16)
    g_lo = (g - g_hi.astype(F32)).astype(BF16)
    y = _unpack_bf16_pairs(yloc[slot]).astype(BF16)
    f = _dot(g_hi, y) + _dot(g_lo, y)
    x = x_ref[...] + mod_ref[5:6, :] * f
    o_ref[...] = _rms(x, fg_ref[...]) if final else x


def _combine(tab3, x_flat, ri_flat, rg_flat, mod, fg, ybuf, nt, final):
    n = x_flat.shape[0]
    steps = n // TM
    tspec = lambda f: pl.BlockSpec((1, 1, LANES), f, memory_space=pltpu.SMEM)
    return pl.pallas_call(
        functools.partial(_combine_kernel, final=final),
        grid=(steps,),
        in_specs=[tspec(lambda i: (i, 0, 0)),
                  tspec(lambda i: (jnp.minimum(i + 1, steps - 1), 0, 0)),
                  pl.BlockSpec((TM, D), lambda i: (i, 0)),
                  pl.BlockSpec((TM, LANES), lambda i: (i, 0)),
                  pl.BlockSpec((TM, LANES), lambda i: (i, 0)),
                  pl.BlockSpec((None, None, 8, D), lambda i: (i // nt, jnp.minimum(i % nt, 1), 0, 0)),
                  pl.BlockSpec((1, D), lambda i: (0, 0)),
                  pl.BlockSpec(memory_space=pl.ANY)],
        out_specs=pl.BlockSpec((TM, D), lambda i: (i, 0)),
        out_shape=jax.ShapeDtypeStruct((n, D), F32),
        scratch_shapes=[pltpu.VMEM((2, LMAX, D // 2), U32), pltpu.SemaphoreType.DMA((2,))],
        compiler_params=_cparams(1, VMEM_LIMIT),
        name="moe_combine",
    )(tab3, tab3, x_flat, ri_flat, rg_flat, mod, fg, ybuf)


def _repack_w_in(w):
    four, s5, q, k, v, r, a1, z, xbc, dt = jnp.split(
        w, np.cumsum([256, 256, 128, 128, 256, 256, 32, 256, 512, 8])[:-1].tolist(), axis=-1)
    misc = jnp.concatenate([a1, dt, jnp.zeros((D, LANES - 40), w.dtype)], axis=-1)
    return jnp.concatenate([four, s5, q, k, v, r, z, xbc, misc], axis=-1).astype(BF16)


def kernel(x, c, ctx, c_ctx, ada_w, ada_b, norm_mix_g, norm_ffn_g, w_in, w_out, four_w, s5_lam_re, s5_lam_im, s5_log_dt, s5_b_re, s5_b_im, s5_c_re, s5_c_im, s5_d, s5_w_glu, gla_a2, gla_a_b, gla_norm_g, ssd_conv_w, ssd_conv_b, ssd_a_log, ssd_dt_bias, ssd_d, ssd_norm_g, router_w, router_b, exp_w_gu, exp_b_gu, exp_w_dn, exp_b_dn, final_g):
    bsz, seq, _ = x.shape
    clen = ctx.shape[1]
    depth = ada_w.shape[0]
    assert clen == TM and seq % TM == 0 and bsz == 8
    s = clen + seq
    nt = s // TM
    n_tok = bsz * s
    nch, ncc = s // S5_T, clen // S5_T

    cond = jnp.concatenate([c, c_ctx[None, :], jnp.zeros((16 - bsz - 1, D), F32)], axis=0)
    mods = _ada(cond, ada_w, ada_b).reshape(depth, 16, N_MOD, D)
    m_lat = mods[:, :bsz]
    m_ctx = jnp.broadcast_to(mods[:, bsz:bsz + 1], m_lat.shape)
    mods = jnp.stack([m_ctx, m_lat], axis=2)
    mods = jnp.pad(mods, ((0, 0), (0, 0), (0, 0), (0, 2), (0, 0)))

    x_all = jnp.concatenate([ctx, x], axis=1)
    csm = _channel_dft()
    f_lat, f_ctx = _dft_matrix(seq), _dft_matrix(clen)
    avg = jnp.asarray(np.kron(np.eye(GLA_H), np.full((GLA_DV, GLA_DV), 1.0 / GLA_DV)), BF16)
    tri = jnp.asarray(np.tril(np.ones((TM, TM)), -1), BF16)
    upper = jnp.asarray(np.triu(np.ones((LANES, LANES)), 1), BF16)

    ntile = n_tok // TM
    nblk = -(-(n_tok * TOP_K + ntile * N_EXP * (SEG - 1) + N_EXP * (BM - 1)) // BM)
    n_rows = nblk * BM

    for l in range(depth):
        mod = mods[l]
        w_in_p = _repack_w_in(w_in[l])
        conv_w = jnp.pad(ssd_conv_w[l], ((0, 8 - SSD_K), (0, 0)))
        p, four_l, four_c, ut = _pre(x_all, mod, norm_mix_g[l][None], w_in_p, conv_w, ssd_conv_b[l][None],
                                     csm, seq, clen)
        fw = four_w[l].astype(BF16)
        fy_l = _fourier(f_lat, four_l.reshape(bsz, 2 * seq, DG), fw)
        fy_c = _fourier(f_ctx, four_c.reshape(bsz, 2 * clen, DG), fw)

        s5_tabs = _s5_tables(s5_lam_re[l], s5_lam_im[l], s5_log_dt[l], s5_b_re[l], s5_b_im[l],
                                    s5_c_re[l], s5_c_im[l])
        y5 = _s5(ut.reshape(DG, n_tok // S5_T, S5_T), *s5_tabs, bsz, nch, ncc).reshape(DG, n_tok)

        gla_o, ssd_y = [], []
        for r in range(2):
            a2m = jnp.zeros((LANES, LANES), F32).at[GLA_RANK * r:GLA_RANK * (r + 1)].set(gla_a2[l, r])
            gla_o.append(_gla(p, a2m.astype(BF16), gla_a_b[l, r][None, :], rev=bool(r)))
            ssd_y.append(_ssd(p, *_ssd_tables(ssd_a_log[l], ssd_dt_bias[l], r), rev=bool(r)))

        vecs = jnp.stack([s5_d[l], jnp.tile(gla_norm_g[l], GLA_H), jnp.repeat(ssd_d[l], SSD_HD),
                          ssd_norm_g[l]] + [jnp.zeros((DG,), F32)] * 4)
        rw = jnp.pad(router_w[l], ((0, 0), (0, LANES - N_EXP)))
        rw_hi = rw.astype(BF16)
        rw = jnp.stack([rw_hi, (rw - rw_hi.astype(F32)).astype(BF16)])
        rb = jnp.concatenate([router_b[l], jnp.full((LANES - N_EXP,), -1e30, F32)])[None]
        x_mid, loc, ri, rg, tab = _post(x_all, mod, p, fy_l, fy_c, y5, gla_o[0], gla_o[1], ssd_y[0], ssd_y[1],
                                        vecs, s5_w_glu[l].astype(BF16), avg, w_out[l].astype(BF16),
                                        norm_ffn_g[l][None], rw, rb, tri, upper, seq, clen)

        tab = tab.reshape(ntile, 8, LANES)
        cnt, lstart = tab[:, 0, :N_EXP], tab[:, 1, :N_EXP]
        cpad = (cnt + SEG - 1) // SEG * SEG
        in_expert = jnp.cumsum(cpad, axis=0) - cpad
        region = (jnp.sum(cpad, axis=0) + BM - 1) // BM * BM
        pad_end = jnp.cumsum(region)
        gstart = (pad_end - region)[None, :] + in_expert
        tab3 = jnp.concatenate([cpad // SEG, lstart, gstart, jnp.zeros_like(cnt)], axis=1)
        tab3 = tab3.astype(I32).reshape(ntile, 1, LANES)
        blk_start = jnp.arange(nblk, dtype=I32) * BM
        blk_e = jnp.minimum(jnp.sum((pad_end[None, :] <= blk_start[:, None]).astype(I32), axis=1), N_EXP - 1)
        n_used = (pad_end[-1] // BM).astype(I32)[None]

        buf = _dispatch(tab3, loc.reshape(ntile, LMAX, D // 2), jnp.zeros((n_rows, D // 2), U32))
        bgu = exp_b_gu[l].reshape(N_EXP, 1, D, 2)
        ybuf = _experts(blk_e, n_used, buf, exp_w_gu, exp_w_dn, l, bgu[..., 0], bgu[..., 1],
                        exp_b_dn[l][:, None, :])
        x_all = _combine(tab3, x_mid.reshape(n_tok, D), ri.reshape(n_tok, LANES), rg.reshape(n_tok, LANES),
                         mod, final_g[None], ybuf, nt, final=(l == depth - 1)).reshape(bsz, s, D)

    return x_all[:, clen:, :]
```

```python
import functools
import math

import jax
import jax.numpy as jnp
import numpy as np
from jax import lax
from jax.experimental import pallas as pl
from jax.experimental.pallas import tpu as pltpu

F32 = jnp.float32
BF16 = jnp.bfloat16
I32 = jnp.int32
U32 = jnp.uint32
HI = lax.Precision.HIGHEST

D = 1024
DG = 256
N_MOD = 6
GRID_W = 64
EPS = 1e-6

S5_CH = 16
S5_G = DG // S5_CH
S5_P = 64
S5_T = 128

GLA_H = 4
GLA_DK = 32
GLA_DV = 64
GLA_RANK = 16
GLA_TAU = 16.0
CHUNK = 64

SSD_H = 4
SSD_HD = 64
SSD_G = 2
SSD_N = 64
SSD_K = 5

N_EXP = 32
TOP_K = 4
LIMIT = 7.0
ALPHA = 1.702

TM = 256
BM = 512
SEG = 8
LMAX = 1280
LANES = 128

C_FOUR, C_S5, C_QK, C_V, C_R, C_Z, C_XBC, C_MISC = 0, 256, 512, 768, 1024, 1280, 1536, 2048
D_P = 2176
MISC_A1 = 0
MISC_DT = 32

VMEM_LIMIT = 56 * 1024 * 1024


def _cparams(n_axes, vmem=None):
    return pltpu.CompilerParams(dimension_semantics=("arbitrary",) * n_axes,
                                vmem_limit_bytes=vmem)


def _silu(x):
    return x * jax.nn.sigmoid(x)


def _rms(x, g):
    return x * lax.rsqrt(jnp.mean(x * x, axis=-1, keepdims=True) + EPS) * g


def _dot(a, b, dims=None, precision=None):
    if dims is None:
        dims = (((a.ndim - 1,), (0,)), ((), ()))
    return lax.dot_general(a, b, dims, precision=precision, preferred_element_type=F32)


_NT = (((1,), (1,)), ((), ()))
_TN = (((0,), (0,)), ((), ()))


def _pack_bf16_pairs(a):
    n = a.shape[1] // 2
    hi = pltpu.bitcast(a[:, :n].astype(BF16).astype(F32), U32)
    lo = pltpu.bitcast(a[:, n:].astype(BF16).astype(F32), U32)
    return hi | (lo >> 16)


def _unpack_bf16_pairs(w):
    hi = pltpu.bitcast(w & jnp.uint32(0xFFFF0000), F32)
    lo = pltpu.bitcast(w << 16, F32)
    return jnp.concatenate([hi, lo], axis=1)


def _ada_kernel(cond_ref, w_ref, b_ref, o_ref):
    c = _silu(cond_ref[...])
    o_ref[...] = _dot(c, w_ref[...], precision=HI) + b_ref[...]


def _ada(cond, ada_w, ada_b):
    depth = ada_w.shape[0]
    rows = cond.shape[0]
    tn = 1536
    return pl.pallas_call(
        _ada_kernel,
        grid=(depth, (N_MOD * D) // tn),
        in_specs=[pl.BlockSpec((rows, D), lambda l, n: (0, 0)),
                  pl.BlockSpec((None, D, tn), lambda l, n: (l, 0, n)),
                  pl.BlockSpec((None, 1, tn), lambda l, n: (l, 0, n))],
        out_specs=pl.BlockSpec((None, rows, tn), lambda l, n: (l, 0, n)),
        out_shape=jax.ShapeDtypeStruct((depth, rows, N_MOD * D), F32),
        compiler_params=_cparams(2, VMEM_LIMIT),
        name="ada_mod",
    )(cond, ada_w, ada_b.reshape(depth, 1, N_MOD * D))


def _pre_kernel(x_ref, mod_ref, g_ref, w_ref, cw_ref, cb_ref, csm_ref, p_ref, fl_ref, fc_ref, ut_ref):
    j = pl.program_id(1)
    x = x_ref[...]
    hn = _rms(x, g_ref[...]) * (1.0 + mod_ref[1:2, :]) + mod_ref[0:1, :]
    p = _dot(hn.astype(BF16), w_ref[...])
    p_ref[:, :C_XBC] = p[:, :C_XBC]
    p_ref[:, C_MISC:] = p[:, C_MISC:]
    ut_ref[...] = p[:, C_S5:C_S5 + DG].T.astype(BF16)

    xbc = p[:, C_XBC:C_MISC]
    period = jnp.where(j == 0, TM, GRID_W)
    pos = lax.broadcasted_iota(I32, (TM, 1), 0) & (period - 1)
    acc = jnp.zeros_like(xbc) + cb_ref[...]
    for k in range(SSD_K):
        d = k - SSD_K // 2
        sh = xbc if d == 0 else pltpu.roll(xbc, (TM - d) % TM, axis=0)
        ok = (pos + d >= 0) & (pos + d < period)
        acc = acc + jnp.where(ok, sh, 0.0) * cw_ref[k:k + 1, :]
    p_ref[:, C_XBC:C_MISC] = _silu(acc)

    xcs = _dot(p[:, C_FOUR:C_FOUR + DG].astype(BF16), csm_ref[...])

    @pl.when(j == 0)
    def _():
        fc_ref[0] = xcs[:, :DG].astype(BF16)
        fc_ref[1] = xcs[:, DG:].astype(BF16)

    @pl.when(j > 0)
    def _():
        fl_ref[0] = xcs[:, :DG].astype(BF16)
        fl_ref[1] = xcs[:, DG:].astype(BF16)


def _pre(x_all, mod, g, w_in_p, conv_w, conv_b, csm, seq, ctx):
    b, s, _ = x_all.shape
    nt = s // TM
    return pl.pallas_call(
        _pre_kernel,
        grid=(b, nt),
        in_specs=[pl.BlockSpec((None, TM, D), lambda i, j: (i, j, 0)),
                  pl.BlockSpec((None, None, 8, D), lambda i, j: (i, jnp.minimum(j, 1), 0, 0)),
                  pl.BlockSpec((1, D), lambda i, j: (0, 0)),
                  pl.BlockSpec((D, D_P), lambda i, j: (0, 0)),
                  pl.BlockSpec((8, 512), lambda i, j: (0, 0)),
                  pl.BlockSpec((1, 512), lambda i, j: (0, 0)),
                  pl.BlockSpec((DG, 2 * DG), lambda i, j: (0, 0))],
        out_specs=[pl.BlockSpec((None, TM, D_P), lambda i, j: (i, j, 0)),
                   pl.BlockSpec((None, 2, TM, DG), lambda i, j: (i, 0, jnp.maximum(j - 1, 0), 0)),
                   pl.BlockSpec((None, 2, ctx, DG), lambda i, j: (i, 0, 0, 0)),
                   pl.BlockSpec((DG, TM), lambda i, j: (0, i * nt + j))],
        out_shape=[jax.ShapeDtypeStruct((b, s, D_P), F32),
                   jax.ShapeDtypeStruct((b, 2, seq, DG), BF16),
                   jax.ShapeDtypeStruct((b, 2, ctx, DG), BF16),
                   jax.ShapeDtypeStruct((DG, b * s), BF16)],
        compiler_params=_cparams(2, VMEM_LIMIT),
        name="pre_mixer",
    )(x_all, mod, g, w_in_p, conv_w, conv_b, csm)


def _four_kernel(f_ref, x_ref, w_ref, o_ref):
    y = _dot(f_ref[...], x_ref[...])
    o_ref[...] = _dot(y.astype(BF16), w_ref[...])


def _fourier(fmat, x2, four_w):
    ln = fmat.shape[0]
    b = x2.shape[0]
    tf = min(512, ln)
    return pl.pallas_call(
        _four_kernel,
        grid=(ln // tf, b),
        in_specs=[pl.BlockSpec((tf, 2 * ln), lambda i, j: (i, 0)),
                  pl.BlockSpec((None, 2 * ln, DG), lambda i, j: (j, 0, 0)),
                  pl.BlockSpec((DG, DG), lambda i, j: (0, 0))],
        out_specs=pl.BlockSpec((None, tf, DG), lambda i, j: (j, i, 0)),
        out_shape=jax.ShapeDtypeStruct((b, ln, DG), F32),
        compiler_params=_cparams(2, VMEM_LIMIT),
        name="fourier_dft",
    )(fmat, x2, four_w)


def _dft_matrix(ln):
    f = jnp.arange(ln, dtype=I32)[:, None]
    t = jnp.arange(ln, dtype=I32)[None, :]
    ang = ((f * t) % ln).astype(F32) * (2.0 * math.pi / ln)
    scale = 1.0 / math.sqrt(ln * 64.0)
    return (jnp.concatenate([jnp.cos(ang), jnp.sin(ang)], axis=1) * scale).astype(BF16)


def _channel_dft():
    i = np.arange(DG)
    same = (i[:, None] // 64) == (i[None, :] // 64)
    ang = 2.0 * np.pi * ((i[:, None] % 64) * (i[None, :] % 64) % 64) / 64.0
    c = np.where(same, np.cos(ang), 0.0)
    s = np.where(same, np.sin(ang), 0.0)
    return jnp.asarray(np.concatenate([c, -s], axis=1), dtype=BF16)


def _s5_kernel(u_ref, kf_ref, kb_ref, p_ref, q_ref, a_ref, y_ref, toep, hlf, hlb, hef, heb,
               *, nb, nch, ncc):
    t_i = lax.broadcasted_iota(I32, (S5_T, S5_T), 0)
    j_i = lax.broadcasted_iota(I32, (S5_T, S5_T), 1)

    def build(h, c):
        for hp in range(S5_CH):
            vf = jnp.broadcast_to(kf_ref[pl.ds(h * S5_CH + hp, 1), :], (S5_T, S5_T))
            vb = jnp.broadcast_to(kb_ref[pl.ds(h * S5_CH + hp, 1), :], (S5_T, S5_T))
            blk = (jnp.where(j_i >= t_i, pltpu.roll(vf, 0, 1, stride=1, stride_axis=0), 0.0)
                   + jnp.where(j_i <= t_i, pltpu.roll(vb, 0, 1, stride=1, stride_axis=0), 0.0))
            toep[pl.ds(pl.multiple_of(h * S5_T, S5_T), S5_T), hp * S5_T:(hp + 1) * S5_T] = blk.astype(BF16)
        return c

    lax.fori_loop(0, S5_CH, build, 0)

    u = jnp.concatenate([u_ref[h] for h in range(S5_CH)], axis=1)
    hl = _dot(u, p_ref[...])
    hlf[...] = hl[:, :LANES]
    hlb[...] = hl[:, LANES:]
    a1f, a2f = a_ref[0:1, :LANES], a_ref[1:2, :LANES]
    a1b, a2b = a_ref[0:1, LANES:], a_ref[1:2, LANES:]

    def step(i, carry):
        sf, sb = carry
        nbk = jnp.where(i < ncc, ncc - 1 - i, nch + ncc - 1 - i)
        rows_f = pl.ds(i, nb, stride=nch)
        rows_b = pl.ds(nbk, nb, stride=nch)
        hef[rows_f, :] = sf
        heb[rows_b, :] = sb
        sf = a1f * sf + a2f * pltpu.roll(sf, 64, axis=1) + hlf[rows_f, :]
        sb = a1b * sb + a2b * pltpu.roll(sb, 64, axis=1) + hlb[rows_b, :]
        return sf, sb

    z = jnp.zeros((nb, LANES), F32)
    lax.fori_loop(0, nch, step, (z, z))
    he = jnp.concatenate([hef[...], heb[...]], axis=1).astype(BF16)
    y = _dot(u, toep[...]) + _dot(he, q_ref[...])
    for hp in range(S5_CH):
        y_ref[hp] = y[:, hp * S5_T:(hp + 1) * S5_T]


def _s5(ut3, kf, kb, ptab, qtab, atab, nb, nch, ncc):
    r = ut3.shape[1]
    w = S5_CH * S5_T
    grp = lambda rows, cols: pl.BlockSpec((None, rows, cols), lambda i: (i, 0, 0))
    return pl.pallas_call(
        functools.partial(_s5_kernel, nb=nb, nch=nch, ncc=ncc),
        grid=(S5_G,),
        in_specs=[pl.BlockSpec((S5_CH, r, S5_T), lambda i: (i, 0, 0)),
                  grp(S5_CH * S5_CH, S5_T), grp(S5_CH * S5_CH, S5_T),
                  grp(w, 2 * LANES), grp(2 * LANES, w), grp(8, 2 * LANES)],
        out_specs=pl.BlockSpec((S5_CH, r, S5_T), lambda i: (i, 0, 0)),
        out_shape=jax.ShapeDtypeStruct((DG, r, S5_T), F32),
        scratch_shapes=[pltpu.VMEM((w, w), BF16)] + [pltpu.VMEM((r, LANES), F32)] * 4,
        compiler_params=_cparams(1, VMEM_LIMIT),
        name="s5_mixer",
    )(ut3, kf, kb, ptab, qtab, atab)


def _s5_tables(lam_re, lam_im, log_dt, b_re, b_im, c_re, c_im):
    t = S5_T
    dt = jnp.exp(log_dt)[..., None]
    ldr, ldi = lam_re * dt, lam_im * dt
    jj = jnp.arange(t + 1, dtype=F32)[None, None, :, None]
    mag = jnp.exp(ldr[:, :, None, :] * jj)
    pr, pi = mag * jnp.cos(ldi[:, :, None, :] * jj), mag * jnp.sin(ldi[:, :, None, :] * jj)
    ar, ai = pr[:, :, 1], pi[:, :, 1]
    den = lam_re * lam_re + lam_im * lam_im
    qr = ((ar - 1.0) * lam_re + ai * lam_im) / den
    qi = (ai * lam_re - (ar - 1.0) * lam_im) / den
    bbr = qr[..., None] * b_re - qi[..., None] * b_im
    bbi = qr[..., None] * b_im + qi[..., None] * b_re
    cpr = c_re[:, :, None] * pr[:, :, :, None, :] - c_im[:, :, None] * pi[:, :, :, None, :]
    cpi = c_re[:, :, None] * pi[:, :, :, None, :] + c_im[:, :, None] * pr[:, :, :, None, :]
    klag = (jnp.einsum('rgjop,rgph->rgjoh', cpr[:, :, :t], bbr, precision=HI)
            - jnp.einsum('rgjop,rgph->rgjoh', cpi[:, :, :t], bbi, precision=HI))
    kf = jnp.transpose(klag[0], (0, 3, 2, 1)).reshape(S5_G, S5_CH * S5_CH, t)
    kbt = jnp.transpose(klag[1], (0, 3, 2, 1))
    kb = jnp.concatenate([kbt[..., :1], jnp.flip(kbt[..., 1:], axis=-1)], axis=-1)
    kb = kb.reshape(S5_G, S5_CH * S5_CH, t)
    pbr = pr[..., None] * bbr[:, :, None] - pi[..., None] * bbi[:, :, None]
    pbi = pr[..., None] * bbi[:, :, None] + pi[..., None] * bbr[:, :, None]

    def ptab(re, im):
        both = jnp.concatenate([jnp.transpose(re, (0, 3, 1, 2)), jnp.transpose(im, (0, 3, 1, 2))], axis=-1)
        return both.reshape(S5_G, S5_CH * t, 2 * S5_P)

    p2 = jnp.concatenate([ptab(jnp.flip(pbr[0][:, :t], axis=1), jnp.flip(pbi[0][:, :t], axis=1)),
                          ptab(pbr[1][:, :t], pbi[1][:, :t])], axis=-1)

    def qtab(re, im):
        re = jnp.transpose(re, (0, 3, 2, 1)).reshape(S5_G, S5_P, S5_CH * t)
        im = jnp.transpose(im, (0, 3, 2, 1)).reshape(S5_G, S5_P, S5_CH * t)
        return jnp.concatenate([re, -im], axis=1)

    q2 = jnp.concatenate([qtab(cpr[0][:, 1:], cpi[0][:, 1:]),
                          qtab(jnp.flip(cpr[1][:, 1:], axis=1), jnp.flip(cpi[1][:, 1:], axis=1))],
                         axis=1)
    art, ait = pr[:, :, t], pi[:, :, t]
    a1 = jnp.concatenate([art[0], art[0], art[1], art[1]], axis=-1)
    a2 = jnp.concatenate([-ait[0], ait[0], -ait[1], ait[1]], axis=-1)
    atab = jnp.concatenate([a1[:, None], a2[:, None], jnp.zeros((S5_G, 6, 4 * S5_P), F32)], axis=1)
    return kf, kb, p2.astype(BF16), q2.astype(BF16), atab


def _split3(x):
    x1 = x.astype(BF16)
    r1 = x - x1.astype(F32)
    x2 = r1.astype(BF16)
    x3 = (r1 - x2.astype(F32)).astype(BF16)
    return x1, x2, x3


def _dot01(m, x):
    x1, x2, x3 = _split3(x)
    return (_dot(m, x3) + _dot(m, x2)) + _dot(m, x1)


def _tile_cumsum_mask(rev):
    t = lax.broadcasted_iota(I32, (TM, TM), 0)
    s = lax.broadcasted_iota(I32, (TM, TM), 1)
    order = (s >= t) if rev else (s <= t)
    return (order & ((t // CHUNK) == (s // CHUNK))).astype(BF16)


def _head_lane_mask(rev, width):
    t = lax.broadcasted_iota(I32, (CHUNK, width), 0)
    s = lax.broadcasted_iota(I32, (CHUNK, width), 1) % CHUNK
    return (s >= t) if rev else (s <= t)


def _gla_kernel(qk_ref, v_ref, misc_ref, a2_ref, ab_ref, o_ref, st_ref, *, rev):
    j = pl.program_id(1)

    @pl.when(j == 0)
    def _():
        st_ref[...] = jnp.zeros_like(st_ref)

    lane_k = lax.broadcasted_iota(I32, (1, LANES), 1) // GLA_DK
    lane_v = lax.broadcasted_iota(I32, (1, DG), 1) // GLA_DV
    causal = _head_lane_mask(rev, DG)
    st_diag = (lax.broadcasted_iota(I32, (DG, 1), 0) // GLA_DV) == lane_k

    z = _dot(misc_ref[...].astype(BF16), a2_ref[...]) + ab_ref[...]
    la_t = -(jnp.maximum(-z, 0.0) + jnp.log1p(jnp.exp(-jnp.abs(z)))) * (1.0 / GLA_TAU)
    b_t = _dot01(_tile_cumsum_mask(rev), la_t)
    st = st_ref[...]
    chunks = range(TM // CHUNK)
    for c in (reversed(chunks) if rev else chunks):
        rows = slice(c * CHUNK, (c + 1) * CHUNK)
        la, bcum = la_t[rows], b_t[rows]
        q = qk_ref[rows, :LANES] * (GLA_DK ** -0.5)
        k = qk_ref[rows, LANES:]
        v = v_ref[rows, :]
        tot = jnp.sum(la, axis=0, keepdims=True)
        q_in = (q * jnp.exp(bcum)).astype(BF16)
        k_in = k * jnp.exp(-bcum)
        k_dec = (k * jnp.exp(tot - bcum)).astype(BF16)
        k4 = jnp.concatenate([jnp.where(lane_k == h, k_in, 0.0) for h in range(GLA_H)], axis=0)
        att = jnp.where(causal, _dot(q_in, k4.astype(BF16), _NT), 0.0)
        vb = v.astype(BF16)
        vbd = jnp.concatenate([jnp.where(lane_v == h, vb, jnp.zeros_like(vb)) for h in range(GLA_H)], axis=0)
        o = _dot(att.astype(BF16), vbd) + _dot(q_in, jnp.where(st_diag, st, 0.0).astype(BF16), _NT)
        o_ref[rows, :] = o
        st = st * jnp.exp(tot) + _dot(vb, k_dec, _TN)
    st_ref[...] = st


def _tile_order(rev, j, nt):
    return jnp.where(j == 0, 0, nt - j) if rev else j


def _gla(p, a2m, abm, rev):
    b, s, _ = p.shape
    nt = s // TM
    return pl.pallas_call(
        functools.partial(_gla_kernel, rev=rev),
        grid=(b, nt),
        in_specs=[pl.BlockSpec((None, TM, DG), lambda i, j: (i, _tile_order(rev, j, nt), C_QK // DG)),
                  pl.BlockSpec((None, TM, DG), lambda i, j: (i, _tile_order(rev, j, nt), C_V // DG)),
                  pl.BlockSpec((None, TM, LANES), lambda i, j: (i, _tile_order(rev, j, nt), C_MISC // LANES)),
                  pl.BlockSpec((LANES, LANES), lambda i, j: (0, 0)),
                  pl.BlockSpec((1, LANES), lambda i, j: (0, 0))],
        out_specs=pl.BlockSpec((None, TM, DG), lambda i, j: (i, _tile_order(rev, j, nt), 0)),
        out_shape=jax.ShapeDtypeStruct((b, s, DG), F32),
        scratch_shapes=[pltpu.VMEM((DG, LANES), F32)],
        compiler_params=_cparams(2, VMEM_LIMIT),
        name="gla_bwd" if rev else "gla_fwd",
    )(p, p, p, a2m, abm)


def _ssd_kernel(xbc_ref, misc_ref, dtb_ref, av_ref, y_ref, sb_ref, *, rev):
    j = pl.program_id(1)

    @pl.when(j == 0)
    def _():
        sb_ref[...] = jnp.zeros_like(sb_ref)

    lane = lax.broadcasted_iota(I32, (1, LANES), 1)
    lane_g = lane // SSD_N
    lane_h = lax.broadcasted_iota(I32, (1, DG), 1) // SSD_HD
    row_h = lax.broadcasted_iota(I32, (DG, 1), 0) // SSD_HD
    blockmask = row_h == lane_h
    causal = _head_lane_mask(rev, DG)
    diag = (lax.broadcasted_iota(I32, (CHUNK, DG), 0)
            == lax.broadcasted_iota(I32, (CHUNK, DG), 1) % CHUNK)

    def per_head(x):
        xr = pltpu.roll(x, SSD_N, axis=1)
        return jnp.concatenate([jnp.where(lane < SSD_N, x, xr), jnp.where(lane < SSD_N, xr, x)], axis=1)

    zm = misc_ref[...] + dtb_ref[...]
    sp = jnp.maximum(zm, 0.0) + jnp.log1p(jnp.exp(-jnp.abs(zm)))
    dt_t = jnp.zeros((TM, DG), F32)
    for h in range(SSD_H):
        col = jnp.sum(jnp.where(lane == MISC_DT + SSD_H * int(rev) + h, sp, 0.0), axis=1, keepdims=True)
        dt_t = jnp.where(lane_h == h, col, dt_t)
    da_t = dt_t * av_ref[...]
    cum_t = _dot01(_tile_cumsum_mask(rev), da_t)

    sb = sb_ref[...]
    chunks = range(TM // CHUNK)
    for c in (reversed(chunks) if rev else chunks):
        rows = slice(c * CHUNK, (c + 1) * CHUNK)
        da, cum = da_t[rows], cum_t[rows]
        bmc = xbc_ref[rows, DG:DG + LANES]
        cmc = xbc_ref[rows, DG + LANES:]
        xdt = (xbc_ref[rows, :DG] * dt_t[rows]).astype(BF16)
        tot = jnp.sum(da, axis=0, keepdims=True)
        cum_row = jnp.sum(jnp.where(diag, cum, 0.0), axis=0, keepdims=True)
        decay = jnp.where(causal, jnp.exp(jnp.minimum(cum - cum_row, 0.0)), 0.0)
        b4 = jnp.concatenate([jnp.where(lane_g == h // (SSD_H // SSD_G), bmc, 0.0) for h in range(SSD_H)],
                             axis=0)
        scores = _dot(cmc.astype(BF16), b4.astype(BF16), _NT) * decay
        xbd = jnp.concatenate([jnp.where(lane_h == h, xdt, jnp.zeros_like(xdt)) for h in range(SSD_H)],
                              axis=0)
        cdec = per_head(cmc) * jnp.exp(cum)
        y = _dot(scores.astype(BF16), xbd) + _dot(cdec.astype(BF16), sb.astype(BF16))
        y_ref[rows, :] = y
        bdec = per_head(bmc) * jnp.exp(tot - cum)
        upd = _dot(bdec.astype(BF16), xdt, _TN)
        sb = sb * jnp.exp(tot) + jnp.where(blockmask, upd, 0.0)
    sb_ref[...] = sb


def _ssd(p, dtb, av, rev):
    b, s, _ = p.shape
    nt = s // TM
    return pl.pallas_call(
        functools.partial(_ssd_kernel, rev=rev),
        grid=(b, nt),
        in_specs=[pl.BlockSpec((None, TM, 512), lambda i, j: (i, _tile_order(rev, j, nt), C_XBC // 512)),
                  pl.BlockSpec((None, TM, LANES), lambda i, j: (i, _tile_order(rev, j, nt), C_MISC // LANES)),
                  pl.BlockSpec((1, LANES), lambda i, j: (0, 0)),
                  pl.BlockSpec((1, DG), lambda i, j: (0, 0))],
        out_specs=pl.BlockSpec((None, TM, DG), lambda i, j: (i, _tile_order(rev, j, nt), 0)),
        out_shape=jax.ShapeDtypeStruct((b, s, DG), F32),
        scratch_shapes=[pltpu.VMEM((DG, DG), F32)],
        compiler_params=_cparams(2, VMEM_LIMIT),
        name="ssd_bwd" if rev else "ssd_fwd",
    )(p, p, dtb, av)


def _ssd_tables(a_log, dt_bias, r):
    dtb = jnp.zeros((1, LANES), F32).at[0, MISC_DT + SSD_H * r:MISC_DT + SSD_H * (r + 1)].set(dt_bias[r])
    av = jnp.repeat(-jnp.exp(a_log[r]), SSD_HD)[None, :]
    return dtb, av


def _gelu_tanh(x):
    return 0.5 * x * (1.0 + jnp.tanh(math.sqrt(2.0 / math.pi) * (x + 0.044715 * x * x * x)))


def _post_kernel(x_ref, mod_ref, us5_ref, r_ref, z_ref, xs_ref, fl_ref, fc_ref, s5y_ref,
                 gf_ref, gb_ref, sf_ref, sbk_ref, vec_ref, wglu_ref, avg_ref, wout_ref,
                 gffn_ref, rw_ref, rb_ref, tri_ref, upper_ref,
                 xo_ref, loc_ref, ri_ref, rg_ref, tab_ref):
    j = pl.program_id(1)
    s5_d, gla_g, ssd_d, ssd_g = vec_ref[0:1, :], vec_ref[1:2, :], vec_ref[2:3, :], vec_ref[3:4, :]
    four = jnp.where(j == 0, fc_ref[...], fl_ref[...])
    y5 = s5y_ref[...].T + s5_d * us5_ref[...]
    act = _gelu_tanh(y5)
    s5o = act * jax.nn.sigmoid(_dot(act.astype(BF16), wglu_ref[...]))
    og = gf_ref[...] + gb_ref[...]
    sq = og * og
    sq_hi = sq.astype(BF16)
    sq_lo = (sq - sq_hi.astype(F32)).astype(BF16)
    ms = _dot(sq_lo, avg_ref[...]) + _dot(sq_hi, avg_ref[...])
    glo = og * lax.rsqrt(ms + EPS) * gla_g * _silu(r_ref[...])
    ys = (sf_ref[...] + sbk_ref[...] + ssd_d * xs_ref[...]) * _silu(z_ref[...])
    sso = _rms(ys, ssd_g)
    mix = jnp.concatenate([four, s5o, glo, sso], axis=1).astype(BF16)
    x = x_ref[...] + mod_ref[2:3, :] * _dot(mix, wout_ref[...])
    xo_ref[...] = x
    h = _rms(x, gffn_ref[...]) * (1.0 + mod_ref[4:5, :]) + mod_ref[3:4, :]

    h_hi = h.astype(BF16)
    h_lo = (h - h_hi.astype(F32)).astype(BF16)
    logits = ((_dot(h_hi, rw_ref[1]) + _dot(h_lo, rw_ref[0])) + _dot(h_hi, rw_ref[0])
              + rb_ref[...])
    lane = lax.broadcasted_iota(I32, (TM, LANES), 1)
    vals, idxs = [], []
    for _k in range(TOP_K):
        mx = jnp.max(logits, axis=1, keepdims=True)
        ix = jnp.min(jnp.where(logits == mx, lane, LANES), axis=1, keepdims=True)
        vals.append(mx)
        idxs.append(ix)
        logits = jnp.where(lane == ix, -jnp.inf, logits)
    es = [jnp.exp(v - vals[0]) for v in vals]
    den = es[0] + es[1] + es[2] + es[3]
    onehots = [(lane == ix) for ix in idxs]
    oh = sum(o.astype(F32) for o in onehots)

    ahead = _dot(tri_ref[...], oh.astype(BF16))
    cnt = jnp.sum(oh, axis=0, keepdims=True)
    cpad = jnp.floor((cnt + (SEG - 1.0)) * (1.0 / SEG)) * SEG
    lstart = _dot(jnp.broadcast_to(cpad, (8, LANES)).astype(BF16), upper_ref[...])[0:1]
    slot_of = lstart + ahead
    lane_r = lax.broadcasted_iota(I32, (1, LMAX), 1)
    ri = jnp.zeros((TM, LANES), I32)
    rg = jnp.zeros((TM, LANES), F32)
    pt = jnp.zeros((TM, LMAX), F32)
    for k in range(TOP_K):
        lpos = jnp.sum(jnp.where(onehots[k], slot_of, 0.0), axis=1, keepdims=True).astype(I32)
        pt = pt + (lane_r == lpos).astype(F32)
        ri = jnp.where(lane == k, idxs[k], ri)
        ri = jnp.where(lane == TOP_K + k, lpos, ri)
        rg = jnp.where(lane == k, es[k] / den, rg)
    ri_ref[...] = ri
    rg_ref[...] = rg
    loc_ref[...] = _pack_bf16_pairs(_dot(pt.astype(BF16), h_hi, _TN))
    row = lax.broadcasted_iota(I32, (8, LANES), 0)
    tab_ref[...] = jnp.where(row == 0, cnt.astype(I32), jnp.where(row == 1, lstart.astype(I32), 0))


def _post(x_all, mod, p, four_l, four_c, s5y, gla_f, gla_b, ssd_f, ssd_b, vecs, wglu, avg, wout, gffn,
          rw, rb, tri, upper, seq, ctx):
    b, s, _ = x_all.shape
    nt = s // TM
    tile = lambda w, col: pl.BlockSpec((None, TM, w), lambda i, j: (i, j, col))
    full = lambda a: pl.BlockSpec(a.shape, lambda i, j: (0,) * a.ndim)
    per_tile = lambda r, w: pl.BlockSpec((None, None, r, w), lambda i, j: (i, j, 0, 0))
    return pl.pallas_call(
        _post_kernel,
        grid=(b, nt),
        in_specs=[tile(D, 0),
                  pl.BlockSpec((None, None, 8, D), lambda i, j: (i, jnp.minimum(j, 1), 0, 0)),
                  tile(DG, C_S5 // DG), tile(DG, C_R // DG), tile(DG, C_Z // DG), tile(DG, C_XBC // DG),
                  pl.BlockSpec((None, TM, DG), lambda i, j: (i, jnp.maximum(j - 1, 0), 0)),
                  pl.BlockSpec((None, ctx, DG), lambda i, j: (i, 0, 0)),
                  pl.BlockSpec((DG, TM), lambda i, j: (0, i * nt + j)),
                  tile(DG, 0), tile(DG, 0), tile(DG, 0), tile(DG, 0),
                  full(vecs), full(wglu), full(avg), full(wout), full(gffn), full(rw), full(rb), full(tri),
                  full(upper)],
        out_specs=[tile(D, 0), per_tile(LMAX, D // 2), tile(LANES, 0), tile(LANES, 0), per_tile(8, LANES)],
        out_shape=[jax.ShapeDtypeStruct((b, s, D), F32),
                   jax.ShapeDtypeStruct((b, nt, LMAX, D // 2), U32),
                   jax.ShapeDtypeStruct((b, s, LANES), I32),
                   jax.ShapeDtypeStruct((b, s, LANES), F32),
                   jax.ShapeDtypeStruct((b, nt, 8, LANES), I32)],
        compiler_params=_cparams(2, VMEM_LIMIT),
        name="post_mixer_router",
    )(x_all, mod, p, p, p, p, four_l, four_c, s5y, gla_f, gla_b, ssd_f, ssd_b,
      vecs, wglu, avg, wout, gffn, rw, rb, tri, upper)


T_NSEG, T_LSTART, T_GSTART, T_NBIG = 0, 32, 64, 96
BIG = 32


def _segment_copies(tab_ref, make_copy, action):
    def per_expert(e, c):
        lstart = pl.multiple_of(tab_ref[0, 0, T_LSTART + e], SEG)
        gstart = pl.multiple_of(tab_ref[0, 0, T_GSTART + e], SEG)
        nbig = tab_ref[0, 0, T_NBIG + e]

        def big_piece(q, c2):
            action(make_copy(lstart + q * BIG, gstart + q * BIG, BIG))
            return c2

        def seg_piece(q, c2):
            action(make_copy(lstart + nbig * BIG + q * SEG, gstart + nbig * BIG + q * SEG, SEG))
            return c2
        c = lax.fori_loop(0, nbig, big_piece, c)
        return lax.fori_loop(0, tab_ref[0, 0, T_NSEG + e], seg_piece, c)
    lax.fori_loop(0, N_EXP, per_expert, 0)


def _dispatch_kernel(tab_ref, loc_ref, buf_in_ref, buf_ref, sem):
    del buf_in_ref

    def make_copy(lrow, grow, rows):
        return pltpu.make_async_copy(loc_ref.at[0, pl.ds(lrow, rows)], buf_ref.at[pl.ds(grow, rows)], sem)

    _segment_copies(tab_ref, make_copy, lambda cp: cp.start())
    _segment_copies(tab_ref, make_copy, lambda cp: cp.wait())


def _dispatch(tab3, loc, buf0):
    ntile = loc.shape[0]
    return pl.pallas_call(
        _dispatch_kernel,
        grid=(ntile,),
        in_specs=[pl.BlockSpec((1, 1, LANES), lambda i: (i, 0, 0), memory_space=pltpu.SMEM),
                  pl.BlockSpec((1, LMAX, D // 2), lambda i: (i, 0, 0)),
                  pl.BlockSpec(memory_space=pl.ANY)],
        out_specs=pl.BlockSpec(memory_space=pl.ANY),
        out_shape=jax.ShapeDtypeStruct(buf0.shape, U32),
        scratch_shapes=[pltpu.SemaphoreType.DMA(())],
        input_output_aliases={2: 0},
        compiler_params=_cparams(1, VMEM_LIMIT),
        name="moe_dispatch",
    )(tab3, loc, buf0)


def _expert_kernel(be_ref, nu_ref, x_ref, wgu_ref, wd_ref, bg_ref, bu_ref, bd_ref, perm_ref, y_ref,
                   wg_sc, wu_sc, wd_sc):
    i = pl.program_id(0)
    used = i < nu_ref[0]
    new_expert = (i == 0) | (be_ref[i] != be_ref[jnp.maximum(i - 1, 0)])

    @pl.when(used & new_expert)
    def _():
        for c in range(2 * D // DG):
            d = _dot(wgu_ref[:, c * DG:(c + 1) * DG].astype(BF16), perm_ref[...])
            wg_sc[:, c * LANES:(c + 1) * LANES] = d[:, :LANES].astype(BF16)
            wu_sc[:, c * LANES:(c + 1) * LANES] = d[:, LANES:].astype(BF16)
        wd_sc[...] = wd_ref[...].astype(BF16)

    @pl.when(used)
    def _():
        x = _unpack_bf16_pairs(x_ref[...]).astype(BF16)
        g = _dot(x, wg_sc[...]) + bg_ref[...]
        u = _dot(x, wu_sc[...]) + bu_ref[...]
        gate = jnp.minimum(g, LIMIT)
        up = jnp.clip(u, -LIMIT, LIMIT)
        act = (up + 1.0) * gate * jax.nn.sigmoid(ALPHA * gate)
        y = _dot(act.astype(BF16), wd_sc[...]) + bd_ref[...]
        y_ref[...] = _pack_bf16_pairs(y)

    @pl.when(jnp.logical_not(used))
    def _():
        y_ref[...] = jnp.zeros_like(y_ref)


def _deinterleave_perm():
    pm = np.zeros((DG, DG), np.float32)
    o = np.arange(LANES)
    pm[2 * o, o] = 1.0
    pm[2 * o + 1, LANES + o] = 1.0
    return jnp.asarray(pm, BF16)


def _experts(blk_e, n_used, buf, wgu_all, wd_all, layer, bg, bu, bd):
    nblk = buf.shape[0] // BM
    row = lambda i, be, nu: (jnp.minimum(i, nu[0] - 1), 0)
    wsel = lambda i, be, nu: (be[jnp.minimum(i, nu[0] - 1)], 0, 0)
    wsel_l = lambda i, be, nu: (layer, be[jnp.minimum(i, nu[0] - 1)], 0, 0)
    grid_spec = pltpu.PrefetchScalarGridSpec(
        num_scalar_prefetch=2,
        grid=(nblk,),
        in_specs=[pl.BlockSpec((BM, D // 2), row),
                  pl.BlockSpec((None, None, D, 2 * D), wsel_l), pl.BlockSpec((None, None, D, D), wsel_l),
                  pl.BlockSpec((None, 1, D), wsel), pl.BlockSpec((None, 1, D), wsel),
                  pl.BlockSpec((None, 1, D), wsel),
                  pl.BlockSpec((DG, DG), lambda i, be, nu: (0, 0))],
        out_specs=pl.BlockSpec((BM, D // 2), lambda i, be, nu: (i, 0)),
        scratch_shapes=[pltpu.VMEM((D, D), BF16), pltpu.VMEM((D, D), BF16), pltpu.VMEM((D, D), BF16)],
    )
    return pl.pallas_call(
        _expert_kernel,
        grid_spec=grid_spec,
        out_shape=jax.ShapeDtypeStruct(buf.shape, U32),
        compiler_params=_cparams(1, VMEM_LIMIT),
        name="moe_experts",
    )(blk_e, n_used, buf, wgu_all, wd_all, bg, bu, bd, _deinterleave_perm())


def _combine_kernel(tab_ref, tabn_ref, x_ref, ri_ref, rg_ref, mod_ref, fg_ref, yb_ref, o_ref, yloc, sem,
                    *, final):
    i = pl.program_id(0)
    n = pl.num_programs(0)
    slot = i % 2

    def make_copy(sl):
        return lambda lrow, grow, rows: pltpu.make_async_copy(
            yb_ref.at[pl.ds(grow, rows)], yloc.at[sl, pl.ds(lrow, rows)], sem.at[sl])

    @pl.when(i == 0)
    def _():
        yloc[...] = jnp.zeros_like(yloc)
        _segment_copies(tab_ref, make_copy(0), lambda cp: cp.start())

    @pl.when(i + 1 < n)
    def _():
        _segment_copies(tabn_ref, make_copy(1 - slot), lambda cp: cp.start())

    _segment_copies(tab_ref, make_copy(slot), lambda cp: cp.wait())

    lane_r = lax.broadcasted_iota(I32, (1, LMAX), 1)
    g = jnp.zeros((TM, LMAX), F32)
    for k in range(TOP_K):
        g = g + jnp.where(lane_r == ri_ref[:, TOP_K + k:TOP_K + k + 1], rg_ref[:, k:k + 1], 0.0)
    g_hi = g.astype(BF16)
    g_lo = (g - g_hi.astype(F32)).astype(BF16)
    y = _unpack_bf16_pairs(yloc[slot]).astype(BF16)
    f = _dot(g_hi, y) + _dot(g_lo, y)
    x = x_ref[...] + mod_ref[5:6, :] * f
    o_ref[...] = _rms(x, fg_ref[...]) if final else x


def _combine(tab3, x_flat, ri_flat, rg_flat, mod, fg, ybuf, nt, final):
    n = x_flat.shape[0]
    steps = n // TM
    tspec = lambda f: pl.BlockSpec((1, 1, LANES), f, memory_space=pltpu.SMEM)
    return pl.pallas_call(
        functools.partial(_combine_kernel, final=final),
        grid=(steps,),
        in_specs=[tspec(lambda i: (i, 0, 0)),
                  tspec(lambda i: (jnp.minimum(i + 1, steps - 1), 0, 0)),
                  pl.BlockSpec((TM, D), lambda i: (i, 0)),
                  pl.BlockSpec((TM, LANES), lambda i: (i, 0)),
                  pl.BlockSpec((TM, LANES), lambda i: (i, 0)),
                  pl.BlockSpec((None, None, 8, D), lambda i: (i // nt, jnp.minimum(i % nt, 1), 0, 0)),
                  pl.BlockSpec((1, D), lambda i: (0, 0)),
                  pl.BlockSpec(memory_space=pl.ANY)],
        out_specs=pl.BlockSpec((TM, D), lambda i: (i, 0)),
        out_shape=jax.ShapeDtypeStruct((n, D), F32),
        scratch_shapes=[pltpu.VMEM((2, LMAX, D // 2), U32), pltpu.SemaphoreType.DMA((2,))],
        compiler_params=_cparams(1, VMEM_LIMIT),
        name="moe_combine",
    )(tab3, tab3, x_flat, ri_flat, rg_flat, mod, fg, ybuf)


def _repack_w_in(w):
    four, s5, q, k, v, r, a1, z, xbc, dt = jnp.split(
        w, np.cumsum([256, 256, 128, 128, 256, 256, 32, 256, 512, 8])[:-1].tolist(), axis=-1)
    misc = jnp.concatenate([a1, dt, jnp.zeros((D, LANES - 40), w.dtype)], axis=-1)
    return jnp.concatenate([four, s5, q, k, v, r, z, xbc, misc], axis=-1).astype(BF16)


def kernel(x, c, ctx, c_ctx, ada_w, ada_b, norm_mix_g, norm_ffn_g, w_in, w_out, four_w, s5_lam_re, s5_lam_im, s5_log_dt, s5_b_re, s5_b_im, s5_c_re, s5_c_im, s5_d, s5_w_glu, gla_a2, gla_a_b, gla_norm_g, ssd_conv_w, ssd_conv_b, ssd_a_log, ssd_dt_bias, ssd_d, ssd_norm_g, router_w, router_b, exp_w_gu, exp_b_gu, exp_w_dn, exp_b_dn, final_g):
    bsz, seq, _ = x.shape
    clen = ctx.shape[1]
    depth = ada_w.shape[0]
    assert clen == TM and seq % TM == 0 and bsz == 8
    s = clen + seq
    nt = s // TM
    n_tok = bsz * s
    nch, ncc = s // S5_T, clen // S5_T

    cond = jnp.concatenate([c, c_ctx[None, :], jnp.zeros((16 - bsz - 1, D), F32)], axis=0)
    mods = _ada(cond, ada_w, ada_b).reshape(depth, 16, N_MOD, D)
    m_lat = mods[:, :bsz]
    m_ctx = jnp.broadcast_to(mods[:, bsz:bsz + 1], m_lat.shape)
    mods = jnp.stack([m_ctx, m_lat], axis=2)
    mods = jnp.pad(mods, ((0, 0), (0, 0), (0, 0), (0, 2), (0, 0)))

    x_all = jnp.concatenate([ctx, x], axis=1)
    csm = _channel_dft()
    f_lat, f_ctx = _dft_matrix(seq), _dft_matrix(clen)
    avg = jnp.asarray(np.kron(np.eye(GLA_H), np.full((GLA_DV, GLA_DV), 1.0 / GLA_DV)), BF16)
    tri = jnp.asarray(np.tril(np.ones((TM, TM)), -1), BF16)
    upper = jnp.asarray(np.triu(np.ones((LANES, LANES)), 1), BF16)

    ntile = n_tok // TM
    nblk = -(-(n_tok * TOP_K + ntile * N_EXP * (SEG - 1) + N_EXP * (BM - 1)) // BM)
    n_rows = nblk * BM

    for l in range(depth):
        mod = mods[l]
        w_in_p = _repack_w_in(w_in[l])
        conv_w = jnp.pad(ssd_conv_w[l], ((0, 8 - SSD_K), (0, 0)))
        p, four_l, four_c, ut = _pre(x_all, mod, norm_mix_g[l][None], w_in_p, conv_w, ssd_conv_b[l][None],
                                     csm, seq, clen)
        fw = four_w[l].astype(BF16)
        fy_l = _fourier(f_lat, four_l.reshape(bsz, 2 * seq, DG), fw)
        fy_c = _fourier(f_ctx, four_c.reshape(bsz, 2 * clen, DG), fw)

        s5_tabs = _s5_tables(s5_lam_re[l], s5_lam_im[l], s5_log_dt[l], s5_b_re[l], s5_b_im[l],
                                    s5_c_re[l], s5_c_im[l])
        y5 = _s5(ut.reshape(DG, n_tok // S5_T, S5_T), *s5_tabs, bsz, nch, ncc).reshape(DG, n_tok)

        gla_o, ssd_y = [], []
        for r in range(2):
            a2m = jnp.zeros((LANES, LANES), F32).at[GLA_RANK * r:GLA_RANK * (r + 1)].set(gla_a2[l, r])
            gla_o.append(_gla(p, a2m.astype(BF16), gla_a_b[l, r][None, :], rev=bool(r)))
            ssd_y.append(_ssd(p, *_ssd_tables(ssd_a_log[l], ssd_dt_bias[l], r), rev=bool(r)))

        vecs = jnp.stack([s5_d[l], jnp.tile(gla_norm_g[l], GLA_H), jnp.repeat(ssd_d[l], SSD_HD),
                          ssd_norm_g[l]] + [jnp.zeros((DG,), F32)] * 4)
        rw = jnp.pad(router_w[l], ((0, 0), (0, LANES - N_EXP)))
        rw_hi = rw.astype(BF16)
        rw = jnp.stack([rw_hi, (rw - rw_hi.astype(F32)).astype(BF16)])
        rb = jnp.concatenate([router_b[l], jnp.full((LANES - N_EXP,), -1e30, F32)])[None]
        x_mid, loc, ri, rg, tab = _post(x_all, mod, p, fy_l, fy_c, y5, gla_o[0], gla_o[1], ssd_y[0], ssd_y[1],
                                        vecs, s5_w_glu[l].astype(BF16), avg, w_out[l].astype(BF16),
                                        norm_ffn_g[l][None], rw, rb, tri, upper, seq, clen)

        tab = tab.reshape(ntile, 8, LANES)
        cnt, lstart = tab[:, 0, :N_EXP], tab[:, 1, :N_EXP]
        cpad = (cnt + SEG - 1) // SEG * SEG
        in_expert = jnp.cumsum(cpad, axis=0) - cpad
        region = (jnp.sum(cpad, axis=0) + BM - 1) // BM * BM
        pad_end = jnp.cumsum(region)
        gstart = (pad_end - region)[None, :] + in_expert
        tab3 = jnp.concatenate([(cpad % BIG) // SEG, lstart, gstart, cpad // BIG], axis=1)
        tab3 = tab3.astype(I32).reshape(ntile, 1, LANES)
        blk_start = jnp.arange(nblk, dtype=I32) * BM
        blk_e = jnp.minimum(jnp.sum((pad_end[None, :] <= blk_start[:, None]).astype(I32), axis=1), N_EXP - 1)
        n_used = (pad_end[-1] // BM).astype(I32)[None]

        buf = _dispatch(tab3, loc.reshape(ntile, LMAX, D // 2), jnp.zeros((n_rows, D // 2), U32))
        bgu = exp_b_gu[l].reshape(N_EXP, 1, D, 2)
        ybuf = _experts(blk_e, n_used, buf, exp_w_gu, exp_w_dn, l, bgu[..., 0], bgu[..., 1],
                        exp_b_dn[l][:, None, :])
        x_all = _combine(tab3, x_mid.reshape(n_tok, D), ri.reshape(n_tok, LANES), rg.reshape(n_tok, LANES),
                         mod, final_g[None], ybuf, nt, final=(l == depth - 1)).reshape(bsz, s, D)

    return x_all[:, clen:, :]
```

```python
import functools
import math

import jax
import jax.numpy as jnp
import numpy as np
from jax import lax
from jax.experimental import pallas as pl
from jax.experimental.pallas import tpu as pltpu

F32 = jnp.float32
BF16 = jnp.bfloat16
I32 = jnp.int32
U32 = jnp.uint32
HI = lax.Precision.HIGHEST

D = 1024
DG = 256
N_MOD = 6
GRID_W = 64
EPS = 1e-6

S5_CH = 16
S5_G = DG // S5_CH
S5_P = 64
S5_T = 128

GLA_H = 4
GLA_DK = 32
GLA_DV = 64
GLA_RANK = 16
GLA_TAU = 16.0
CHUNK = 64

SSD_H = 4
SSD_HD = 64
SSD_G = 2
SSD_N = 64
SSD_K = 5

N_EXP = 32
TOP_K = 4
LIMIT = 7.0
ALPHA = 1.702

TM = 256
BM = 512
SEG = 8
LMAX = 1280
LANES = 128

C_FOUR, C_S5, C_QK, C_V, C_R, C_Z, C_XBC, C_MISC = 0, 256, 512, 768, 1024, 1280, 1536, 2048
D_P = 2176
MISC_A1 = 0
MISC_DT = 32

VMEM_LIMIT = 56 * 1024 * 1024


def _cparams(n_axes, vmem=None):
    return pltpu.CompilerParams(dimension_semantics=("arbitrary",) * n_axes,
                                vmem_limit_bytes=vmem)


def _silu(x):
    return x * jax.nn.sigmoid(x)


def _rms(x, g):
    return x * lax.rsqrt(jnp.mean(x * x, axis=-1, keepdims=True) + EPS) * g


def _dot(a, b, dims=None, precision=None):
    if dims is None:
        dims = (((a.ndim - 1,), (0,)), ((), ()))
    return lax.dot_general(a, b, dims, precision=precision, preferred_element_type=F32)


_NT = (((1,), (1,)), ((), ()))
_TN = (((0,), (0,)), ((), ()))


def _pack_bf16_pairs(a):
    n = a.shape[1] // 2
    hi = pltpu.bitcast(a[:, :n].astype(BF16).astype(F32), U32)
    lo = pltpu.bitcast(a[:, n:].astype(BF16).astype(F32), U32)
    return hi | (lo >> 16)


def _unpack_bf16_pairs(w):
    hi = pltpu.bitcast(w & jnp.uint32(0xFFFF0000), F32)
    lo = pltpu.bitcast(w << 16, F32)
    return jnp.concatenate([hi, lo], axis=1)


def _ada_kernel(cond_ref, w_ref, b_ref, o_ref):
    c = _silu(cond_ref[...])
    o_ref[...] = _dot(c, w_ref[...], precision=HI) + b_ref[...]


def _ada(cond, ada_w, ada_b):
    depth = ada_w.shape[0]
    rows = cond.shape[0]
    tn = 1536
    return pl.pallas_call(
        _ada_kernel,
        grid=(depth, (N_MOD * D) // tn),
        in_specs=[pl.BlockSpec((rows, D), lambda l, n: (0, 0)),
                  pl.BlockSpec((None, D, tn), lambda l, n: (l, 0, n)),
                  pl.BlockSpec((None, 1, tn), lambda l, n: (l, 0, n))],
        out_specs=pl.BlockSpec((None, rows, tn), lambda l, n: (l, 0, n)),
        out_shape=jax.ShapeDtypeStruct((depth, rows, N_MOD * D), F32),
        compiler_params=_cparams(2, VMEM_LIMIT),
        name="ada_mod",
    )(cond, ada_w, ada_b.reshape(depth, 1, N_MOD * D))


def _pre_kernel(x_ref, mod_ref, g_ref, w_ref, cw_ref, cb_ref, csm_ref, p_ref, fl_ref, fc_ref, ut_ref):
    j = pl.program_id(1)
    x = x_ref[...]
    hn = _rms(x, g_ref[...]) * (1.0 + mod_ref[1:2, :]) + mod_ref[0:1, :]
    p = _dot(hn.astype(BF16), w_ref[...])
    p_ref[:, :C_XBC] = p[:, :C_XBC]
    p_ref[:, C_MISC:] = p[:, C_MISC:]
    ut_ref[...] = p[:, C_S5:C_S5 + DG].T.astype(BF16)

    xbc = p[:, C_XBC:C_MISC]
    period = jnp.where(j == 0, TM, GRID_W)
    pos = lax.broadcasted_iota(I32, (TM, 1), 0) & (period - 1)
    acc = jnp.zeros_like(xbc) + cb_ref[...]
    for k in range(SSD_K):
        d = k - SSD_K // 2
        sh = xbc if d == 0 else pltpu.roll(xbc, (TM - d) % TM, axis=0)
        ok = (pos + d >= 0) & (pos + d < period)
        acc = acc + jnp.where(ok, sh, 0.0) * cw_ref[k:k + 1, :]
    p_ref[:, C_XBC:C_MISC] = _silu(acc)

    xcs = _dot(p[:, C_FOUR:C_FOUR + DG].astype(BF16), csm_ref[...])

    @pl.when(j == 0)
    def _():
        fc_ref[0] = xcs[:, :DG].astype(BF16)
        fc_ref[1] = xcs[:, DG:].astype(BF16)

    @pl.when(j > 0)
    def _():
        fl_ref[0] = xcs[:, :DG].astype(BF16)
        fl_ref[1] = xcs[:, DG:].astype(BF16)


def _pre(x_all, mod, g, w_in_p, conv_w, conv_b, csm, seq, ctx):
    b, s, _ = x_all.shape
    nt = s // TM
    return pl.pallas_call(
        _pre_kernel,
        grid=(b, nt),
        in_specs=[pl.BlockSpec((None, TM, D), lambda i, j: (i, j, 0)),
                  pl.BlockSpec((None, None, 8, D), lambda i, j: (i, jnp.minimum(j, 1), 0, 0)),
                  pl.BlockSpec((1, D), lambda i, j: (0, 0)),
                  pl.BlockSpec((D, D_P), lambda i, j: (0, 0)),
                  pl.BlockSpec((8, 512), lambda i, j: (0, 0)),
                  pl.BlockSpec((1, 512), lambda i, j: (0, 0)),
                  pl.BlockSpec((DG, 2 * DG), lambda i, j: (0, 0))],
        out_specs=[pl.BlockSpec((None, TM, D_P), lambda i, j: (i, j, 0)),
                   pl.BlockSpec((None, 2, TM, DG), lambda i, j: (i, 0, jnp.maximum(j - 1, 0), 0)),
                   pl.BlockSpec((None, 2, ctx, DG), lambda i, j: (i, 0, 0, 0)),
                   pl.BlockSpec((DG, TM), lambda i, j: (0, i * nt + j))],
        out_shape=[jax.ShapeDtypeStruct((b, s, D_P), F32),
                   jax.ShapeDtypeStruct((b, 2, seq, DG), BF16),
                   jax.ShapeDtypeStruct((b, 2, ctx, DG), BF16),
                   jax.ShapeDtypeStruct((DG, b * s), BF16)],
        compiler_params=_cparams(2, VMEM_LIMIT),
        name="pre_mixer",
    )(x_all, mod, g, w_in_p, conv_w, conv_b, csm)


def _four_kernel(f_ref, x_ref, w_ref, o_ref):
    y = _dot(f_ref[...], x_ref[...])
    o_ref[...] = _dot(y.astype(BF16), w_ref[...])


def _fourier(fmat, x2, four_w):
    ln = fmat.shape[0]
    b = x2.shape[0]
    tf = min(512, ln)
    return pl.pallas_call(
        _four_kernel,
        grid=(ln // tf, b),
        in_specs=[pl.BlockSpec((tf, 2 * ln), lambda i, j: (i, 0)),
                  pl.BlockSpec((None, 2 * ln, DG), lambda i, j: (j, 0, 0)),
                  pl.BlockSpec((DG, DG), lambda i, j: (0, 0))],
        out_specs=pl.BlockSpec((None, tf, DG), lambda i, j: (j, i, 0)),
        out_shape=jax.ShapeDtypeStruct((b, ln, DG), F32),
        compiler_params=_cparams(2, VMEM_LIMIT),
        name="fourier_dft",
    )(fmat, x2, four_w)


def _dft_matrix(ln):
    f = jnp.arange(ln, dtype=I32)[:, None]
    t = jnp.arange(ln, dtype=I32)[None, :]
    ang = ((f * t) % ln).astype(F32) * (2.0 * math.pi / ln)
    scale = 1.0 / math.sqrt(ln * 64.0)
    return (jnp.concatenate([jnp.cos(ang), jnp.sin(ang)], axis=1) * scale).astype(BF16)


def _channel_dft():
    i = np.arange(DG)
    same = (i[:, None] // 64) == (i[None, :] // 64)
    ang = 2.0 * np.pi * ((i[:, None] % 64) * (i[None, :] % 64) % 64) / 64.0
    c = np.where(same, np.cos(ang), 0.0)
    s = np.where(same, np.sin(ang), 0.0)
    return jnp.asarray(np.concatenate([c, -s], axis=1), dtype=BF16)


def _s5_kernel(u_ref, kf_ref, kb_ref, p_ref, q_ref, a_ref, y_ref, toep, hlf, hlb, hef, heb,
               *, nb, nch, ncc):
    t_i = lax.broadcasted_iota(I32, (S5_T, S5_T), 0)
    j_i = lax.broadcasted_iota(I32, (S5_T, S5_T), 1)

    def build(h, c):
        for hp in range(S5_CH):
            vf = jnp.broadcast_to(kf_ref[pl.ds(h * S5_CH + hp, 1), :], (S5_T, S5_T))
            vb = jnp.broadcast_to(kb_ref[pl.ds(h * S5_CH + hp, 1), :], (S5_T, S5_T))
            blk = (jnp.where(j_i >= t_i, pltpu.roll(vf, 0, 1, stride=1, stride_axis=0), 0.0)
                   + jnp.where(j_i <= t_i, pltpu.roll(vb, 0, 1, stride=1, stride_axis=0), 0.0))
            toep[pl.ds(pl.multiple_of(h * S5_T, S5_T), S5_T), hp * S5_T:(hp + 1) * S5_T] = blk.astype(BF16)
        return c

    lax.fori_loop(0, S5_CH, build, 0)

    u = jnp.concatenate([u_ref[h] for h in range(S5_CH)], axis=1)
    hl = _dot(u, p_ref[...])
    hlf[...] = hl[:, :LANES]
    hlb[...] = hl[:, LANES:]
    a1f, a2f = a_ref[0:1, :LANES], a_ref[1:2, :LANES]
    a1b, a2b = a_ref[0:1, LANES:], a_ref[1:2, LANES:]

    def step(i, carry):
        sf, sb = carry
        nbk = jnp.where(i < ncc, ncc - 1 - i, nch + ncc - 1 - i)
        rows_f = pl.ds(i, nb, stride=nch)
        rows_b = pl.ds(nbk, nb, stride=nch)
        hef[rows_f, :] = sf
        heb[rows_b, :] = sb
        sf = a1f * sf + a2f * pltpu.roll(sf, 64, axis=1) + hlf[rows_f, :]
        sb = a1b * sb + a2b * pltpu.roll(sb, 64, axis=1) + hlb[rows_b, :]
        return sf, sb

    z = jnp.zeros((nb, LANES), F32)
    lax.fori_loop(0, nch, step, (z, z))
    he = jnp.concatenate([hef[...], heb[...]], axis=1).astype(BF16)
    y = _dot(u, toep[...]) + _dot(he, q_ref[...])
    for hp in range(S5_CH):
        y_ref[hp] = y[:, hp * S5_T:(hp + 1) * S5_T]


def _s5(ut3, kf, kb, ptab, qtab, atab, nb, nch, ncc):
    r = ut3.shape[1]
    w = S5_CH * S5_T
    grp = lambda rows, cols: pl.BlockSpec((None, rows, cols), lambda i: (i, 0, 0))
    return pl.pallas_call(
        functools.partial(_s5_kernel, nb=nb, nch=nch, ncc=ncc),
        grid=(S5_G,),
        in_specs=[pl.BlockSpec((S5_CH, r, S5_T), lambda i: (i, 0, 0)),
                  grp(S5_CH * S5_CH, S5_T), grp(S5_CH * S5_CH, S5_T),
                  grp(w, 2 * LANES), grp(2 * LANES, w), grp(8, 2 * LANES)],
        out_specs=pl.BlockSpec((S5_CH, r, S5_T), lambda i: (i, 0, 0)),
        out_shape=jax.ShapeDtypeStruct((DG, r, S5_T), F32),
        scratch_shapes=[pltpu.VMEM((w, w), BF16)] + [pltpu.VMEM((r, LANES), F32)] * 4,
        compiler_params=_cparams(1, VMEM_LIMIT),
        name="s5_mixer",
    )(ut3, kf, kb, ptab, qtab, atab)


def _s5_tables(lam_re, lam_im, log_dt, b_re, b_im, c_re, c_im):
    t = S5_T
    dt = jnp.exp(log_dt)[..., None]
    ldr, ldi = lam_re * dt, lam_im * dt
    ar_t = np.arange(t)

    def powers(r, exps):
        e = jnp.asarray(np.asarray(exps, np.float32))[None, :, None]
        mag = jnp.exp(ldr[r][:, None, :] * e)
        return mag * jnp.cos(ldi[r][:, None, :] * e), mag * jnp.sin(ldi[r][:, None, :] * e)

    ar, ai = (jnp.stack(z)[:, :, 0] for z in zip(powers(0, [1]), powers(1, [1])))
    den = lam_re * lam_re + lam_im * lam_im
    qr = ((ar - 1.0) * lam_re + ai * lam_im) / den
    qi = (ai * lam_re - (ar - 1.0) * lam_im) / den
    bbr = qr[..., None] * b_re - qi[..., None] * b_im
    bbi = qr[..., None] * b_im + qi[..., None] * b_re

    def c_pow(r, exps):
        pr, pi = powers(r, exps)
        return (c_re[r][:, None] * pr[:, :, None, :] - c_im[r][:, None] * pi[:, :, None, :],
                c_re[r][:, None] * pi[:, :, None, :] + c_im[r][:, None] * pr[:, :, None, :])

    def lag_kernels(r, lags):
        cr, ci = c_pow(r, lags)
        k = (jnp.einsum('gjop,gph->ghoj', cr, bbr[r], precision=HI)
             - jnp.einsum('gjop,gph->ghoj', ci, bbi[r], precision=HI))
        return k.reshape(S5_G, S5_CH * S5_CH, t)

    kf = lag_kernels(0, ar_t)
    kb = lag_kernels(1, (t - ar_t) % t)

    def ptab(r, exps):
        pr, pi = powers(r, exps)
        re = jnp.einsum('gtp,gph->ghtp', pr, bbr[r]) - jnp.einsum('gtp,gph->ghtp', pi, bbi[r])
        im = jnp.einsum('gtp,gph->ghtp', pr, bbi[r]) + jnp.einsum('gtp,gph->ghtp', pi, bbr[r])
        return jnp.concatenate([re, im], axis=-1).reshape(S5_G, S5_CH * t, 2 * S5_P)

    p2 = jnp.concatenate([ptab(0, t - 1 - ar_t), ptab(1, ar_t)], axis=-1)

    def qtab(r, exps):
        cr, ci = c_pow(r, exps)
        re = jnp.transpose(cr, (0, 3, 2, 1)).reshape(S5_G, S5_P, S5_CH * t)
        im = jnp.transpose(ci, (0, 3, 2, 1)).reshape(S5_G, S5_P, S5_CH * t)
        return jnp.concatenate([re, -im], axis=1)

    q2 = jnp.concatenate([qtab(0, ar_t + 1), qtab(1, t - ar_t)], axis=1)
    art, ait = (jnp.stack(z)[:, :, 0] for z in zip(powers(0, [t]), powers(1, [t])))
    a1 = jnp.concatenate([art[0], art[0], art[1], art[1]], axis=-1)
    a2 = jnp.concatenate([-ait[0], ait[0], -ait[1], ait[1]], axis=-1)
    atab = jnp.concatenate([a1[:, None], a2[:, None], jnp.zeros((S5_G, 6, 4 * S5_P), F32)], axis=1)
    return kf, kb, p2.astype(BF16), q2.astype(BF16), atab


def _split3(x):
    x1 = x.astype(BF16)
    r1 = x - x1.astype(F32)
    x2 = r1.astype(BF16)
    x3 = (r1 - x2.astype(F32)).astype(BF16)
    return x1, x2, x3


def _dot01(m, x):
    x1, x2, x3 = _split3(x)
    return (_dot(m, x3) + _dot(m, x2)) + _dot(m, x1)


def _tile_cumsum_mask(rev):
    t = lax.broadcasted_iota(I32, (TM, TM), 0)
    s = lax.broadcasted_iota(I32, (TM, TM), 1)
    order = (s >= t) if rev else (s <= t)
    return (order & ((t // CHUNK) == (s // CHUNK))).astype(BF16)


def _head_lane_mask(rev, width):
    t = lax.broadcasted_iota(I32, (CHUNK, width), 0)
    s = lax.broadcasted_iota(I32, (CHUNK, width), 1) % CHUNK
    return (s >= t) if rev else (s <= t)


def _gla_kernel(qk_ref, v_ref, misc_ref, a2_ref, ab_ref, o_ref, st_ref, *, rev):
    j = pl.program_id(1)

    @pl.when(j == 0)
    def _():
        st_ref[...] = jnp.zeros_like(st_ref)

    lane_k = lax.broadcasted_iota(I32, (1, LANES), 1) // GLA_DK
    lane_v = lax.broadcasted_iota(I32, (1, DG), 1) // GLA_DV
    causal = _head_lane_mask(rev, DG)
    st_diag = (lax.broadcasted_iota(I32, (DG, 1), 0) // GLA_DV) == lane_k

    z = _dot(misc_ref[...].astype(BF16), a2_ref[...]) + ab_ref[...]
    la_t = -(jnp.maximum(-z, 0.0) + jnp.log1p(jnp.exp(-jnp.abs(z)))) * (1.0 / GLA_TAU)
    b_t = _dot01(_tile_cumsum_mask(rev), la_t)
    st = st_ref[...]
    chunks = range(TM // CHUNK)
    for c in (reversed(chunks) if rev else chunks):
        rows = slice(c * CHUNK, (c + 1) * CHUNK)
        la, bcum = la_t[rows], b_t[rows]
        q = qk_ref[rows, :LANES] * (GLA_DK ** -0.5)
        k = qk_ref[rows, LANES:]
        v = v_ref[rows, :]
        tot = jnp.sum(la, axis=0, keepdims=True)
        q_in = (q * jnp.exp(bcum)).astype(BF16)
        k_in = k * jnp.exp(-bcum)
        k_dec = (k * jnp.exp(tot - bcum)).astype(BF16)
        k4 = jnp.concatenate([jnp.where(lane_k == h, k_in, 0.0) for h in range(GLA_H)], axis=0)
        att = jnp.where(causal, _dot(q_in, k4.astype(BF16), _NT), 0.0)
        vb = v.astype(BF16)
        vbd = jnp.concatenate([jnp.where(lane_v == h, vb, jnp.zeros_like(vb)) for h in range(GLA_H)], axis=0)
        o = _dot(att.astype(BF16), vbd) + _dot(q_in, jnp.where(st_diag, st, 0.0).astype(BF16), _NT)
        o_ref[rows, :] = o
        st = st * jnp.exp(tot) + _dot(vb, k_dec, _TN)
    st_ref[...] = st


def _tile_order(rev, j, nt):
    return jnp.where(j == 0, 0, nt - j) if rev else j


def _gla(p, a2m, abm, rev):
    b, s, _ = p.shape
    nt = s // TM
    return pl.pallas_call(
        functools.partial(_gla_kernel, rev=rev),
        grid=(b, nt),
        in_specs=[pl.BlockSpec((None, TM, DG), lambda i, j: (i, _tile_order(rev, j, nt), C_QK // DG)),
                  pl.BlockSpec((None, TM, DG), lambda i, j: (i, _tile_order(rev, j, nt), C_V // DG)),
                  pl.BlockSpec((None, TM, LANES), lambda i, j: (i, _tile_order(rev, j, nt), C_MISC // LANES)),
                  pl.BlockSpec((LANES, LANES), lambda i, j: (0, 0)),
                  pl.BlockSpec((1, LANES), lambda i, j: (0, 0))],
        out_specs=pl.BlockSpec((None, TM, DG), lambda i, j: (i, _tile_order(rev, j, nt), 0)),
        out_shape=jax.ShapeDtypeStruct((b, s, DG), F32),
        scratch_shapes=[pltpu.VMEM((DG, LANES), F32)],
        compiler_params=_cparams(2, VMEM_LIMIT),
        name="gla_bwd" if rev else "gla_fwd",
    )(p, p, p, a2m, abm)


def _ssd_kernel(xbc_ref, misc_ref, dtb_ref, av_ref, y_ref, sb_ref, *, rev):
    j = pl.program_id(1)

    @pl.when(j == 0)
    def _():
        sb_ref[...] = jnp.zeros_like(sb_ref)

    lane = lax.broadcasted_iota(I32, (1, LANES), 1)
    lane_g = lane // SSD_N
    lane_h = lax.broadcasted_iota(I32, (1, DG), 1) // SSD_HD
    row_h = lax.broadcasted_iota(I32, (DG, 1), 0) // SSD_HD
    blockmask = row_h == lane_h
    causal = _head_lane_mask(rev, DG)
    diag = (lax.broadcasted_iota(I32, (CHUNK, DG), 0)
            == lax.broadcasted_iota(I32, (CHUNK, DG), 1) % CHUNK)

    def per_head(x):
        xr = pltpu.roll(x, SSD_N, axis=1)
        return jnp.concatenate([jnp.where(lane < SSD_N, x, xr), jnp.where(lane < SSD_N, xr, x)], axis=1)

    zm = misc_ref[...] + dtb_ref[...]
    sp = jnp.maximum(zm, 0.0) + jnp.log1p(jnp.exp(-jnp.abs(zm)))
    dt_t = jnp.zeros((TM, DG), F32)
    for h in range(SSD_H):
        col = jnp.sum(jnp.where(lane == MISC_DT + SSD_H * int(rev) + h, sp, 0.0), axis=1, keepdims=True)
        dt_t = jnp.where(lane_h == h, col, dt_t)
    da_t = dt_t * av_ref[...]
    cum_t = _dot01(_tile_cumsum_mask(rev), da_t)

    sb = sb_ref[...]
    chunks = range(TM // CHUNK)
    for c in (reversed(chunks) if rev else chunks):
        rows = slice(c * CHUNK, (c + 1) * CHUNK)
        da, cum = da_t[rows], cum_t[rows]
        bmc = xbc_ref[rows, DG:DG + LANES]
        cmc = xbc_ref[rows, DG + LANES:]
        xdt = (xbc_ref[rows, :DG] * dt_t[rows]).astype(BF16)
        tot = jnp.sum(da, axis=0, keepdims=True)
        cum_row = jnp.sum(jnp.where(diag, cum, 0.0), axis=0, keepdims=True)
        decay = jnp.where(causal, jnp.exp(jnp.minimum(cum - cum_row, 0.0)), 0.0)
        b4 = jnp.concatenate([jnp.where(lane_g == h // (SSD_H // SSD_G), bmc, 0.0) for h in range(SSD_H)],
                             axis=0)
        scores = _dot(cmc.astype(BF16), b4.astype(BF16), _NT) * decay
        xbd = jnp.concatenate([jnp.where(lane_h == h, xdt, jnp.zeros_like(xdt)) for h in range(SSD_H)],
                              axis=0)
        cdec = per_head(cmc) * jnp.exp(cum)
        y = _dot(scores.astype(BF16), xbd) + _dot(cdec.astype(BF16), sb.astype(BF16))
        y_ref[rows, :] = y
        bdec = per_head(bmc) * jnp.exp(tot - cum)
        upd = _dot(bdec.astype(BF16), xdt, _TN)
        sb = sb * jnp.exp(tot) + jnp.where(blockmask, upd, 0.0)
    sb_ref[...] = sb


def _ssd(p, dtb, av, rev):
    b, s, _ = p.shape
    nt = s // TM
    return pl.pallas_call(
        functools.partial(_ssd_kernel, rev=rev),
        grid=(b, nt),
        in_specs=[pl.BlockSpec((None, TM, 512), lambda i, j: (i, _tile_order(rev, j, nt), C_XBC // 512)),
                  pl.BlockSpec((None, TM, LANES), lambda i, j: (i, _tile_order(rev, j, nt), C_MISC // LANES)),
                  pl.BlockSpec((1, LANES), lambda i, j: (0, 0)),
                  pl.BlockSpec((1, DG), lambda i, j: (0, 0))],
        out_specs=pl.BlockSpec((None, TM, DG), lambda i, j: (i, _tile_order(rev, j, nt), 0)),
        out_shape=jax.ShapeDtypeStruct((b, s, DG), F32),
        scratch_shapes=[pltpu.VMEM((DG, DG), F32)],
        compiler_params=_cparams(2, VMEM_LIMIT),
        name="ssd_bwd" if rev else "ssd_fwd",
    )(p, p, dtb, av)


def _ssd_tables(a_log, dt_bias, r):
    dtb = jnp.zeros((1, LANES), F32).at[0, MISC_DT + SSD_H * r:MISC_DT + SSD_H * (r + 1)].set(dt_bias[r])
    av = jnp.repeat(-jnp.exp(a_log[r]), SSD_HD)[None, :]
    return dtb, av


def _gelu_tanh(x):
    return 0.5 * x * (1.0 + jnp.tanh(math.sqrt(2.0 / math.pi) * (x + 0.044715 * x * x * x)))


def _post_kernel(x_ref, mod_ref, us5_ref, r_ref, z_ref, xs_ref, fl_ref, fc_ref, s5y_ref,
                 gf_ref, gb_ref, sf_ref, sbk_ref, vec_ref, wglu_ref, avg_ref, wout_ref,
                 gffn_ref, rw_ref, rb_ref, tri_ref, upper_ref,
                 xo_ref, loc_ref, ri_ref, rg_ref, tab_ref):
    j = pl.program_id(1)
    s5_d, gla_g, ssd_d, ssd_g = vec_ref[0:1, :], vec_ref[1:2, :], vec_ref[2:3, :], vec_ref[3:4, :]
    four = jnp.where(j == 0, fc_ref[...], fl_ref[...])
    y5 = s5y_ref[...].T + s5_d * us5_ref[...]
    act = _gelu_tanh(y5)
    s5o = act * jax.nn.sigmoid(_dot(act.astype(BF16), wglu_ref[...]))
    og = gf_ref[...] + gb_ref[...]
    sq = og * og
    sq_hi = sq.astype(BF16)
    sq_lo = (sq - sq_hi.astype(F32)).astype(BF16)
    ms = _dot(sq_lo, avg_ref[...]) + _dot(sq_hi, avg_ref[...])
    glo = og * lax.rsqrt(ms + EPS) * gla_g * _silu(r_ref[...])
    ys = (sf_ref[...] + sbk_ref[...] + ssd_d * xs_ref[...]) * _silu(z_ref[...])
    sso = _rms(ys, ssd_g)
    mix = jnp.concatenate([four, s5o, glo, sso], axis=1).astype(BF16)
    x = x_ref[...] + mod_ref[2:3, :] * _dot(mix, wout_ref[...])
    xo_ref[...] = x
    h = _rms(x, gffn_ref[...]) * (1.0 + mod_ref[4:5, :]) + mod_ref[3:4, :]

    h_hi = h.astype(BF16)
    h_lo = (h - h_hi.astype(F32)).astype(BF16)
    logits = ((_dot(h_hi, rw_ref[1]) + _dot(h_lo, rw_ref[0])) + _dot(h_hi, rw_ref[0])
              + rb_ref[...])
    lane = lax.broadcasted_iota(I32, (TM, LANES), 1)
    vals, idxs = [], []
    for _k in range(TOP_K):
        mx = jnp.max(logits, axis=1, keepdims=True)
        ix = jnp.min(jnp.where(logits == mx, lane, LANES), axis=1, keepdims=True)
        vals.append(mx)
        idxs.append(ix)
        logits = jnp.where(lane == ix, -jnp.inf, logits)
    es = [jnp.exp(v - vals[0]) for v in vals]
    den = es[0] + es[1] + es[2] + es[3]
    onehots = [(lane == ix) for ix in idxs]
    oh = sum(o.astype(F32) for o in onehots)

    ahead = _dot(tri_ref[...], oh.astype(BF16))
    cnt = jnp.sum(oh, axis=0, keepdims=True)
    cpad = jnp.floor((cnt + (SEG - 1.0)) * (1.0 / SEG)) * SEG
    lstart = _dot(jnp.broadcast_to(cpad, (8, LANES)).astype(BF16), upper_ref[...])[0:1]
    slot_of = lstart + ahead
    lane_r = lax.broadcasted_iota(I32, (1, LMAX), 1)
    ri = jnp.zeros((TM, LANES), I32)
    rg = jnp.zeros((TM, LANES), F32)
    pt = jnp.zeros((TM, LMAX), F32)
    for k in range(TOP_K):
        lpos = jnp.sum(jnp.where(onehots[k], slot_of, 0.0), axis=1, keepdims=True).astype(I32)
        pt = pt + (lane_r == lpos).astype(F32)
        ri = jnp.where(lane == k, idxs[k], ri)
        ri = jnp.where(lane == TOP_K + k, lpos, ri)
        rg = jnp.where(lane == k, es[k] / den, rg)
    ri_ref[...] = ri
    rg_ref[...] = rg
    loc_ref[...] = _pack_bf16_pairs(_dot(pt.astype(BF16), h_hi, _TN))
    row = lax.broadcasted_iota(I32, (8, LANES), 0)
    tab_ref[...] = jnp.where(row == 0, cnt.astype(I32), jnp.where(row == 1, lstart.astype(I32), 0))


def _post(x_all, mod, p, four_l, four_c, s5y, gla_f, gla_b, ssd_f, ssd_b, vecs, wglu, avg, wout, gffn,
          rw, rb, tri, upper, seq, ctx):
    b, s, _ = x_all.shape
    nt = s // TM
    tile = lambda w, col: pl.BlockSpec((None, TM, w), lambda i, j: (i, j, col))
    full = lambda a: pl.BlockSpec(a.shape, lambda i, j: (0,) * a.ndim)
    per_tile = lambda r, w: pl.BlockSpec((None, None, r, w), lambda i, j: (i, j, 0, 0))
    return pl.pallas_call(
        _post_kernel,
        grid=(b, nt),
        in_specs=[tile(D, 0),
                  pl.BlockSpec((None, None, 8, D), lambda i, j: (i, jnp.minimum(j, 1), 0, 0)),
                  tile(DG, C_S5 // DG), tile(DG, C_R // DG), tile(DG, C_Z // DG), tile(DG, C_XBC // DG),
                  pl.BlockSpec((None, TM, DG), lambda i, j: (i, jnp.maximum(j - 1, 0), 0)),
                  pl.BlockSpec((None, ctx, DG), lambda i, j: (i, 0, 0)),
                  pl.BlockSpec((DG, TM), lambda i, j: (0, i * nt + j)),
                  tile(DG, 0), tile(DG, 0), tile(DG, 0), tile(DG, 0),
                  full(vecs), full(wglu), full(avg), full(wout), full(gffn), full(rw), full(rb), full(tri),
                  full(upper)],
        out_specs=[tile(D, 0), per_tile(LMAX, D // 2), tile(LANES, 0), tile(LANES, 0), per_tile(8, LANES)],
        out_shape=[jax.ShapeDtypeStruct((b, s, D), F32),
                   jax.ShapeDtypeStruct((b, nt, LMAX, D // 2), U32),
                   jax.ShapeDtypeStruct((b, s, LANES), I32),
                   jax.ShapeDtypeStruct((b, s, LANES), F32),
                   jax.ShapeDtypeStruct((b, nt, 8, LANES), I32)],
        compiler_params=_cparams(2, VMEM_LIMIT),
        name="post_mixer_router",
    )(x_all, mod, p, p, p, p, four_l, four_c, s5y, gla_f, gla_b, ssd_f, ssd_b,
      vecs, wglu, avg, wout, gffn, rw, rb, tri, upper)


T_NSEG, T_LSTART, T_GSTART, T_NBIG = 0, 32, 64, 96
BIG = 32


def _segment_copies(tab_ref, make_copy, action):
    def per_expert(e, c):
        lstart = pl.multiple_of(tab_ref[0, 0, T_LSTART + e], SEG)
        gstart = pl.multiple_of(tab_ref[0, 0, T_GSTART + e], SEG)
        nbig = tab_ref[0, 0, T_NBIG + e]

        def big_piece(q, c2):
            action(make_copy(lstart + q * BIG, gstart + q * BIG, BIG))
            return c2

        def seg_piece(q, c2):
            action(make_copy(lstart + nbig * BIG + q * SEG, gstart + nbig * BIG + q * SEG, SEG))
            return c2
        c = lax.fori_loop(0, nbig, big_piece, c)
        return lax.fori_loop(0, tab_ref[0, 0, T_NSEG + e], seg_piece, c)
    lax.fori_loop(0, N_EXP, per_expert, 0)


def _dispatch_kernel(tab_ref, loc_ref, buf_in_ref, buf_ref, sem):
    del buf_in_ref

    def make_copy(lrow, grow, rows):
        return pltpu.make_async_copy(loc_ref.at[0, pl.ds(lrow, rows)], buf_ref.at[pl.ds(grow, rows)], sem)

    _segment_copies(tab_ref, make_copy, lambda cp: cp.start())
    _segment_copies(tab_ref, make_copy, lambda cp: cp.wait())


def _dispatch(tab3, loc, buf0):
    ntile = loc.shape[0]
    return pl.pallas_call(
        _dispatch_kernel,
        grid=(ntile,),
        in_specs=[pl.BlockSpec((1, 1, LANES), lambda i: (i, 0, 0), memory_space=pltpu.SMEM),
                  pl.BlockSpec((1, LMAX, D // 2), lambda i: (i, 0, 0)),
                  pl.BlockSpec(memory_space=pl.ANY)],
        out_specs=pl.BlockSpec(memory_space=pl.ANY),
        out_shape=jax.ShapeDtypeStruct(buf0.shape, U32),
        scratch_shapes=[pltpu.SemaphoreType.DMA(())],
        input_output_aliases={2: 0},
        compiler_params=_cparams(1, VMEM_LIMIT),
        name="moe_dispatch",
    )(tab3, loc, buf0)


def _expert_kernel(be_ref, nu_ref, x_ref, wgu_ref, wd_ref, bg_ref, bu_ref, bd_ref, perm_ref, y_ref,
                   wg_sc, wu_sc, wd_sc):
    i = pl.program_id(0)
    used = i < nu_ref[0]
    new_expert = (i == 0) | (be_ref[i] != be_ref[jnp.maximum(i - 1, 0)])

    @pl.when(used & new_expert)
    def _():
        for c in range(2 * D // DG):
            d = _dot(wgu_ref[:, c * DG:(c + 1) * DG].astype(BF16), perm_ref[...])
            wg_sc[:, c * LANES:(c + 1) * LANES] = d[:, :LANES].astype(BF16)
            wu_sc[:, c * LANES:(c + 1) * LANES] = d[:, LANES:].astype(BF16)
        wd_sc[...] = wd_ref[...].astype(BF16)

    @pl.when(used)
    def _():
        x = _unpack_bf16_pairs(x_ref[...]).astype(BF16)
        g = _dot(x, wg_sc[...]) + bg_ref[...]
        u = _dot(x, wu_sc[...]) + bu_ref[...]
        gate = jnp.minimum(g, LIMIT)
        up = jnp.clip(u, -LIMIT, LIMIT)
        act = (up + 1.0) * gate * jax.nn.sigmoid(ALPHA * gate)
        y = _dot(act.astype(BF16), wd_sc[...]) + bd_ref[...]
        y_ref[...] = _pack_bf16_pairs(y)

    @pl.when(jnp.logical_not(used))
    def _():
        y_ref[...] = jnp.zeros_like(y_ref)


def _deinterleave_perm():
    pm = np.zeros((DG, DG), np.float32)
    o = np.arange(LANES)
    pm[2 * o, o] = 1.0
    pm[2 * o + 1, LANES + o] = 1.0
    return jnp.asarray(pm, BF16)


def _experts(blk_e, n_used, buf, wgu_all, wd_all, layer, bg, bu, bd):
    nblk = buf.shape[0] // BM
    row = lambda i, be, nu: (jnp.minimum(i, nu[0] - 1), 0)
    wsel = lambda i, be, nu: (be[jnp.minimum(i, nu[0] - 1)], 0, 0)
    wsel_l = lambda i, be, nu: (layer, be[jnp.minimum(i, nu[0] - 1)], 0, 0)
    grid_spec = pltpu.PrefetchScalarGridSpec(
        num_scalar_prefetch=2,
        grid=(nblk,),
        in_specs=[pl.BlockSpec((BM, D // 2), row),
                  pl.BlockSpec((None, None, D, 2 * D), wsel_l), pl.BlockSpec((None, None, D, D), wsel_l),
                  pl.BlockSpec((None, 1, D), wsel), pl.BlockSpec((None, 1, D), wsel),
                  pl.BlockSpec((None, 1, D), wsel),
                  pl.BlockSpec((DG, DG), lambda i, be, nu: (0, 0))],
        out_specs=pl.BlockSpec((BM, D // 2), lambda i, be, nu: (i, 0)),
        scratch_shapes=[pltpu.VMEM((D, D), BF16), pltpu.VMEM((D, D), BF16), pltpu.VMEM((D, D), BF16)],
    )
    return pl.pallas_call(
        _expert_kernel,
        grid_spec=grid_spec,
        out_shape=jax.ShapeDtypeStruct(buf.shape, U32),
        compiler_params=_cparams(1, VMEM_LIMIT),
        name="moe_experts",
    )(blk_e, n_used, buf, wgu_all, wd_all, bg, bu, bd, _deinterleave_perm())


def _combine_kernel(tab_ref, tabn_ref, x_ref, ri_ref, rg_ref, mod_ref, fg_ref, yb_ref, o_ref, yloc, sem,
                    *, final):
    i = pl.program_id(0)
    n = pl.num_programs(0)
    slot = i % 2

    def make_copy(sl):
        return lambda lrow, grow, rows: pltpu.make_async_copy(
            yb_ref.at[pl.ds(grow, rows)], yloc.at[sl, pl.ds(lrow, rows)], sem.at[sl])

    @pl.when(i == 0)
    def _():
        yloc[...] = jnp.zeros_like(yloc)
        _segment_copies(tab_ref, make_copy(0), lambda cp: cp.start())

    @pl.when(i + 1 < n)
    def _():
        _segment_copies(tabn_ref, make_copy(1 - slot), lambda cp: cp.start())

    _segment_copies(tab_ref, make_copy(slot), lambda cp: cp.wait())

    lane_r = lax.broadcasted_iota(I32, (1, LMAX), 1)
    g = jnp.zeros((TM, LMAX), F32)
    for k in range(TOP_K):
        g = g + jnp.where(lane_r == ri_ref[:, TOP_K + k:TOP_K + k + 1], rg_ref[:, k:k + 1], 0.0)
    g_hi = g.astype(BF16)
    g_lo = (g - g_hi.astype(F32)).astype(BF16)
    y = _unpack_bf16_pairs(yloc[slot]).astype(BF16)
    f = _dot(g_hi, y) + _dot(g_lo, y)
    x = x_ref[...] + mod_ref[5:6, :] * f
    o_ref[...] = _rms(x, fg_ref[...]) if final else x


def _combine(tab3, x_flat, ri_flat, rg_flat, mod, fg, ybuf, nt, final):
    n = x_flat.shape[0]
    steps = n // TM
    tspec = lambda f: pl.BlockSpec((1, 1, LANES), f, memory_space=pltpu.SMEM)
    return pl.pallas_call(
        functools.partial(_combine_kernel, final=final),
        grid=(steps,),
        in_specs=[tspec(lambda i: (i, 0, 0)),
                  tspec(lambda i: (jnp.minimum(i + 1, steps - 1), 0, 0)),
                  pl.BlockSpec((TM, D), lambda i: (i, 0)),
                  pl.BlockSpec((TM, LANES), lambda i: (i, 0)),
                  pl.BlockSpec((TM, LANES), lambda i: (i, 0)),
                  pl.BlockSpec((None, None, 8, D), lambda i: (i // nt, jnp.minimum(i % nt, 1), 0, 0)),
                  pl.BlockSpec((1, D), lambda i: (0, 0)),
                  pl.BlockSpec(memory_space=pl.ANY)],
        out_specs=pl.BlockSpec((TM, D), lambda i: (i, 0)),
        out_shape=jax.ShapeDtypeStruct((n, D), F32),
        scratch_shapes=[pltpu.VMEM((2, LMAX, D // 2), U32), pltpu.SemaphoreType.DMA((2,))],
        compiler_params=_cparams(1, VMEM_LIMIT),
        name="moe_combine",
    )(tab3, tab3, x_flat, ri_flat, rg_flat, mod, fg, ybuf)


def _repack_w_in(w):
    four, s5, q, k, v, r, a1, z, xbc, dt = jnp.split(
        w, np.cumsum([256, 256, 128, 128, 256, 256, 32, 256, 512, 8])[:-1].tolist(), axis=-1)
    misc = jnp.concatenate([a1, dt, jnp.zeros((D, LANES - 40), w.dtype)], axis=-1)
    return jnp.concatenate([four, s5, q, k, v, r, z, xbc, misc], axis=-1).astype(BF16)


def kernel(x, c, ctx, c_ctx, ada_w, ada_b, norm_mix_g, norm_ffn_g, w_in, w_out, four_w, s5_lam_re, s5_lam_im, s5_log_dt, s5_b_re, s5_b_im, s5_c_re, s5_c_im, s5_d, s5_w_glu, gla_a2, gla_a_b, gla_norm_g, ssd_conv_w, ssd_conv_b, ssd_a_log, ssd_dt_bias, ssd_d, ssd_norm_g, router_w, router_b, exp_w_gu, exp_b_gu, exp_w_dn, exp_b_dn, final_g):
    bsz, seq, _ = x.shape
    clen = ctx.shape[1]
    depth = ada_w.shape[0]
    assert clen == TM and seq % TM == 0 and bsz == 8
    s = clen + seq
    nt = s // TM
    n_tok = bsz * s
    nch, ncc = s // S5_T, clen // S5_T

    cond = jnp.concatenate([c, c_ctx[None, :], jnp.zeros((16 - bsz - 1, D), F32)], axis=0)
    mods = _ada(cond, ada_w, ada_b).reshape(depth, 16, N_MOD, D)
    m_lat = mods[:, :bsz]
    m_ctx = jnp.broadcast_to(mods[:, bsz:bsz + 1], m_lat.shape)
    mods = jnp.stack([m_ctx, m_lat], axis=2)
    mods = jnp.pad(mods, ((0, 0), (0, 0), (0, 0), (0, 2), (0, 0)))

    x_all = jnp.concatenate([ctx, x], axis=1)
    csm = _channel_dft()
    f_lat, f_ctx = _dft_matrix(seq), _dft_matrix(clen)
    avg = jnp.asarray(np.kron(np.eye(GLA_H), np.full((GLA_DV, GLA_DV), 1.0 / GLA_DV)), BF16)
    tri = jnp.asarray(np.tril(np.ones((TM, TM)), -1), BF16)
    upper = jnp.asarray(np.triu(np.ones((LANES, LANES)), 1), BF16)

    ntile = n_tok // TM
    nblk = -(-(n_tok * TOP_K + ntile * N_EXP * (SEG - 1) + N_EXP * (BM - 1)) // BM)
    n_rows = nblk * BM

    for l in range(depth):
        mod = mods[l]
        w_in_p = _repack_w_in(w_in[l])
        conv_w = jnp.pad(ssd_conv_w[l], ((0, 8 - SSD_K), (0, 0)))
        p, four_l, four_c, ut = _pre(x_all, mod, norm_mix_g[l][None], w_in_p, conv_w, ssd_conv_b[l][None],
                                     csm, seq, clen)
        fw = four_w[l].astype(BF16)
        fy_l = _fourier(f_lat, four_l.reshape(bsz, 2 * seq, DG), fw)
        fy_c = _fourier(f_ctx, four_c.reshape(bsz, 2 * clen, DG), fw)

        s5_tabs = _s5_tables(s5_lam_re[l], s5_lam_im[l], s5_log_dt[l], s5_b_re[l], s5_b_im[l],
                                    s5_c_re[l], s5_c_im[l])
        y5 = _s5(ut.reshape(DG, n_tok // S5_T, S5_T), *s5_tabs, bsz, nch, ncc).reshape(DG, n_tok)

        gla_o, ssd_y = [], []
        for r in range(2):
            a2m = jnp.zeros((LANES, LANES), F32).at[GLA_RANK * r:GLA_RANK * (r + 1)].set(gla_a2[l, r])
            gla_o.append(_gla(p, a2m.astype(BF16), gla_a_b[l, r][None, :], rev=bool(r)))
            ssd_y.append(_ssd(p, *_ssd_tables(ssd_a_log[l], ssd_dt_bias[l], r), rev=bool(r)))

        vecs = jnp.stack([s5_d[l], jnp.tile(gla_norm_g[l], GLA_H), jnp.repeat(ssd_d[l], SSD_HD),
                          ssd_norm_g[l]] + [jnp.zeros((DG,), F32)] * 4)
        rw = jnp.pad(router_w[l], ((0, 0), (0, LANES - N_EXP)))
        rw_hi = rw.astype(BF16)
        rw = jnp.stack([rw_hi, (rw - rw_hi.astype(F32)).astype(BF16)])
        rb = jnp.concatenate([router_b[l], jnp.full((LANES - N_EXP,), -1e30, F32)])[None]
        x_mid, loc, ri, rg, tab = _post(x_all, mod, p, fy_l, fy_c, y5, gla_o[0], gla_o[1], ssd_y[0], ssd_y[1],
                                        vecs, s5_w_glu[l].astype(BF16), avg, w_out[l].astype(BF16),
                                        norm_ffn_g[l][None], rw, rb, tri, upper, seq, clen)

        tab = tab.reshape(ntile, 8, LANES)
        cnt, lstart = tab[:, 0, :N_EXP], tab[:, 1, :N_EXP]
        cpad = (cnt + SEG - 1) // SEG * SEG
        in_expert = jnp.cumsum(cpad, axis=0) - cpad
        region = (jnp.sum(cpad, axis=0) + BM - 1) // BM * BM
        pad_end = jnp.cumsum(region)
        gstart = (pad_end - region)[None, :] + in_expert
        tab3 = jnp.concatenate([(cpad % BIG) // SEG, lstart, gstart, cpad // BIG], axis=1)
        tab3 = tab3.astype(I32).reshape(ntile, 1, LANES)
        blk_start = jnp.arange(nblk, dtype=I32) * BM
        blk_e = jnp.minimum(jnp.sum((pad_end[None, :] <= blk_start[:, None]).astype(I32), axis=1), N_EXP - 1)
        n_used = (pad_end[-1] // BM).astype(I32)[None]

        buf = _dispatch(tab3, loc.reshape(ntile, LMAX, D // 2), jnp.zeros((n_rows, D // 2), U32))
        bgu = exp_b_gu[l].reshape(N_EXP, 1, D, 2)
        ybuf = _experts(blk_e, n_used, buf, exp_w_gu, exp_w_dn, l, bgu[..., 0], bgu[..., 1],
                        exp_b_dn[l][:, None, :])
        x_all = _combine(tab3, x_mid.reshape(n_tok, D), ri.reshape(n_tok, LANES), rg.reshape(n_tok, LANES),
                         mod, final_g[None], ybuf, nt, final=(l == depth - 1)).reshape(bsz, s, D)

    return x_all[:, clen:, :]
```

```python
import functools
import math

import jax
import jax.numpy as jnp
import numpy as np
from jax import lax
from jax.experimental import pallas as pl
from jax.experimental.pallas import tpu as pltpu

F32 = jnp.float32
BF16 = jnp.bfloat16
I32 = jnp.int32
U32 = jnp.uint32
HI = lax.Precision.HIGHEST

D = 1024
DG = 256
N_MOD = 6
GRID_W = 64
EPS = 1e-6

S5_CH = 16
S5_G = DG // S5_CH
S5_P = 64
S5_T = 128

GLA_H = 4
GLA_DK = 32
GLA_DV = 64
GLA_RANK = 16
GLA_TAU = 16.0
CHUNK = 64

SSD_H = 4
SSD_HD = 64
SSD_G = 2
SSD_N = 64
SSD_K = 5

N_EXP = 32
TOP_K = 4
LIMIT = 7.0
ALPHA = 1.702

TM = 256
BM = 512
SEG = 8
LMAX = 1280
LANES = 128

C_FOUR, C_S5, C_QK, C_V, C_R, C_Z, C_XBC, C_MISC = 0, 256, 512, 768, 1024, 1280, 1536, 2048
D_P = 2176
MISC_A1 = 0
MISC_DT = 32

VMEM_LIMIT = 56 * 1024 * 1024


def _cparams(n_axes, vmem=None):
    return pltpu.CompilerParams(dimension_semantics=("arbitrary",) * n_axes,
                                vmem_limit_bytes=vmem)


def _silu(x):
    return x * jax.nn.sigmoid(x)


def _rms(x, g):
    return x * lax.rsqrt(jnp.mean(x * x, axis=-1, keepdims=True) + EPS) * g


def _dot(a, b, dims=None, precision=None):
    if dims is None:
        dims = (((a.ndim - 1,), (0,)), ((), ()))
    return lax.dot_general(a, b, dims, precision=precision, preferred_element_type=F32)


_NT = (((1,), (1,)), ((), ()))
_TN = (((0,), (0,)), ((), ()))


def _pack_bf16_pairs(a):
    n = a.shape[1] // 2
    hi = pltpu.bitcast(a[:, :n].astype(BF16).astype(F32), U32)
    lo = pltpu.bitcast(a[:, n:].astype(BF16).astype(F32), U32)
    return hi | (lo >> 16)


def _unpack_bf16_pairs(w):
    hi = pltpu.bitcast(w & jnp.uint32(0xFFFF0000), F32)
    lo = pltpu.bitcast(w << 16, F32)
    return jnp.concatenate([hi, lo], axis=1)


def _ada_kernel(cond_ref, w_ref, b_ref, o_ref):
    c = _silu(cond_ref[...])
    o_ref[...] = _dot(c, w_ref[...], precision=HI) + b_ref[...]


def _ada(cond, ada_w, ada_b):
    depth = ada_w.shape[0]
    rows = cond.shape[0]
    tn = 1536
    return pl.pallas_call(
        _ada_kernel,
        grid=(depth, (N_MOD * D) // tn),
        in_specs=[pl.BlockSpec((rows, D), lambda l, n: (0, 0)),
                  pl.BlockSpec((None, D, tn), lambda l, n: (l, 0, n)),
                  pl.BlockSpec((None, 1, tn), lambda l, n: (l, 0, n))],
        out_specs=pl.BlockSpec((None, rows, tn), lambda l, n: (l, 0, n)),
        out_shape=jax.ShapeDtypeStruct((depth, rows, N_MOD * D), F32),
        compiler_params=_cparams(2, VMEM_LIMIT),
        name="ada_mod",
    )(cond, ada_w, ada_b.reshape(depth, 1, N_MOD * D))


def _pre_kernel(x_ref, mod_ref, g_ref, w_ref, cw_ref, cb_ref, csm_ref, p_ref, fl_ref, fc_ref, ut_ref):
    j = pl.program_id(1)
    x = x_ref[...]
    hn = _rms(x, g_ref[...]) * (1.0 + mod_ref[1:2, :]) + mod_ref[0:1, :]
    p = _dot(hn.astype(BF16), w_ref[...])
    p_ref[:, :C_XBC] = p[:, :C_XBC]
    p_ref[:, C_MISC:] = p[:, C_MISC:]
    ut_ref[...] = p[:, C_S5:C_S5 + DG].T.astype(BF16)

    xbc = p[:, C_XBC:C_MISC]
    period = jnp.where(j == 0, TM, GRID_W)
    pos = lax.broadcasted_iota(I32, (TM, 1), 0) & (period - 1)
    acc = jnp.zeros_like(xbc) + cb_ref[...]
    for k in range(SSD_K):
        d = k - SSD_K // 2
        sh = xbc if d == 0 else pltpu.roll(xbc, (TM - d) % TM, axis=0)
        ok = (pos + d >= 0) & (pos + d < period)
        acc = acc + jnp.where(ok, sh, 0.0) * cw_ref[k:k + 1, :]
    p_ref[:, C_XBC:C_MISC] = _silu(acc)

    xcs = _dot(p[:, C_FOUR:C_FOUR + DG].astype(BF16), csm_ref[...])

    @pl.when(j == 0)
    def _():
        fc_ref[0] = xcs[:, :DG].astype(BF16)
        fc_ref[1] = xcs[:, DG:].astype(BF16)

    @pl.when(j > 0)
    def _():
        fl_ref[0] = xcs[:, :DG].astype(BF16)
        fl_ref[1] = xcs[:, DG:].astype(BF16)


def _pre(x_all, mod, g, w_in_p, conv_w, conv_b, csm, seq, ctx):
    b, s, _ = x_all.shape
    nt = s // TM
    return pl.pallas_call(
        _pre_kernel,
        grid=(b, nt),
        in_specs=[pl.BlockSpec((None, TM, D), lambda i, j: (i, j, 0)),
                  pl.BlockSpec((None, None, 8, D), lambda i, j: (i, jnp.minimum(j, 1), 0, 0)),
                  pl.BlockSpec((1, D), lambda i, j: (0, 0)),
                  pl.BlockSpec((D, D_P), lambda i, j: (0, 0)),
                  pl.BlockSpec((8, 512), lambda i, j: (0, 0)),
                  pl.BlockSpec((1, 512), lambda i, j: (0, 0)),
                  pl.BlockSpec((DG, 2 * DG), lambda i, j: (0, 0))],
        out_specs=[pl.BlockSpec((None, TM, D_P), lambda i, j: (i, j, 0)),
                   pl.BlockSpec((None, 2, TM, DG), lambda i, j: (i, 0, jnp.maximum(j - 1, 0), 0)),
                   pl.BlockSpec((None, 2, ctx, DG), lambda i, j: (i, 0, 0, 0)),
                   pl.BlockSpec((DG, TM), lambda i, j: (0, i * nt + j))],
        out_shape=[jax.ShapeDtypeStruct((b, s, D_P), F32),
                   jax.ShapeDtypeStruct((b, 2, seq, DG), BF16),
                   jax.ShapeDtypeStruct((b, 2, ctx, DG), BF16),
                   jax.ShapeDtypeStruct((DG, b * s), BF16)],
        compiler_params=_cparams(2, VMEM_LIMIT),
        name="pre_mixer",
    )(x_all, mod, g, w_in_p, conv_w, conv_b, csm)


def _four_kernel(f_ref, x_ref, w_ref, o_ref):
    y = _dot(f_ref[...], x_ref[...])
    o_ref[...] = _dot(y.astype(BF16), w_ref[...])


def _fourier(fmat, x2, four_w):
    ln = fmat.shape[0]
    b = x2.shape[0]
    tf = min(512, ln)
    return pl.pallas_call(
        _four_kernel,
        grid=(ln // tf, b),
        in_specs=[pl.BlockSpec((tf, 2 * ln), lambda i, j: (i, 0)),
                  pl.BlockSpec((None, 2 * ln, DG), lambda i, j: (j, 0, 0)),
                  pl.BlockSpec((DG, DG), lambda i, j: (0, 0))],
        out_specs=pl.BlockSpec((None, tf, DG), lambda i, j: (j, i, 0)),
        out_shape=jax.ShapeDtypeStruct((b, ln, DG), F32),
        compiler_params=_cparams(2, VMEM_LIMIT),
        name="fourier_dft",
    )(fmat, x2, four_w)


def _dft_matrix(ln):
    f = jnp.arange(ln, dtype=I32)[:, None]
    t = jnp.arange(ln, dtype=I32)[None, :]
    ang = ((f * t) % ln).astype(F32) * (2.0 * math.pi / ln)
    scale = 1.0 / math.sqrt(ln * 64.0)
    return (jnp.concatenate([jnp.cos(ang), jnp.sin(ang)], axis=1) * scale).astype(BF16)


def _channel_dft():
    i = np.arange(DG)
    same = (i[:, None] // 64) == (i[None, :] // 64)
    ang = 2.0 * np.pi * ((i[:, None] % 64) * (i[None, :] % 64) % 64) / 64.0
    c = np.where(same, np.cos(ang), 0.0)
    s = np.where(same, np.sin(ang), 0.0)
    return jnp.asarray(np.concatenate([c, -s], axis=1), dtype=BF16)


def _s5_kernel(u_ref, kf_ref, kb_ref, p_ref, q_ref, a_ref, y_ref, toep, hlf, hlb, hef, heb,
               *, nb, nch, ncc):
    t_i = lax.broadcasted_iota(I32, (S5_T, S5_T), 0)
    j_i = lax.broadcasted_iota(I32, (S5_T, S5_T), 1)

    def build(h, c):
        for hp in range(S5_CH):
            vf = jnp.broadcast_to(kf_ref[pl.ds(h * S5_CH + hp, 1), :], (S5_T, S5_T))
            vb = jnp.broadcast_to(kb_ref[pl.ds(h * S5_CH + hp, 1), :], (S5_T, S5_T))
            blk = (jnp.where(j_i >= t_i, pltpu.roll(vf, 0, 1, stride=1, stride_axis=0), 0.0)
                   + jnp.where(j_i <= t_i, pltpu.roll(vb, 0, 1, stride=1, stride_axis=0), 0.0))
            toep[pl.ds(pl.multiple_of(h * S5_T, S5_T), S5_T), hp * S5_T:(hp + 1) * S5_T] = blk.astype(BF16)
        return c

    lax.fori_loop(0, S5_CH, build, 0)

    u = jnp.concatenate([u_ref[h] for h in range(S5_CH)], axis=1)
    hl = _dot(u, p_ref[...])
    hlf[...] = hl[:, :LANES]
    hlb[...] = hl[:, LANES:]
    a1f, a2f = a_ref[0:1, :LANES], a_ref[1:2, :LANES]
    a1b, a2b = a_ref[0:1, LANES:], a_ref[1:2, LANES:]

    def step(i, carry):
        sf, sb = carry
        nbk = jnp.where(i < ncc, ncc - 1 - i, nch + ncc - 1 - i)
        rows_f = pl.ds(i, nb, stride=nch)
        rows_b = pl.ds(nbk, nb, stride=nch)
        hef[rows_f, :] = sf
        heb[rows_b, :] = sb
        sf = a1f * sf + a2f * pltpu.roll(sf, 64, axis=1) + hlf[rows_f, :]
        sb = a1b * sb + a2b * pltpu.roll(sb, 64, axis=1) + hlb[rows_b, :]
        return sf, sb

    z = jnp.zeros((nb, LANES), F32)
    lax.fori_loop(0, nch, step, (z, z))
    he = jnp.concatenate([hef[...], heb[...]], axis=1).astype(BF16)
    y = _dot(u, toep[...]) + _dot(he, q_ref[...])
    for hp in range(S5_CH):
        y_ref[hp] = y[:, hp * S5_T:(hp + 1) * S5_T]


def _s5(ut3, kf, kb, ptab, qtab, atab, nb, nch, ncc):
    r = ut3.shape[1]
    w = S5_CH * S5_T
    grp = lambda rows, cols: pl.BlockSpec((None, rows, cols), lambda i: (i, 0, 0))
    return pl.pallas_call(
        functools.partial(_s5_kernel, nb=nb, nch=nch, ncc=ncc),
        grid=(S5_G,),
        in_specs=[pl.BlockSpec((S5_CH, r, S5_T), lambda i: (i, 0, 0)),
                  grp(S5_CH * S5_CH, S5_T), grp(S5_CH * S5_CH, S5_T),
                  grp(w, 2 * LANES), grp(2 * LANES, w), grp(8, 2 * LANES)],
        out_specs=pl.BlockSpec((S5_CH, r, S5_T), lambda i: (i, 0, 0)),
        out_shape=jax.ShapeDtypeStruct((DG, r, S5_T), F32),
        scratch_shapes=[pltpu.VMEM((w, w), BF16)] + [pltpu.VMEM((r, LANES), F32)] * 4,
        compiler_params=_cparams(1, VMEM_LIMIT),
        name="s5_mixer",
    )(ut3, kf, kb, ptab, qtab, atab)


def _s5_tables(lam_re, lam_im, log_dt, b_re, b_im, c_re, c_im):
    t = S5_T
    dt = jnp.exp(log_dt)[..., None]
    ldr, ldi = lam_re * dt, lam_im * dt
    ar_t = np.arange(t)

    def powers(r, exps):
        e = jnp.asarray(np.asarray(exps, np.float32))[None, :, None]
        mag = jnp.exp(ldr[r][:, None, :] * e)
        return mag * jnp.cos(ldi[r][:, None, :] * e), mag * jnp.sin(ldi[r][:, None, :] * e)

    ar, ai = (jnp.stack(z)[:, :, 0] for z in zip(powers(0, [1]), powers(1, [1])))
    den = lam_re * lam_re + lam_im * lam_im
    qr = ((ar - 1.0) * lam_re + ai * lam_im) / den
    qi = (ai * lam_re - (ar - 1.0) * lam_im) / den
    bbr = qr[..., None] * b_re - qi[..., None] * b_im
    bbi = qr[..., None] * b_im + qi[..., None] * b_re

    def c_pow(r, exps):
        pr, pi = powers(r, exps)
        return (c_re[r][:, None] * pr[:, :, None, :] - c_im[r][:, None] * pi[:, :, None, :],
                c_re[r][:, None] * pi[:, :, None, :] + c_im[r][:, None] * pr[:, :, None, :])

    def lag_kernels(r, lags):
        cr, ci = c_pow(r, lags)
        k = (jnp.einsum('gjop,gph->ghoj', cr, bbr[r], precision=HI)
             - jnp.einsum('gjop,gph->ghoj', ci, bbi[r], precision=HI))
        return k.reshape(S5_G, S5_CH * S5_CH, t)

    kf = lag_kernels(0, ar_t)
    kb = lag_kernels(1, (t - ar_t) % t)

    def ptab(r, exps):
        pr, pi = powers(r, exps)
        re = jnp.einsum('gtp,gph->ghtp', pr, bbr[r]) - jnp.einsum('gtp,gph->ghtp', pi, bbi[r])
        im = jnp.einsum('gtp,gph->ghtp', pr, bbi[r]) + jnp.einsum('gtp,gph->ghtp', pi, bbr[r])
        return jnp.concatenate([re, im], axis=-1).reshape(S5_G, S5_CH * t, 2 * S5_P)

    p2 = jnp.concatenate([ptab(0, t - 1 - ar_t), ptab(1, ar_t)], axis=-1)

    def qtab(r, exps):
        cr, ci = c_pow(r, exps)
        re = jnp.transpose(cr, (0, 3, 2, 1)).reshape(S5_G, S5_P, S5_CH * t)
        im = jnp.transpose(ci, (0, 3, 2, 1)).reshape(S5_G, S5_P, S5_CH * t)
        return jnp.concatenate([re, -im], axis=1)

    q2 = jnp.concatenate([qtab(0, ar_t + 1), qtab(1, t - ar_t)], axis=1)
    art, ait = (jnp.stack(z)[:, :, 0] for z in zip(powers(0, [t]), powers(1, [t])))
    a1 = jnp.concatenate([art[0], art[0], art[1], art[1]], axis=-1)
    a2 = jnp.concatenate([-ait[0], ait[0], -ait[1], ait[1]], axis=-1)
    atab = jnp.concatenate([a1[:, None], a2[:, None], jnp.zeros((S5_G, 6, 4 * S5_P), F32)], axis=1)
    return kf, kb, p2.astype(BF16), q2.astype(BF16), atab


def _split3(x):
    x1 = x.astype(BF16)
    r1 = x - x1.astype(F32)
    x2 = r1.astype(BF16)
    x3 = (r1 - x2.astype(F32)).astype(BF16)
    return x1, x2, x3


def _dot01(m, x):
    x1, x2, x3 = _split3(x)
    return (_dot(m, x3) + _dot(m, x2)) + _dot(m, x1)


def _tile_cumsum_mask(rev):
    t = lax.broadcasted_iota(I32, (TM, TM), 0)
    s = lax.broadcasted_iota(I32, (TM, TM), 1)
    order = (s >= t) if rev else (s <= t)
    return (order & ((t // CHUNK) == (s // CHUNK))).astype(BF16)


def _head_lane_mask(rev, width):
    t = lax.broadcasted_iota(I32, (CHUNK, width), 0)
    s = lax.broadcasted_iota(I32, (CHUNK, width), 1) % CHUNK
    return (s >= t) if rev else (s <= t)


def _gla_kernel(qk_ref, v_ref, misc_ref, a2_ref, ab_ref, o_ref, st_ref, *, rev):
    j = pl.program_id(1)

    @pl.when(j == 0)
    def _():
        st_ref[...] = jnp.zeros_like(st_ref)

    lane_k = lax.broadcasted_iota(I32, (1, LANES), 1) // GLA_DK
    lane_v = lax.broadcasted_iota(I32, (1, DG), 1) // GLA_DV
    causal = _head_lane_mask(rev, DG)
    st_diag = (lax.broadcasted_iota(I32, (DG, 1), 0) // GLA_DV) == lane_k

    z = _dot(misc_ref[...].astype(BF16), a2_ref[...]) + ab_ref[...]
    la_t = -(jnp.maximum(-z, 0.0) + jnp.log1p(jnp.exp(-jnp.abs(z)))) * (1.0 / GLA_TAU)
    b_t = _dot01(_tile_cumsum_mask(rev), la_t)
    st = st_ref[...]
    chunks = range(TM // CHUNK)
    for c in (reversed(chunks) if rev else chunks):
        rows = slice(c * CHUNK, (c + 1) * CHUNK)
        la, bcum = la_t[rows], b_t[rows]
        q = qk_ref[rows, :LANES] * (GLA_DK ** -0.5)
        k = qk_ref[rows, LANES:]
        v = v_ref[rows, :]
        tot = jnp.sum(la, axis=0, keepdims=True)
        q_in = (q * jnp.exp(bcum)).astype(BF16)
        k_in = k * jnp.exp(-bcum)
        k_dec = (k * jnp.exp(tot - bcum)).astype(BF16)
        k4 = jnp.concatenate([jnp.where(lane_k == h, k_in, 0.0) for h in range(GLA_H)], axis=0)
        att = jnp.where(causal, _dot(q_in, k4.astype(BF16), _NT), 0.0)
        vb = v.astype(BF16)
        vbd = jnp.concatenate([jnp.where(lane_v == h, vb, jnp.zeros_like(vb)) for h in range(GLA_H)], axis=0)
        o = _dot(att.astype(BF16), vbd) + _dot(q_in, jnp.where(st_diag, st, 0.0).astype(BF16), _NT)
        o_ref[rows, :] = o
        st = st * jnp.exp(tot) + _dot(vb, k_dec, _TN)
    st_ref[...] = st


def _tile_order(rev, j, nt):
    return jnp.where(j == 0, 0, nt - j) if rev else j


def _gla(p, a2m, abm, rev):
    b, s, _ = p.shape
    nt = s // TM
    return pl.pallas_call(
        functools.partial(_gla_kernel, rev=rev),
        grid=(b, nt),
        in_specs=[pl.BlockSpec((None, TM, DG), lambda i, j: (i, _tile_order(rev, j, nt), C_QK // DG)),
                  pl.BlockSpec((None, TM, DG), lambda i, j: (i, _tile_order(rev, j, nt), C_V // DG)),
                  pl.BlockSpec((None, TM, LANES), lambda i, j: (i, _tile_order(rev, j, nt), C_MISC // LANES)),
                  pl.BlockSpec((LANES, LANES), lambda i, j: (0, 0)),
                  pl.BlockSpec((1, LANES), lambda i, j: (0, 0))],
        out_specs=pl.BlockSpec((None, TM, DG), lambda i, j: (i, _tile_order(rev, j, nt), 0)),
        out_shape=jax.ShapeDtypeStruct((b, s, DG), F32),
        scratch_shapes=[pltpu.VMEM((DG, LANES), F32)],
        compiler_params=_cparams(2, VMEM_LIMIT),
        name="gla_bwd" if rev else "gla_fwd",
    )(p, p, p, a2m, abm)


def _ssd_kernel(xbc_ref, misc_ref, dtb_ref, av_ref, y_ref, sb_ref, *, rev):
    j = pl.program_id(1)

    @pl.when(j == 0)
    def _():
        sb_ref[...] = jnp.zeros_like(sb_ref)

    lane = lax.broadcasted_iota(I32, (1, LANES), 1)
    lane_g = lane // SSD_N
    lane_h = lax.broadcasted_iota(I32, (1, DG), 1) // SSD_HD
    row_h = lax.broadcasted_iota(I32, (DG, 1), 0) // SSD_HD
    blockmask = row_h == lane_h
    causal = _head_lane_mask(rev, DG)
    diag = (lax.broadcasted_iota(I32, (CHUNK, DG), 0)
            == lax.broadcasted_iota(I32, (CHUNK, DG), 1) % CHUNK)

    def per_head(x):
        xr = pltpu.roll(x, SSD_N, axis=1)
        return jnp.concatenate([jnp.where(lane < SSD_N, x, xr), jnp.where(lane < SSD_N, xr, x)], axis=1)

    zm = misc_ref[...] + dtb_ref[...]
    sp = jnp.maximum(zm, 0.0) + jnp.log1p(jnp.exp(-jnp.abs(zm)))
    dt_t = jnp.zeros((TM, DG), F32)
    for h in range(SSD_H):
        col = jnp.sum(jnp.where(lane == MISC_DT + SSD_H * int(rev) + h, sp, 0.0), axis=1, keepdims=True)
        dt_t = jnp.where(lane_h == h, col, dt_t)
    da_t = dt_t * av_ref[...]
    cum_t = _dot01(_tile_cumsum_mask(rev), da_t)

    sb = sb_ref[...]
    chunks = range(TM // CHUNK)
    for c in (reversed(chunks) if rev else chunks):
        rows = slice(c * CHUNK, (c + 1) * CHUNK)
        da, cum = da_t[rows], cum_t[rows]
        bmc = xbc_ref[rows, DG:DG + LANES]
        cmc = xbc_ref[rows, DG + LANES:]
        xdt = (xbc_ref[rows, :DG] * dt_t[rows]).astype(BF16)
        tot = jnp.sum(da, axis=0, keepdims=True)
        cum_row = jnp.sum(jnp.where(diag, cum, 0.0), axis=0, keepdims=True)
        decay = jnp.where(causal, jnp.exp(jnp.minimum(cum - cum_row, 0.0)), 0.0)
        b4 = jnp.concatenate([jnp.where(lane_g == h // (SSD_H // SSD_G), bmc, 0.0) for h in range(SSD_H)],
                             axis=0)
        scores = _dot(cmc.astype(BF16), b4.astype(BF16), _NT) * decay
        xbd = jnp.concatenate([jnp.where(lane_h == h, xdt, jnp.zeros_like(xdt)) for h in range(SSD_H)],
                              axis=0)
        cdec = per_head(cmc) * jnp.exp(cum)
        y = _dot(scores.astype(BF16), xbd) + _dot(cdec.astype(BF16), sb.astype(BF16))
        y_ref[rows, :] = y
        bdec = per_head(bmc) * jnp.exp(tot - cum)
        upd = _dot(bdec.astype(BF16), xdt, _TN)
        sb = sb * jnp.exp(tot) + jnp.where(blockmask, upd, 0.0)
    sb_ref[...] = sb


def _ssd(p, dtb, av, rev):
    b, s, _ = p.shape
    nt = s // TM
    return pl.pallas_call(
        functools.partial(_ssd_kernel, rev=rev),
        grid=(b, nt),
        in_specs=[pl.BlockSpec((None, TM, 512), lambda i, j: (i, _tile_order(rev, j, nt), C_XBC // 512)),
                  pl.BlockSpec((None, TM, LANES), lambda i, j: (i, _tile_order(rev, j, nt), C_MISC // LANES)),
                  pl.BlockSpec((1, LANES), lambda i, j: (0, 0)),
                  pl.BlockSpec((1, DG), lambda i, j: (0, 0))],
        out_specs=pl.BlockSpec((None, TM, DG), lambda i, j: (i, _tile_order(rev, j, nt), 0)),
        out_shape=jax.ShapeDtypeStruct((b, s, DG), F32),
        scratch_shapes=[pltpu.VMEM((DG, DG), F32)],
        compiler_params=_cparams(2, VMEM_LIMIT),
        name="ssd_bwd" if rev else "ssd_fwd",
    )(p, p, dtb, av)


def _ssd_tables(a_log, dt_bias, r):
    dtb = jnp.zeros((1, LANES), F32).at[0, MISC_DT + SSD_H * r:MISC_DT + SSD_H * (r + 1)].set(dt_bias[r])
    av = jnp.repeat(-jnp.exp(a_log[r]), SSD_HD)[None, :]
    return dtb, av


def _gelu_tanh(x):
    return 0.5 * x * (1.0 + jnp.tanh(math.sqrt(2.0 / math.pi) * (x + 0.044715 * x * x * x)))


def _post_kernel(x_ref, mod_ref, us5_ref, r_ref, z_ref, xs_ref, fl_ref, fc_ref, s5y_ref,
                 gf_ref, gb_ref, sf_ref, sbk_ref, vec_ref, wglu_ref, avg_ref, wout_ref,
                 gffn_ref, rw_ref, rb_ref, tri_ref, upper_ref,
                 xo_ref, loc_ref, ri_ref, rg_ref, tab_ref):
    j = pl.program_id(1)
    s5_d, gla_g, ssd_d, ssd_g = vec_ref[0:1, :], vec_ref[1:2, :], vec_ref[2:3, :], vec_ref[3:4, :]
    four = jnp.where(j == 0, fc_ref[...], fl_ref[...])
    y5 = s5y_ref[...].T + s5_d * us5_ref[...]
    act = _gelu_tanh(y5)
    s5o = act * jax.nn.sigmoid(_dot(act.astype(BF16), wglu_ref[...]))
    og = gf_ref[...] + gb_ref[...]
    sq = og * og
    sq_hi = sq.astype(BF16)
    sq_lo = (sq - sq_hi.astype(F32)).astype(BF16)
    ms = _dot(sq_lo, avg_ref[...]) + _dot(sq_hi, avg_ref[...])
    glo = og * lax.rsqrt(ms + EPS) * gla_g * _silu(r_ref[...])
    ys = (sf_ref[...] + sbk_ref[...] + ssd_d * xs_ref[...]) * _silu(z_ref[...])
    sso = _rms(ys, ssd_g)
    mix = jnp.concatenate([four, s5o, glo, sso], axis=1).astype(BF16)
    x = x_ref[...] + mod_ref[2:3, :] * _dot(mix, wout_ref[...])
    xo_ref[...] = x
    h = _rms(x, gffn_ref[...]) * (1.0 + mod_ref[4:5, :]) + mod_ref[3:4, :]

    h_hi = h.astype(BF16)
    h_lo = (h - h_hi.astype(F32)).astype(BF16)
    logits = ((_dot(h_hi, rw_ref[1]) + _dot(h_lo, rw_ref[0])) + _dot(h_hi, rw_ref[0])
              + rb_ref[...])
    lane = lax.broadcasted_iota(I32, (TM, LANES), 1)
    vals, idxs = [], []
    for _k in range(TOP_K):
        mx = jnp.max(logits, axis=1, keepdims=True)
        ix = jnp.min(jnp.where(logits == mx, lane, LANES), axis=1, keepdims=True)
        vals.append(mx)
        idxs.append(ix)
        logits = jnp.where(lane == ix, -jnp.inf, logits)
    es = [jnp.exp(v - vals[0]) for v in vals]
    den = es[0] + es[1] + es[2] + es[3]
    onehots = [(lane == ix) for ix in idxs]
    oh = sum(o.astype(F32) for o in onehots)

    ahead = _dot(tri_ref[...], oh.astype(BF16))
    cnt = jnp.sum(oh, axis=0, keepdims=True)
    cpad = jnp.floor((cnt + (SEG - 1.0)) * (1.0 / SEG)) * SEG
    lstart = _dot(jnp.broadcast_to(cpad, (8, LANES)).astype(BF16), upper_ref[...])[0:1]
    slot_of = lstart + ahead
    lane_r = lax.broadcasted_iota(I32, (1, LMAX), 1)
    ri = jnp.zeros((TM, LANES), I32)
    rg = jnp.zeros((TM, LANES), F32)
    pt = jnp.zeros((TM, LMAX), F32)
    for k in range(TOP_K):
        lpos = jnp.sum(jnp.where(onehots[k], slot_of, 0.0), axis=1, keepdims=True).astype(I32)
        pt = pt + (lane_r == lpos).astype(F32)
        ri = jnp.where(lane == k, idxs[k], ri)
        ri = jnp.where(lane == TOP_K + k, lpos, ri)
        rg = jnp.where(lane == k, es[k] / den, rg)
    ri_ref[...] = ri
    rg_ref[...] = rg
    loc_ref[...] = _pack_bf16_pairs(_dot(pt.astype(BF16), h_hi, _TN))
    row = lax.broadcasted_iota(I32, (8, LANES), 0)
    tab_ref[...] = jnp.where(row == 0, cnt.astype(I32), jnp.where(row == 1, lstart.astype(I32), 0))


def _post(x_all, mod, p, four_l, four_c, s5y, gla_f, gla_b, ssd_f, ssd_b, vecs, wglu, avg, wout, gffn,
          rw, rb, tri, upper, seq, ctx):
    b, s, _ = x_all.shape
    nt = s // TM
    tile = lambda w, col: pl.BlockSpec((None, TM, w), lambda i, j: (i, j, col))
    full = lambda a: pl.BlockSpec(a.shape, lambda i, j: (0,) * a.ndim)
    per_tile = lambda r, w: pl.BlockSpec((None, None, r, w), lambda i, j: (i, j, 0, 0))
    return pl.pallas_call(
        _post_kernel,
        grid=(b, nt),
        in_specs=[tile(D, 0),
                  pl.BlockSpec((None, None, 8, D), lambda i, j: (i, jnp.minimum(j, 1), 0, 0)),
                  tile(DG, C_S5 // DG), tile(DG, C_R // DG), tile(DG, C_Z // DG), tile(DG, C_XBC // DG),
                  pl.BlockSpec((None, TM, DG), lambda i, j: (i, jnp.maximum(j - 1, 0), 0)),
                  pl.BlockSpec((None, ctx, DG), lambda i, j: (i, 0, 0)),
                  pl.BlockSpec((DG, TM), lambda i, j: (0, i * nt + j)),
                  tile(DG, 0), tile(DG, 0), tile(DG, 0), tile(DG, 0),
                  full(vecs), full(wglu), full(avg), full(wout), full(gffn), full(rw), full(rb), full(tri),
                  full(upper)],
        out_specs=[tile(D, 0), per_tile(LMAX, D // 2), tile(LANES, 0), tile(LANES, 0), per_tile(8, LANES)],
        out_shape=[jax.ShapeDtypeStruct((b, s, D), F32),
                   jax.ShapeDtypeStruct((b, nt, LMAX, D // 2), U32),
                   jax.ShapeDtypeStruct((b, s, LANES), I32),
                   jax.ShapeDtypeStruct((b, s, LANES), F32),
                   jax.ShapeDtypeStruct((b, nt, 8, LANES), I32)],
        compiler_params=_cparams(2, VMEM_LIMIT),
        name="post_mixer_router",
    )(x_all, mod, p, p, p, p, four_l, four_c, s5y, gla_f, gla_b, ssd_f, ssd_b,
      vecs, wglu, avg, wout, gffn, rw, rb, tri, upper)


T_NSEG, T_LSTART, T_GSTART, T_NBIG = 0, 32, 64, 96
BIG = 32


def _segment_copies(tab_ref, make_copy, action):
    def per_expert(e, c):
        lstart = pl.multiple_of(tab_ref[0, 0, T_LSTART + e], SEG)
        gstart = pl.multiple_of(tab_ref[0, 0, T_GSTART + e], SEG)
        nbig = tab_ref[0, 0, T_NBIG + e]

        def big_piece(q, c2):
            action(make_copy(lstart + q * BIG, gstart + q * BIG, BIG))
            return c2

        def seg_piece(q, c2):
            action(make_copy(lstart + nbig * BIG + q * SEG, gstart + nbig * BIG + q * SEG, SEG))
            return c2
        c = lax.fori_loop(0, nbig, big_piece, c)
        return lax.fori_loop(0, tab_ref[0, 0, T_NSEG + e], seg_piece, c)
    lax.fori_loop(0, N_EXP, per_expert, 0)


E_END, E_NUSED = 0, 32


def _zero_fills(ends_ref, nblk, make_fill, action):
    def per_expert(e, c):
        end = ends_ref[0, 0, E_END + e]
        prev = jnp.where(e > 0, ends_ref[0, 0, E_END + jnp.maximum(e - 1, 0)], 0)

        @pl.when(end > prev)
        def _():
            action(make_fill(pl.multiple_of(end - BM, BM)))
        return c
    lax.fori_loop(0, N_EXP, per_expert, 0)

    def per_block(b, c):
        action(make_fill(pl.multiple_of(b * BM, BM)))
        return c
    lax.fori_loop(ends_ref[0, 0, E_NUSED], nblk, per_block, 0)


def _dispatch_kernel(ends_ref, tab_ref, loc_ref, buf_ref, zeros, sem, zsem, *, nblk):
    @pl.when(pl.program_id(0) == 0)
    def _():
        zeros[...] = jnp.zeros_like(zeros)
        make_fill = lambda grow: pltpu.make_async_copy(zeros, buf_ref.at[pl.ds(grow, BM)], zsem)
        _zero_fills(ends_ref, nblk, make_fill, lambda cp: cp.start())
        _zero_fills(ends_ref, nblk, make_fill, lambda cp: cp.wait())

    def make_copy(lrow, grow, rows):
        return pltpu.make_async_copy(loc_ref.at[0, pl.ds(lrow, rows)], buf_ref.at[pl.ds(grow, rows)], sem)

    _segment_copies(tab_ref, make_copy, lambda cp: cp.start())
    _segment_copies(tab_ref, make_copy, lambda cp: cp.wait())


def _dispatch(ends3, tab3, loc, n_rows):
    ntile = loc.shape[0]
    return pl.pallas_call(
        functools.partial(_dispatch_kernel, nblk=n_rows // BM),
        grid=(ntile,),
        in_specs=[pl.BlockSpec((1, 1, LANES), lambda i: (0, 0, 0), memory_space=pltpu.SMEM),
                  pl.BlockSpec((1, 1, LANES), lambda i: (i, 0, 0), memory_space=pltpu.SMEM),
                  pl.BlockSpec((1, LMAX, D // 2), lambda i: (i, 0, 0))],
        out_specs=pl.BlockSpec(memory_space=pl.ANY),
        out_shape=jax.ShapeDtypeStruct((n_rows, D // 2), U32),
        scratch_shapes=[pltpu.VMEM((BM, D // 2), U32), pltpu.SemaphoreType.DMA(()),
                        pltpu.SemaphoreType.DMA(())],
        compiler_params=_cparams(1, VMEM_LIMIT),
        name="moe_dispatch",
    )(ends3, tab3, loc)


def _expert_kernel(be_ref, nu_ref, x_ref, wgu_ref, wd_ref, bg_ref, bu_ref, bd_ref, perm_ref, y_ref,
                   wg_sc, wu_sc, wd_sc):
    i = pl.program_id(0)
    used = i < nu_ref[0]
    new_expert = (i == 0) | (be_ref[i] != be_ref[jnp.maximum(i - 1, 0)])

    @pl.when(used & new_expert)
    def _():
        for c in range(2 * D // DG):
            d = _dot(wgu_ref[:, c * DG:(c + 1) * DG].astype(BF16), perm_ref[...])
            wg_sc[:, c * LANES:(c + 1) * LANES] = d[:, :LANES].astype(BF16)
            wu_sc[:, c * LANES:(c + 1) * LANES] = d[:, LANES:].astype(BF16)
        wd_sc[...] = wd_ref[...].astype(BF16)

    @pl.when(used)
    def _():
        x = _unpack_bf16_pairs(x_ref[...]).astype(BF16)
        g = _dot(x, wg_sc[...]) + bg_ref[...]
        u = _dot(x, wu_sc[...]) + bu_ref[...]
        gate = jnp.minimum(g, LIMIT)
        up = jnp.clip(u, -LIMIT, LIMIT)
        act = (up + 1.0) * gate * jax.nn.sigmoid(ALPHA * gate)
        y = _dot(act.astype(BF16), wd_sc[...]) + bd_ref[...]
        y_ref[...] = _pack_bf16_pairs(y)

    @pl.when(jnp.logical_not(used))
    def _():
        y_ref[...] = jnp.zeros_like(y_ref)


def _deinterleave_perm():
    pm = np.zeros((DG, DG), np.float32)
    o = np.arange(LANES)
    pm[2 * o, o] = 1.0
    pm[2 * o + 1, LANES + o] = 1.0
    return jnp.asarray(pm, BF16)


def _experts(blk_e, n_used, buf, wgu_all, wd_all, layer, bg, bu, bd):
    nblk = buf.shape[0] // BM
    row = lambda i, be, nu: (jnp.minimum(i, nu[0] - 1), 0)
    wsel = lambda i, be, nu: (be[jnp.minimum(i, nu[0] - 1)], 0, 0)
    wsel_l = lambda i, be, nu: (layer, be[jnp.minimum(i, nu[0] - 1)], 0, 0)
    grid_spec = pltpu.PrefetchScalarGridSpec(
        num_scalar_prefetch=2,
        grid=(nblk,),
        in_specs=[pl.BlockSpec((BM, D // 2), row),
                  pl.BlockSpec((None, None, D, 2 * D), wsel_l), pl.BlockSpec((None, None, D, D), wsel_l),
                  pl.BlockSpec((None, 1, D), wsel), pl.BlockSpec((None, 1, D), wsel),
                  pl.BlockSpec((None, 1, D), wsel),
                  pl.BlockSpec((DG, DG), lambda i, be, nu: (0, 0))],
        out_specs=pl.BlockSpec((BM, D // 2), lambda i, be, nu: (i, 0)),
        scratch_shapes=[pltpu.VMEM((D, D), BF16), pltpu.VMEM((D, D), BF16), pltpu.VMEM((D, D), BF16)],
    )
    return pl.pallas_call(
        _expert_kernel,
        grid_spec=grid_spec,
        out_shape=jax.ShapeDtypeStruct(buf.shape, U32),
        compiler_params=_cparams(1, VMEM_LIMIT),
        name="moe_experts",
    )(blk_e, n_used, buf, wgu_all, wd_all, bg, bu, bd, _deinterleave_perm())


def _combine_kernel(tab_ref, tabn_ref, x_ref, ri_ref, rg_ref, mod_ref, fg_ref, yb_ref, o_ref, yloc, sem,
                    *, final):
    i = pl.program_id(0)
    n = pl.num_programs(0)
    slot = i % 2

    def make_copy(sl):
        return lambda lrow, grow, rows: pltpu.make_async_copy(
            yb_ref.at[pl.ds(grow, rows)], yloc.at[sl, pl.ds(lrow, rows)], sem.at[sl])

    @pl.when(i == 0)
    def _():
        yloc[...] = jnp.zeros_like(yloc)
        _segment_copies(tab_ref, make_copy(0), lambda cp: cp.start())

    @pl.when(i + 1 < n)
    def _():
        _segment_copies(tabn_ref, make_copy(1 - slot), lambda cp: cp.start())

    _segment_copies(tab_ref, make_copy(slot), lambda cp: cp.wait())

    lane_r = lax.broadcasted_iota(I32, (1, LMAX), 1)
    g = jnp.zeros((TM, LMAX), F32)
    for k in range(TOP_K):
        g = g + jnp.where(lane_r == ri_ref[:, TOP_K + k:TOP_K + k + 1], rg_ref[:, k:k + 1], 0.0)
    g_hi = g.astype(BF16)
    g_lo = (g - g_hi.astype(F32)).astype(BF16)
    y = _unpack_bf16_pairs(yloc[slot]).astype(BF16)
    f = _dot(g_hi, y) + _dot(g_lo, y)
    x = x_ref[...] + mod_ref[5:6, :] * f
    o_ref[...] = _rms(x, fg_ref[...]) if final else x


def _combine(tab3, x_flat, ri_flat, rg_flat, mod, fg, ybuf, nt, final):
    n = x_flat.shape[0]
    steps = n // TM
    tspec = lambda f: pl.BlockSpec((1, 1, LANES), f, memory_space=pltpu.SMEM)
    return pl.pallas_call(
        functools.partial(_combine_kernel, final=final),
        grid=(steps,),
        in_specs=[tspec(lambda i: (i, 0, 0)),
                  tspec(lambda i: (jnp.minimum(i + 1, steps - 1), 0, 0)),
                  pl.BlockSpec((TM, D), lambda i: (i, 0)),
                  pl.BlockSpec((TM, LANES), lambda i: (i, 0)),
                  pl.BlockSpec((TM, LANES), lambda i: (i, 0)),
                  pl.BlockSpec((None, None, 8, D), lambda i: (i // nt, jnp.minimum(i % nt, 1), 0, 0)),
                  pl.BlockSpec((1, D), lambda i: (0, 0)),
                  pl.BlockSpec(memory_space=pl.ANY)],
        out_specs=pl.BlockSpec((TM, D), lambda i: (i, 0)),
        out_shape=jax.ShapeDtypeStruct((n, D), F32),
        scratch_shapes=[pltpu.VMEM((2, LMAX, D // 2), U32), pltpu.SemaphoreType.DMA((2,))],
        compiler_params=_cparams(1, VMEM_LIMIT),
        name="moe_combine",
    )(tab3, tab3, x_flat, ri_flat, rg_flat, mod, fg, ybuf)


def _repack_w_in(w):
    four, s5, q, k, v, r, a1, z, xbc, dt = jnp.split(
        w, np.cumsum([256, 256, 128, 128, 256, 256, 32, 256, 512, 8])[:-1].tolist(), axis=-1)
    misc = jnp.concatenate([a1, dt, jnp.zeros((D, LANES - 40), w.dtype)], axis=-1)
    return jnp.concatenate([four, s5, q, k, v, r, z, xbc, misc], axis=-1).astype(BF16)


def kernel(x, c, ctx, c_ctx, ada_w, ada_b, norm_mix_g, norm_ffn_g, w_in, w_out, four_w, s5_lam_re, s5_lam_im, s5_log_dt, s5_b_re, s5_b_im, s5_c_re, s5_c_im, s5_d, s5_w_glu, gla_a2, gla_a_b, gla_norm_g, ssd_conv_w, ssd_conv_b, ssd_a_log, ssd_dt_bias, ssd_d, ssd_norm_g, router_w, router_b, exp_w_gu, exp_b_gu, exp_w_dn, exp_b_dn, final_g):
    bsz, seq, _ = x.shape
    clen = ctx.shape[1]
    depth = ada_w.shape[0]
    assert clen == TM and seq % TM == 0 and bsz == 8
    s = clen + seq
    nt = s // TM
    n_tok = bsz * s
    nch, ncc = s // S5_T, clen // S5_T

    cond = jnp.concatenate([c, c_ctx[None, :], jnp.zeros((16 - bsz - 1, D), F32)], axis=0)
    mods = _ada(cond, ada_w, ada_b).reshape(depth, 16, N_MOD, D)
    m_lat = mods[:, :bsz]
    m_ctx = jnp.broadcast_to(mods[:, bsz:bsz + 1], m_lat.shape)
    mods = jnp.stack([m_ctx, m_lat], axis=2)
    mods = jnp.pad(mods, ((0, 0), (0, 0), (0, 0), (0, 2), (0, 0)))

    x_all = jnp.concatenate([ctx, x], axis=1)
    csm = _channel_dft()
    f_lat, f_ctx = _dft_matrix(seq), _dft_matrix(clen)
    avg = jnp.asarray(np.kron(np.eye(GLA_H), np.full((GLA_DV, GLA_DV), 1.0 / GLA_DV)), BF16)
    tri = jnp.asarray(np.tril(np.ones((TM, TM)), -1), BF16)
    upper = jnp.asarray(np.triu(np.ones((LANES, LANES)), 1), BF16)

    ntile = n_tok // TM
    nblk = -(-(n_tok * TOP_K + ntile * N_EXP * (SEG - 1) + N_EXP * (BM - 1)) // BM)
    n_rows = nblk * BM

    for l in range(depth):
        mod = mods[l]
        w_in_p = _repack_w_in(w_in[l])
        conv_w = jnp.pad(ssd_conv_w[l], ((0, 8 - SSD_K), (0, 0)))
        p, four_l, four_c, ut = _pre(x_all, mod, norm_mix_g[l][None], w_in_p, conv_w, ssd_conv_b[l][None],
                                     csm, seq, clen)
        fw = four_w[l].astype(BF16)
        fy_l = _fourier(f_lat, four_l.reshape(bsz, 2 * seq, DG), fw)
        fy_c = _fourier(f_ctx, four_c.reshape(bsz, 2 * clen, DG), fw)

        s5_tabs = _s5_tables(s5_lam_re[l], s5_lam_im[l], s5_log_dt[l], s5_b_re[l], s5_b_im[l],
                                    s5_c_re[l], s5_c_im[l])
        y5 = _s5(ut.reshape(DG, n_tok // S5_T, S5_T), *s5_tabs, bsz, nch, ncc).reshape(DG, n_tok)

        gla_o, ssd_y = [], []
        for r in range(2):
            a2m = jnp.zeros((LANES, LANES), F32).at[GLA_RANK * r:GLA_RANK * (r + 1)].set(gla_a2[l, r])
            gla_o.append(_gla(p, a2m.astype(BF16), gla_a_b[l, r][None, :], rev=bool(r)))
            ssd_y.append(_ssd(p, *_ssd_tables(ssd_a_log[l], ssd_dt_bias[l], r), rev=bool(r)))

        vecs = jnp.stack([s5_d[l], jnp.tile(gla_norm_g[l], GLA_H), jnp.repeat(ssd_d[l], SSD_HD),
                          ssd_norm_g[l]] + [jnp.zeros((DG,), F32)] * 4)
        rw = jnp.pad(router_w[l], ((0, 0), (0, LANES - N_EXP)))
        rw_hi = rw.astype(BF16)
        rw = jnp.stack([rw_hi, (rw - rw_hi.astype(F32)).astype(BF16)])
        rb = jnp.concatenate([router_b[l], jnp.full((LANES - N_EXP,), -1e30, F32)])[None]
        x_mid, loc, ri, rg, tab = _post(x_all, mod, p, fy_l, fy_c, y5, gla_o[0], gla_o[1], ssd_y[0], ssd_y[1],
                                        vecs, s5_w_glu[l].astype(BF16), avg, w_out[l].astype(BF16),
                                        norm_ffn_g[l][None], rw, rb, tri, upper, seq, clen)

        tab = tab.reshape(ntile, 8, LANES)
        cnt, lstart = tab[:, 0, :N_EXP], tab[:, 1, :N_EXP]
        cpad = (cnt + SEG - 1) // SEG * SEG
        in_expert = jnp.cumsum(cpad, axis=0) - cpad
        region = (jnp.sum(cpad, axis=0) + BM - 1) // BM * BM
        pad_end = jnp.cumsum(region)
        gstart = (pad_end - region)[None, :] + in_expert
        tab3 = jnp.concatenate([(cpad % BIG) // SEG, lstart, gstart, cpad // BIG], axis=1)
        tab3 = tab3.astype(I32).reshape(ntile, 1, LANES)
        blk_start = jnp.arange(nblk, dtype=I32) * BM
        blk_e = jnp.minimum(jnp.sum((pad_end[None, :] <= blk_start[:, None]).astype(I32), axis=1), N_EXP - 1)
        n_used = (pad_end[-1] // BM).astype(I32)[None]

        ends3 = jnp.concatenate([pad_end.astype(I32), n_used, jnp.zeros((LANES - N_EXP - 1,), I32)])
        buf = _dispatch(ends3.reshape(1, 1, LANES), tab3, loc.reshape(ntile, LMAX, D // 2), n_rows)
        bgu = exp_b_gu[l].reshape(N_EXP, 1, D, 2)
        ybuf = _experts(blk_e, n_used, buf, exp_w_gu, exp_w_dn, l, bgu[..., 0], bgu[..., 1],
                        exp_b_dn[l][:, None, :])
        x_all = _combine(tab3, x_mid.reshape(n_tok, D), ri.reshape(n_tok, LANES), rg.reshape(n_tok, LANES),
                         mod, final_g[None], ybuf, nt, final=(l == depth - 1)).reshape(bsz, s, D)

    return x_all[:, clen:, :]
```
